```python
import math
import jax
import jax.numpy as jnp
from jax import lax
import numpy as np

D_MODEL = 1024
BATCH = 2
SEQ = 8192
DEPTH = 1
DEC_BATCH = 128
DEC_SEQ = 4
PAST_LEN = 8192
PAGE_SIZE = 128

FOX_HEADS = 8
FOX_HEAD_DIM = 64
FOX_WIDTH = FOX_HEADS * FOX_HEAD_DIM
FOX_BLOCK = 128
FOX_FORGET_BIAS = 2.0
HG_HEADS = 4
HG_KDIM = 128
HG_VDIM = 128
HG_KWIDTH = HG_HEADS * HG_KDIM
HG_VWIDTH = HG_HEADS * HG_VDIM
HG_CHUNK = 32
N_EXPERTS = 256
TOP_K = 8
EXPERT_DIM = 256
SHARED_DIM = 256
ROUTED_SCALE = 2.5
ROUTER_BIAS_SCALE = 0.01
MOE_BLOCK = 128
DEEPNORM_ALPHA = (2.0 * DEPTH) ** 0.25
DEEPNORM_BETA = (8.0 * DEPTH) ** -0.25
LN_EPS = 1e-5
RMS_EPS = 1e-6
NEG_INF = -1e30
IN_SIZES = (FOX_WIDTH, FOX_WIDTH, FOX_WIDTH, FOX_HEADS, HG_KWIDTH, HG_KWIDTH, HG_VWIDTH, HG_VWIDTH, D_MODEL, D_MODEL)
IN_COL_SCALES = (1.0, 1.0, DEEPNORM_BETA, 1.0, 1.0, 1.0, DEEPNORM_BETA, 1.0, 1.0, 1.0)
IN_COLS = sum(IN_SIZES)
IN_OFFSETS = tuple(int(v) for v in np.cumsum(IN_SIZES)[:-1])

kernel_name = 'fox_hgrn2_moe_deepnorm_adaln_step'


def _layer_norm(x, g=None, b=None):
    x32 = x.astype(jnp.float32)
    mu = jnp.mean(x32, axis=-1, keepdims=True)
    var = jnp.mean(jnp.square(x32 - mu), axis=-1, keepdims=True)
    y = (x32 - mu) * lax.rsqrt(var + LN_EPS)
    if g is not None:
        y = y * g.astype(jnp.float32) + b.astype(jnp.float32)
    return y.astype(x.dtype)


def _rms_norm(x, g):
    x32 = x.astype(jnp.float32)
    return x32 * lax.rsqrt(jnp.mean(jnp.square(x32), axis=-1, keepdims=True) + RMS_EPS) * g.astype(jnp.float32)


def _adaln(c, w_ada, b_ada):
    m = jax.nn.silu(c) @ w_ada + b_ada
    return jnp.split(m[:, None, :], 6, axis=-1)


def _modulate(x, shift, scale):
    return _layer_norm(x) * (1.0 + scale) + shift


def _swiglu(x, wg, wu, wd):
    return (jax.nn.silu(x @ wg) * (x @ wu)) @ wd


def _fox_prompt(q, k, v, logf):
    B, T, H, Dh = q.shape
    cum_k = jnp.transpose(jnp.cumsum(logf, axis=1), (0, 2, 1))
    nb = T // FOX_BLOCK
    qb = q.reshape(B, nb, FOX_BLOCK, H, Dh).transpose(1, 0, 2, 3, 4)
    cqb = cum_k.reshape(B, H, nb, FOX_BLOCK).transpose(2, 0, 1, 3)
    pos_k = jnp.arange(T)
    scale = FOX_HEAD_DIM ** -0.5

    def block(args):
        qi, ci, bidx = args
        pos_q = bidx * FOX_BLOCK + jnp.arange(FOX_BLOCK)
        s = jnp.einsum('bqhd,bkhd->bhqk', qi, k, preferred_element_type=jnp.float32) * scale
        logits = s + ci[..., None] - cum_k[:, :, None, :]
        logits = jnp.where(pos_k[None, :] <= pos_q[:, None], logits, NEG_INF)
        p = jax.nn.softmax(logits, axis=-1)
        return jnp.einsum('bhqk,bkhd->bqhd', p.astype(v.dtype), v)

    o = lax.map(block, (qb, cqb, jnp.arange(nb)))
    return o.transpose(1, 0, 2, 3, 4).reshape(B, T, H, Dh)


def _fox_sample(q, k, v, logf, cache_k, cache_v, cache_logf, page_table):
    Bd, T, H, Dh = q.shape
    past = page_table.shape[1] * cache_k.shape[1]
    L = past + T
    mask = jnp.arange(L)[None, :] <= (past + jnp.arange(T))[:, None]
    scale = FOX_HEAD_DIM ** -0.5

    def one(args):
        qi, ki, vi, fi, pages = args
        k_all = jnp.concatenate([cache_k[pages].reshape(past, H, Dh), ki], axis=0)
        v_all = jnp.concatenate([cache_v[pages].reshape(past, H, Dh), vi], axis=0)
        f_all = jnp.concatenate([cache_logf[pages].reshape(past, H).astype(jnp.float32), fi], axis=0)
        cum = jnp.cumsum(f_all, axis=0)
        s = jnp.einsum('qhd,khd->hqk', qi, k_all, preferred_element_type=jnp.float32) * scale
        logits = s + cum[past:].T[:, :, None] - cum.T[:, None, :]
        logits = jnp.where(mask, logits, NEG_INF)
        p = jax.nn.softmax(logits, axis=-1)
        return jnp.einsum('hqk,khd->qhd', p.astype(v_all.dtype), v_all)

    return lax.map(one, (q, k, v, logf, page_table))


def _hgrn2(q, f_raw, i, lb, S0):
    B, T, H, dk = q.shape
    dv = i.shape[-1]
    C = math.gcd(T, HG_CHUNK)
    nc = T // C
    f = lb + (1.0 - lb) * jax.nn.sigmoid(f_raw.astype(jnp.float32))
    logf = jnp.log(f)
    k = 1.0 - f

    def chunks(a):
        return a.reshape(B, nc, C, H, a.shape[-1]).transpose(1, 0, 3, 2, 4)

    qc, kc, ic, gc = chunks(q.astype(jnp.float32)), chunks(k), chunks(i.astype(jnp.float32)), chunks(logf)
    g = jnp.cumsum(gc, axis=3)
    g_last = g[:, :, :, -1:, :]
    q_dec = qc * jnp.exp(g)
    k_inv = kc * jnp.exp(-g)
    k_to_end = kc * jnp.exp(g_last - g)
    chunk_decay = jnp.exp(g_last[:, :, :, 0, :])
    causal = jnp.tril(jnp.ones((C, C), dtype=bool))
    A = jnp.where(causal, jnp.einsum('nbhcd,nbhsd->nbhcs', q_dec, k_inv), 0.0)
    o_intra = jnp.einsum('nbhcs,nbhsv->nbhcv', A, ic)

    def step(S, xs):
        qd, kte, iv, dec = xs
        o_inter = jnp.einsum('bhcd,bhdv->bhcv', qd, S)
        S = dec[..., None] * S + jnp.einsum('bhcd,bhcv->bhdv', kte, iv)
        return S, o_inter

    S_final, o_inter = lax.scan(step, S0.astype(jnp.float32), (q_dec, k_to_end, ic, chunk_decay))
    o = (o_intra + o_inter).transpose(1, 0, 3, 2, 4).reshape(B, T, H, dv)
    return o, S_final


def _moe(h, lp):
    n = h.shape[0]
    scores = jax.nn.sigmoid((h @ lp['w_router']).astype(jnp.float32))
    _, idx = lax.top_k(scores + lp['router_bias'].astype(jnp.float32), TOP_K)
    w_sel = jnp.take_along_axis(scores, idx, axis=1)
    w_sel = ROUTED_SCALE * w_sel / jnp.sum(w_sel, axis=1, keepdims=True)
    n_pairs = n * TOP_K
    flat_e = idx.reshape(n_pairs)
    order = jnp.argsort(flat_e)
    sorted_e = flat_e[order]
    counts = jnp.bincount(flat_e, length=N_EXPERTS)
    padded = (counts + MOE_BLOCK - 1) // MOE_BLOCK * MOE_BLOCK
    start = jnp.cumsum(counts) - counts
    pend = jnp.cumsum(padded)
    pstart = pend - padded
    dest = pstart[sorted_e] + jnp.arange(n_pairs) - start[sorted_e]
    n_blocks = (n_pairs + N_EXPERTS * (MOE_BLOCK - 1) + MOE_BLOCK - 1) // MOE_BLOCK
    n_rows = n_blocks * MOE_BLOCK
    row_tok = jnp.zeros((n_rows,), jnp.int32).at[dest].set((order // TOP_K).astype(jnp.int32))
    row_w = jnp.zeros((n_rows,), jnp.float32).at[dest].set(w_sel.reshape(n_pairs)[order])
    block_e = jnp.minimum(jnp.searchsorted(pend, jnp.arange(n_blocks) * MOE_BLOCK, side='right'), N_EXPERTS - 1)
    xr = h[row_tok].reshape(n_blocks, MOE_BLOCK, D_MODEL)

    def expert_block(args):
        xb, e = args
        return _swiglu(xb, lp['w_exp_gate'][e], lp['w_exp_up'][e], lp['w_exp_down'][e])

    yr = lax.map(expert_block, (xr, block_e)).reshape(n_rows, D_MODEL)
    routed = jnp.zeros((n, D_MODEL), jnp.float32).at[row_tok].add(yr.astype(jnp.float32) * row_w[:, None])
    shared = _swiglu(h, lp['w_sh_gate'], lp['w_sh_up'], lp['w_sh_down']).astype(jnp.float32)
    return (routed + shared).astype(h.dtype)


def _layer(x, c, lp, lb, hg_state0, fox_attend):
    B, T, _ = x.shape
    sh1, sc1, g1, sh2, sc2, g2 = _adaln(c, lp['w_ada'], lp['b_ada'])
    h = _modulate(x, sh1, sc1)
    fq, fk, fv, ff, hq, hf, hi, hog, ga, gb = jnp.split(h @ lp['w_in'], IN_OFFSETS, axis=-1)
    fq = fq.reshape(B, T, FOX_HEADS, FOX_HEAD_DIM)
    fk = fk.reshape(B, T, FOX_HEADS, FOX_HEAD_DIM)
    fv = fv.reshape(B, T, FOX_HEADS, FOX_HEAD_DIM)
    logf = jax.nn.log_sigmoid((ff + lp['b_fox_f']).astype(jnp.float32))
    o_a = fox_attend(fq, fk, fv, logf).reshape(B, T, FOX_WIDTH)
    o_b, S_new = _hgrn2(jax.nn.silu(hq).reshape(B, T, HG_HEADS, HG_KDIM),
                        hf.reshape(B, T, HG_HEADS, HG_KDIM),
                        hi.reshape(B, T, HG_HEADS, HG_VDIM), lb, hg_state0)
    o_b = (_rms_norm(o_b, lp['hg_norm_g']) * jax.nn.silu(hog.reshape(B, T, HG_HEADS, HG_VDIM).astype(jnp.float32)))
    o_b = o_b.astype(x.dtype).reshape(B, T, HG_VWIDTH)
    mixed = jax.nn.sigmoid(ga) * (o_a @ lp['w_branch_a']) + jax.nn.sigmoid(gb) * (o_b @ lp['w_branch_b'])
    x = _layer_norm(DEEPNORM_ALPHA * x + g1 * (mixed @ lp['w_out']), lp['ln1_g'], lp['ln1_b'])
    h2 = _modulate(x, sh2, sc2)
    y = _moe(h2.reshape(B * T, D_MODEL), lp).reshape(B, T, D_MODEL)
    x = _layer_norm(DEEPNORM_ALPHA * x + g2 * y, lp['ln2_g'], lp['ln2_b'])
    return x, fk, fv, logf, S_new


def setup_inputs(seed: int = 0) -> dict:
    key = jax.random.key(seed)
    ks = iter(jax.random.split(key, 40))

    def nrm(shape, scale):
        return scale * jax.random.normal(next(ks), shape, jnp.float32)

    n_pages = PAST_LEN // PAGE_SIZE
    n_used = DEC_BATCH * n_pages
    n_phys = n_used + max(1, n_used // 4)
    col_scale = jnp.asarray(np.concatenate([np.full((s,), sc, np.float32) for s, sc in zip(IN_SIZES, IN_COL_SCALES)]))
    D = D_MODEL
    inp = {}
    inp['x_prompt'] = nrm((BATCH, SEQ, D), 1.0)
    inp['x_sample'] = nrm((DEC_BATCH, DEC_SEQ, D), 1.0)
    inp['c_prompt'] = nrm((BATCH, D), 1.0)
    inp['c_sample'] = nrm((DEC_BATCH, D), 1.0)
    inp['cache_k'] = nrm((DEPTH, n_phys, PAGE_SIZE, FOX_HEADS, FOX_HEAD_DIM), 1.0)
    inp['cache_v'] = nrm((DEPTH, n_phys, PAGE_SIZE, FOX_HEADS, FOX_HEAD_DIM), 1.0)
    inp['cache_logf'] = jax.nn.log_sigmoid(FOX_FORGET_BIAS + nrm((DEPTH, n_phys, PAGE_SIZE, FOX_HEADS), 1.0))
    inp['state_hg'] = nrm((DEPTH, DEC_BATCH, HG_HEADS, HG_KDIM, HG_VDIM), 0.3)
    inp['page_table'] = jax.random.permutation(next(ks), n_phys)[:n_used].reshape(DEC_BATCH, n_pages).astype(jnp.int32)
    inp['w_ada'] = nrm((DEPTH, D, 6 * D), D ** -0.5)
    inp['b_ada'] = nrm((DEPTH, 6 * D), 0.01)
    inp['w_in'] = nrm((DEPTH, D, IN_COLS), D ** -0.5) * col_scale
    inp['b_fox_f'] = FOX_FORGET_BIAS + nrm((DEPTH, FOX_HEADS), 0.1)
    inp['hg_lb_logits'] = nrm((DEPTH + 1, HG_KWIDTH), 0.5)
    inp['hg_norm_g'] = 1.0 + nrm((DEPTH, HG_VDIM), 0.05)
    inp['w_branch_a'] = nrm((DEPTH, FOX_WIDTH, D), DEEPNORM_BETA * FOX_WIDTH ** -0.5)
    inp['w_branch_b'] = nrm((DEPTH, HG_VWIDTH, D), DEEPNORM_BETA * HG_VWIDTH ** -0.5)
    inp['w_out'] = nrm((DEPTH, D, D), DEEPNORM_BETA * D ** -0.5)
    inp['ln1_g'] = 1.0 + nrm((DEPTH, D), 0.05)
    inp['ln1_b'] = nrm((DEPTH, D), 0.01)
    inp['w_router'] = nrm((DEPTH, D, N_EXPERTS), D ** -0.5)
    inp['router_bias'] = nrm((DEPTH, N_EXPERTS), ROUTER_BIAS_SCALE)
    inp['w_exp_gate'] = nrm((DEPTH, N_EXPERTS, D, EXPERT_DIM), DEEPNORM_BETA * D ** -0.5)
    inp['w_exp_up'] = nrm((DEPTH, N_EXPERTS, D, EXPERT_DIM), DEEPNORM_BETA * D ** -0.5)
    inp['w_exp_down'] = nrm((DEPTH, N_EXPERTS, EXPERT_DIM, D), DEEPNORM_BETA * EXPERT_DIM ** -0.5)
    inp['w_sh_gate'] = nrm((DEPTH, D, SHARED_DIM), DEEPNORM_BETA * D ** -0.5)
    inp['w_sh_up'] = nrm((DEPTH, D, SHARED_DIM), DEEPNORM_BETA * D ** -0.5)
    inp['w_sh_down'] = nrm((DEPTH, SHARED_DIM, D), DEEPNORM_BETA * SHARED_DIM ** -0.5)
    inp['ln2_g'] = 1.0 + nrm((DEPTH, D), 0.05)
    inp['ln2_b'] = nrm((DEPTH, D), 0.01)
    return inp


def reference(x_prompt, x_sample, c_prompt, c_sample, cache_k, cache_v, cache_logf, state_hg, page_table,
              w_ada, b_ada, w_in, b_fox_f, hg_lb_logits, hg_norm_g, w_branch_a, w_branch_b, w_out,
              ln1_g, ln1_b, w_router, router_bias, w_exp_gate, w_exp_up, w_exp_down,
              w_sh_gate, w_sh_up, w_sh_down, ln2_g, ln2_b):
    lb_all = jnp.cumsum(jax.nn.softmax(hg_lb_logits.astype(jnp.float32), axis=0), axis=0)
    xp, xs = x_prompt, x_sample
    kp_l, vp_l, fp_l, sp_l, ks_l, vs_l, fs_l, ss_l = [], [], [], [], [], [], [], []
    for l in range(DEPTH):
        lp = {'w_ada': w_ada[l], 'b_ada': b_ada[l], 'w_in': w_in[l], 'b_fox_f': b_fox_f[l],
              'hg_norm_g': hg_norm_g[l], 'w_branch_a': w_branch_a[l], 'w_branch_b': w_branch_b[l],
              'w_out': w_out[l], 'ln1_g': ln1_g[l], 'ln1_b': ln1_b[l], 'w_router': w_router[l],
              'router_bias': router_bias[l], 'w_exp_gate': w_exp_gate[l], 'w_exp_up': w_exp_up[l],
              'w_exp_down': w_exp_down[l], 'w_sh_gate': w_sh_gate[l], 'w_sh_up': w_sh_up[l],
              'w_sh_down': w_sh_down[l], 'ln2_g': ln2_g[l], 'ln2_b': ln2_b[l]}
        lb = lb_all[l].reshape(HG_HEADS, HG_KDIM)
        s0 = jnp.zeros((xp.shape[0], HG_HEADS, HG_KDIM, HG_VDIM), jnp.float32)
        xp, kp, vp, fp, sp = _layer(xp, c_prompt, lp, lb, s0, _fox_prompt)
        ck, cv, cf = cache_k[l], cache_v[l], cache_logf[l]
        xs, ks, vs, fs, ss = _layer(xs, c_sample, lp, lb, state_hg[l],
                                    lambda q, k, v, f: _fox_sample(q, k, v, f, ck, cv, cf, page_table))
        kp_l.append(kp); vp_l.append(vp); fp_l.append(fp.astype(cache_logf.dtype)); sp_l.append(sp.astype(state_hg.dtype))
        ks_l.append(ks); vs_l.append(vs); fs_l.append(fs.astype(cache_logf.dtype)); ss_l.append(ss.astype(state_hg.dtype))
    k_prompt, v_prompt = jnp.stack(kp_l), jnp.stack(vp_l)
    logf_prompt, hg_state_prompt = jnp.stack(fp_l), jnp.stack(sp_l)
    k_sample, v_sample = jnp.stack(ks_l), jnp.stack(vs_l)
    logf_sample, hg_state_sample = jnp.stack(fs_l), jnp.stack(ss_l)
    return (xp, xs, k_prompt, v_prompt, logf_prompt, hg_state_prompt, k_sample, v_sample, logf_sample, hg_state_sample)
```

```python
import functools

import jax
import jax.numpy as jnp
import numpy as np
from jax import lax
from jax.experimental import pallas as pl
from jax.experimental.pallas import tpu as pltpu

F32 = jnp.float32
BF16 = jnp.bfloat16
I32 = jnp.int32

D_MODEL = 1024
FOX_HEADS = 8
FOX_HEAD_DIM = 64
FOX_WIDTH = FOX_HEADS * FOX_HEAD_DIM
HG_HEADS = 4
HG_DIM = 128
HG_WIDTH = HG_HEADS * HG_DIM
HG_CHUNK = 32
N_EXPERTS = 256
TOP_K = 8
EXPERT_DIM = 256
ROUTED_SCALE = 2.5
DEPTH = 1
DEEPNORM_ALPHA = (2.0 * DEPTH) ** 0.25
LN_EPS = 1e-5
RMS_EPS = 1e-6
NEG_INF = -1e30
PAGE = 128
LANES = 128
MOE_BLOCK = 128
N_HALVES = 2
VMEM_LIMIT = 56 * 1024 * 1024

OFF_Q, OFF_K, OFF_V, OFF_F = 0, 512, 1024, 1536
OFF_HQ, OFF_HF, OFF_HI, OFF_HOG = 1664, 2176, 2688, 3200
OFF_GA, OFF_GB, W_ALL_COLS = 3712, 4736, 5760


def _cparams(sem):
    return pltpu.CompilerParams(dimension_semantics=sem, vmem_limit_bytes=VMEM_LIMIT)


def _ln(x):
    mu = jnp.mean(x, axis=-1, keepdims=True)
    xc = x - mu
    var = jnp.mean(xc * xc, axis=-1, keepdims=True)
    return xc * lax.rsqrt(var + LN_EPS)


def _silu(x):
    return x * jax.nn.sigmoid(x)


def _split3(x):
    hi = x.astype(BF16)
    r1 = x - hi.astype(F32)
    mid = r1.astype(BF16)
    lo = (r1 - mid.astype(F32)).astype(BF16)
    return hi, mid, lo


def _dot3(x, m):
    hi, mid, lo = _split3(x)
    d = lambda a: jnp.dot(a, m, preferred_element_type=F32)
    return d(hi) + d(mid) + d(lo)


def _dot_nt(a, b):
    return lax.dot_general(a, b, (((1,), (1,)), ((), ())), preferred_element_type=F32)


def _dot_tn(a, b):
    return lax.dot_general(a, b, (((0,), (0,)), ((), ())), preferred_element_type=F32)


def _adaln_kernel(c_ref, w_ref, b_ref, o_ref):
    s = _silu(c_ref[...]).astype(BF16)
    o_ref[...] = jnp.dot(s, w_ref[...].astype(BF16), preferred_element_type=F32) + b_ref[...]


def _adaln(c, w_ada, b_ada):
    r = c.shape[0]
    tn = 1024
    return pl.pallas_call(
        _adaln_kernel,
        grid=(6 * D_MODEL // tn,),
        in_specs=[pl.BlockSpec((r, D_MODEL), lambda j: (0, 0)),
                  pl.BlockSpec((D_MODEL, tn), lambda j: (0, j)),
                  pl.BlockSpec((1, tn), lambda j: (0, j))],
        out_specs=pl.BlockSpec((r, tn), lambda j: (0, j)),
        out_shape=jax.ShapeDtypeStruct((r, 6 * D_MODEL), F32),
        compiler_params=_cparams(("parallel",)),
        name="adaln",
    )(c, w_ada, b_ada.reshape(1, -1))


def _inproj_kernel(x_ref, sh_ref, sc_ref, w_ref, bf_ref, lbl_ref,
                   q_ref, kb_ref, vb_ref, kf_ref, vf_ref, lf_ref,
                   hq_ref, hf_ref, hi_ref, hog_ref, sga_ref, sgb_ref):
    h = _ln(x_ref[...]) * (1.0 + sc_ref[0]) + sh_ref[0]
    hb = h.astype(BF16)

    def proj(a, b):
        return jnp.dot(hb, w_ref[:, a:b], preferred_element_type=F32)

    q_ref[...] = (proj(OFF_Q, OFF_K) * (FOX_HEAD_DIM ** -0.5)).astype(BF16)
    k = proj(OFF_K, OFF_V)
    kf_ref[...] = k
    kb_ref[...] = k.astype(BF16)
    v = proj(OFF_V, OFF_F)
    vf_ref[...] = v
    vb_ref[...] = v.astype(BF16)
    logf = jax.nn.log_sigmoid(proj(OFF_F, OFF_HQ) + bf_ref[...])
    lf_ref[...] = logf[:, :FOX_HEADS]

    hq_ref[...] = _silu(proj(OFF_HQ, OFF_HF))
    l0 = lbl_ref[0:1, :]
    l1 = lbl_ref[1:2, :]
    mx = jnp.maximum(l0, l1)
    e0 = jnp.exp(l0 - mx)
    lb = e0 / (e0 + jnp.exp(l1 - mx))
    hf_ref[...] = lb + (1.0 - lb) * jax.nn.sigmoid(proj(OFF_HF, OFF_HI))
    hi_ref[...] = proj(OFF_HI, OFF_HOG)
    hog_ref[...] = _silu(proj(OFF_HOG, OFF_GA))
    sga_ref[...] = jax.nn.sigmoid(proj(OFF_GA, OFF_GB))
    sgb_ref[...] = jax.nn.sigmoid(proj(OFF_GB, W_ALL_COLS))


def _inproj(x, sh, sc, tiles_per_group, w_all, bf_pad, lbl, tm):
    n = x.shape[0]
    r = sh.shape[1]
    row = lambda w: pl.BlockSpec((tm, w), lambda i: (i, 0))
    mod = pl.BlockSpec((1, r, D_MODEL), lambda i: (i // tiles_per_group, 0, 0))
    const = lambda s: pl.BlockSpec(s, lambda i: (0, 0))
    sds = lambda w, dt: jax.ShapeDtypeStruct((n, w), dt)
    return pl.pallas_call(
        _inproj_kernel,
        grid=(n // tm,),
        in_specs=[row(D_MODEL), mod, mod,
                  pl.BlockSpec((D_MODEL, W_ALL_COLS), lambda i: (0, 0), pipeline_mode=pl.Buffered(1)),
                  const((1, LANES)), const((2, HG_WIDTH))],
        out_specs=[row(FOX_WIDTH)] * 5 + [row(FOX_HEADS)] + [row(HG_WIDTH)] * 4 + [row(D_MODEL)] * 2,
        out_shape=[sds(FOX_WIDTH, BF16)] * 3 + [sds(FOX_WIDTH, F32)] * 2 + [sds(FOX_HEADS, F32)]
        + [sds(HG_WIDTH, F32)] * 4 + [sds(D_MODEL, F32)] * 2,
        compiler_params=_cparams(("parallel",)),
        name="inproj",
    )(x, sh, sc, w_all, bf_pad, lbl)


def _cum_matrices():
    r = np.arange(PAGE * FOX_HEADS)
    rp, rh = r // FOX_HEADS, r % FOX_HEADS
    ch, cp = r // PAGE, r % PAGE
    same = rh[:, None] == ch[None, :]
    m_cum = same & (rp[:, None] <= cp[None, :])
    return jnp.asarray(m_cum, BF16), jnp.asarray(same, BF16)


def _cum_pages_kernel(x_ref, m_ref, o_ref):
    o_ref[...] = _dot3(x_ref[...], m_ref[...])


def _cum_pages(x, m_cum, tb):
    n = x.shape[0]
    w = PAGE * FOX_HEADS
    return pl.pallas_call(
        _cum_pages_kernel,
        grid=(n // tb,),
        in_specs=[pl.BlockSpec((tb, w), lambda i: (i, 0)), pl.BlockSpec((w, w), lambda i: (0, 0))],
        out_specs=pl.BlockSpec((tb, w), lambda i: (i, 0)),
        out_shape=jax.ShapeDtypeStruct((n, w), F32),
        compiler_params=_cparams(("parallel",)),
        name="cum_pages",
    )(x, m_cum)


def _cum_seq_kernel(x_ref, m_ref, s_ref, g_ref, o_ref):
    x = x_ref[...]
    within = _dot3(x, m_ref[...])
    tot = _dot3(x, s_ref[...])
    hi, mid, lo = _split3(tot)
    g = g_ref[...]
    d = lambda a: jnp.dot(g, a, preferred_element_type=F32)
    o_ref[...] = within + d(hi) + d(mid) + d(lo)


def _cum_seq(x, m_cum, m_same, blocks_per_seq):
    nb = x.shape[0]
    b = np.arange(nb)
    g = (b[:, None] // blocks_per_seq == b[None, :] // blocks_per_seq) & (b[None, :] < b[:, None])
    return pl.pallas_call(
        _cum_seq_kernel,
        out_shape=jax.ShapeDtypeStruct(x.shape, F32),
        compiler_params=pltpu.CompilerParams(vmem_limit_bytes=VMEM_LIMIT),
        name="cum_seq",
    )(x, m_cum, m_same, jnp.asarray(g, BF16))


def _fox_prompt_kernel(q_ref, k_ref, v_ref, ck_ref, o_ref, m_sc, l_sc, acc_sc, *, tq):
    i = pl.program_id(2)
    q = q_ref[0]
    lo_lanes = lax.broadcasted_iota(I32, (tq, LANES), 1) < FOX_HEAD_DIM
    zero = jnp.zeros_like(q)
    qh = (jnp.where(lo_lanes, q, zero), jnp.where(lo_lanes, zero, q))
    q_off = pl.multiple_of(i * tq, tq)
    cq_rows = ck_ref[0, 0, :, pl.ds(q_off, tq)]
    cq = [jnp.transpose(jnp.broadcast_to(cq_rows[h:h + 1, :], (LANES, tq))) for h in range(2)]
    reps = tq // LANES
    row = lax.broadcasted_iota(I32, (tq, tq), 0)
    col = lax.broadcasted_iota(I32, (tq, tq), 1)

    m_sc[...] = jnp.full(m_sc.shape, NEG_INF, F32)
    l_sc[...] = jnp.zeros(l_sc.shape, F32)
    acc_sc[...] = jnp.zeros(acc_sc.shape, F32)

    def kv_step(j, diag):
        k_off = pl.multiple_of(j * tq, tq)
        kt = k_ref[0, pl.ds(k_off, tq), :]
        vt = v_ref[0, pl.ds(k_off, tq), :]
        ckt = ck_ref[0, 0, :, pl.ds(k_off, tq)]
        for h in range(2):
            s = _dot_nt(qh[h], kt)
            logits = s + jnp.concatenate([cq[h]] * reps, axis=1) - ckt[h:h + 1, :]
            if diag:
                logits = jnp.where(col <= row, logits, NEG_INF)
            m_prev = m_sc[h]
            m_new = jnp.maximum(m_prev, jnp.max(logits, axis=1, keepdims=True))
            p = jnp.exp(logits - jnp.concatenate([m_new] * reps, axis=1))
            alpha = jnp.exp(m_prev - m_new)
            l_sc[h] = alpha * l_sc[h] + jnp.sum(p, axis=1, keepdims=True)
            acc_sc[h] = alpha * acc_sc[h] + jnp.dot(p.astype(BF16), vt, preferred_element_type=F32)
            m_sc[h] = m_new

    def body(j, c):
        kv_step(j, False)
        return c

    lax.fori_loop(0, i, body, 0)
    kv_step(i, True)
    o = jnp.where(lo_lanes, acc_sc[0] / l_sc[0], acc_sc[1] / l_sc[1])
    o_ref[0] = o.astype(BF16)


def _fox_prompt(q, k, v, ck, tq):
    b, t, _ = q.shape
    pairs = FOX_HEADS // 2
    return pl.pallas_call(
        functools.partial(_fox_prompt_kernel, tq=tq),
        grid=(b, pairs, t // tq),
        in_specs=[pl.BlockSpec((1, tq, LANES), lambda bi, p, i: (bi, i, p)),
                  pl.BlockSpec((1, t, LANES), lambda bi, p, i: (bi, 0, p)),
                  pl.BlockSpec((1, t, LANES), lambda bi, p, i: (bi, 0, p)),
                  pl.BlockSpec((1, 1, 2, t), lambda bi, p, i: (bi, p, 0, 0))],
        out_specs=pl.BlockSpec((1, tq, LANES), lambda bi, p, i: (bi, i, p)),
        out_shape=jax.ShapeDtypeStruct((b, t, FOX_WIDTH), BF16),
        scratch_shapes=[pltpu.VMEM((2, tq, LANES), F32)] * 3,
        compiler_params=_cparams(("parallel", "parallel", "arbitrary")),
        name="fox_prompt",
    )(q, k, v, ck)


def _fox_sample_kernel(pt_ref, q_ref, kn_ref, vn_ref, cn_ref, *rest, pages):
    k_refs = rest[:pages]
    v_refs = rest[pages:2 * pages]
    c_refs = rest[2 * pages:3 * pages]
    o_ref, m_sc, l_sc, acc_sc, off_sc = rest[3 * pages:]
    g = pl.program_id(1)
    n_q = q_ref.shape[1]
    rows = n_q * FOX_HEADS

    @pl.when(g == 0)
    def _():
        m_sc[...] = jnp.full(m_sc.shape, NEG_INF, F32)
        l_sc[...] = jnp.zeros(l_sc.shape, F32)
        acc_sc[...] = jnp.zeros(acc_sc.shape, F32)
        off_sc[...] = jnp.zeros(off_sc.shape, F32)

    q4 = q_ref[0].astype(F32)
    head_of_lane = lax.broadcasted_iota(I32, (FOX_HEADS, FOX_WIDTH), 1) // FOX_HEAD_DIM
    bmask = head_of_lane == lax.broadcasted_iota(I32, (FOX_HEADS, FOX_WIDTH), 0)
    wq = [jnp.where(bmask, jnp.broadcast_to(q4[t:t + 1, :], (FOX_HEADS, FOX_WIDTH)), 0.0) for t in range(n_q)]
    wq = jnp.concatenate(wq + [jnp.zeros((PAGE - rows, FOX_WIDTH), F32)], axis=0).astype(BF16)

    def scores(kb, cum):
        st = _dot_nt(kb, wq)
        s = jnp.transpose(st)[:rows, :]
        return s - jnp.concatenate([cum] * n_q, axis=0)

    def update(s_all, v_list):
        m_prev = m_sc[...]
        m_new = jnp.maximum(m_prev, jnp.max(s_all, axis=1, keepdims=True))
        p = jnp.exp(s_all - jnp.concatenate([m_new] * (s_all.shape[1] // LANES), axis=1))
        alpha = jnp.exp(m_prev - m_new)
        l_sc[...] = alpha * l_sc[...] + jnp.sum(p, axis=1, keepdims=True)
        pv = jnp.zeros((rows, FOX_WIDTH), F32)
        for i, vb in enumerate(v_list):
            pv = pv + jnp.dot(p[:, i * PAGE:(i + 1) * PAGE].astype(BF16), vb, preferred_element_type=F32)
        acc_sc[...] = jnp.concatenate([alpha] * (FOX_WIDTH // LANES), axis=1) * acc_sc[...] + pv
        m_sc[...] = m_new

    off = off_sc[...]
    s_list, v_list = [], []
    for i in range(pages):
        within = c_refs[i][0]
        s_list.append(scores(k_refs[i][0].astype(BF16), off + within))
        v_list.append(v_refs[i][0].astype(BF16))
        off = off + jnp.broadcast_to(within[:, PAGE - 1:PAGE], (FOX_HEADS, PAGE))
    off_sc[...] = off
    update(jnp.concatenate(s_list, axis=1), v_list)

    @pl.when(g == pl.num_programs(1) - 1)
    def _():
        pad = jnp.zeros((PAGE - kn_ref.shape[1], FOX_WIDTH), F32)
        kb = jnp.concatenate([kn_ref[0], pad], axis=0).astype(BF16)
        vb = jnp.concatenate([vn_ref[0], pad], axis=0).astype(BF16)
        s = scores(kb, off + cn_ref[0])
        t_of_row = lax.broadcasted_iota(I32, (rows, PAGE), 0) // FOX_HEADS
        key = lax.broadcasted_iota(I32, (rows, PAGE), 1)
        update(jnp.where(key <= t_of_row, s, NEG_INF), [vb])
        o32 = acc_sc[...] / jnp.concatenate([l_sc[...]] * (FOX_WIDTH // LANES), axis=1)
        outs = []
        for t in range(n_q):
            blk = o32[t * FOX_HEADS:(t + 1) * FOX_HEADS, :]
            outs.append(jnp.sum(jnp.where(bmask, blk, 0.0), axis=0, keepdims=True))
        o_ref[0] = jnp.concatenate(outs, axis=0).astype(BF16)


def _fox_sample(page_table, q, k_new, v_new, cum_new, cache_k, cache_v, cum_pages, pages):
    bd, n_q, _ = q.shape
    n_pages = page_table.shape[1]
    seq = lambda s: pl.BlockSpec((1,) + s, lambda b, g, pt: (b, 0, 0))
    page = lambda s, i: pl.BlockSpec((1,) + s, lambda b, g, pt, i=i: (pt[b, g * pages + i], 0, 0))
    in_specs = ([seq((n_q, FOX_WIDTH)), seq(k_new.shape[1:]), seq(v_new.shape[1:]), seq((FOX_HEADS, PAGE))]
                + [page((PAGE, FOX_WIDTH), i) for i in range(pages)]
                + [page((PAGE, FOX_WIDTH), i) for i in range(pages)]
                + [page((FOX_HEADS, PAGE), i) for i in range(pages)])
    rows = n_q * FOX_HEADS
    return pl.pallas_call(
        functools.partial(_fox_sample_kernel, pages=pages),
        grid_spec=pltpu.PrefetchScalarGridSpec(
            num_scalar_prefetch=1,
            grid=(bd, n_pages // pages),
            in_specs=in_specs,
            out_specs=pl.BlockSpec((1, n_q, FOX_WIDTH), lambda b, g, pt: (b, 0, 0)),
            scratch_shapes=[pltpu.VMEM((rows, LANES), F32), pltpu.VMEM((rows, LANES), F32),
                            pltpu.VMEM((rows, FOX_WIDTH), F32), pltpu.VMEM((FOX_HEADS, PAGE), F32)]),
        out_shape=jax.ShapeDtypeStruct((bd, n_q, FOX_WIDTH), BF16),
        compiler_params=_cparams(("parallel", "arbitrary")),
        name="fox_sample",
    )(page_table, q, k_new, v_new, cum_new, *([cache_k] * pages), *([cache_v] * pages), *([cum_pages] * pages))


def _hgrn2_kernel(q_ref, f_ref, i_ref, s0_ref, o_ref, s_ref, st_sc, *, n_chunks):
    c = HG_CHUNK
    tci = pl.program_id(2)

    @pl.when(tci == 0)
    def _():
        st_sc[...] = jnp.transpose(s0_ref[0, 0])

    tril = lax.broadcasted_iota(I32, (c, c), 1) <= lax.broadcasted_iota(I32, (c, c), 0)
    ltri = tril.astype(BF16)

    def chunk(ci, carry):
        r0 = pl.multiple_of(ci * c, c)
        f = f_ref[0, pl.ds(r0, c), :]
        qc = q_ref[0, pl.ds(r0, c), :]
        ic = i_ref[0, pl.ds(r0, c), :].astype(BF16)
        kc = 1.0 - f
        hi, mid, lo = _split3(jnp.log(f))
        d = lambda a: jnp.dot(ltri, a, preferred_element_type=F32)
        g = d(hi) + d(mid) + d(lo)
        g_last = g[c - 1:c, :]
        q_dec = (qc * jnp.exp(g)).astype(BF16)
        k_inv = (kc * jnp.exp(-g)).astype(BF16)
        k_end = (kc * jnp.exp(g_last - g)).astype(BF16)
        a = jnp.where(tril, _dot_nt(q_dec, k_inv), 0.0)
        st = st_sc[...]
        o = jnp.dot(a.astype(BF16), ic, preferred_element_type=F32) + _dot_nt(q_dec, st.astype(BF16))
        o_ref[0, pl.ds(r0, c), :] = o
        st_sc[...] = st * jnp.exp(g_last) + _dot_tn(ic, k_end)
        return carry

    lax.fori_loop(0, n_chunks, chunk, 0)

    @pl.when(tci == pl.num_programs(2) - 1)
    def _():
        s_ref[0, 0] = jnp.transpose(st_sc[...])


def _hgrn2(q, f, i, s0, tc):
    b, t, _ = q.shape
    tok = pl.BlockSpec((1, tc, HG_DIM), lambda bi, h, ti: (bi, ti, h))
    st = pl.BlockSpec((1, 1, HG_DIM, HG_DIM), lambda bi, h, ti: (bi, h, 0, 0))
    return pl.pallas_call(
        functools.partial(_hgrn2_kernel, n_chunks=tc // HG_CHUNK),
        grid=(b, HG_HEADS, t // tc),
        in_specs=[tok, tok, tok, st],
        out_specs=[tok, st],
        out_shape=[jax.ShapeDtypeStruct((b, t, HG_WIDTH), F32),
                   jax.ShapeDtypeStruct((b, HG_HEADS, HG_DIM, HG_DIM), F32)],
        scratch_shapes=[pltpu.VMEM((HG_DIM, HG_DIM), F32)],
        compiler_params=_cparams(("parallel", "parallel", "arbitrary")),
        name="hgrn2",
    )(q, f, i, s0)


def _post_kernel(x_ref, oa_ref, ohg_ref, hog_ref, sga_ref, sgb_ref, g1_ref, sh2_ref, sc2_ref,
                 wa_ref, wb_ref, wo_ref, ng_ref, l1g_ref, l1b_ref, wr_ref, rb_ref,
                 wsg_ref, wsu_ref, wsd_ref,
                 x1_ref, h2p_ref, idx_ref, wt_ref, shd_ref):
    tm = x_ref.shape[0]
    ohg = ohg_ref[...]
    heads = []
    for h in range(HG_HEADS):
        oh = ohg[:, h * HG_DIM:(h + 1) * HG_DIM]
        heads.append(oh * lax.rsqrt(jnp.mean(oh * oh, axis=-1, keepdims=True) + RMS_EPS))
    ob = (jnp.concatenate(heads, axis=1) * ng_ref[...] * hog_ref[...]).astype(BF16)
    mixed = (sga_ref[...] * jnp.dot(oa_ref[...], wa_ref[...], preferred_element_type=F32)
             + sgb_ref[...] * jnp.dot(ob, wb_ref[...], preferred_element_type=F32))
    y = jnp.dot(mixed.astype(BF16), wo_ref[...], preferred_element_type=F32)
    x1 = _ln(DEEPNORM_ALPHA * x_ref[...] + g1_ref[0] * y) * l1g_ref[...] + l1b_ref[...]
    x1_ref[...] = x1
    h2 = _ln(x1) * (1.0 + sc2_ref[0]) + sh2_ref[0]
    h2b = h2.astype(BF16)

    half = D_MODEL // 2
    lo_bits = lax.shift_right_logical(pltpu.bitcast(h2b[:, :half].astype(F32), I32), 16)
    hi_bits = pltpu.bitcast(h2b[:, half:].astype(F32), I32) & jnp.int32(-65536)
    h2p_ref[...] = hi_bits | lo_bits

    scores = jax.nn.sigmoid(jnp.dot(h2b, wr_ref[...], preferred_element_type=F32))
    sel = scores + rb_ref[...]
    lane_e = lax.broadcasted_iota(I32, (tm, N_EXPERTS), 1).astype(F32)
    lane_o = lax.broadcasted_iota(I32, (tm, LANES), 1)
    idx_acc = jnp.zeros((tm, LANES), F32)
    w_acc = jnp.zeros((tm, LANES), F32)
    w_sum = jnp.zeros((tm, 1), F32)
    for k in range(TOP_K):
        mx = jnp.max(sel, axis=1, keepdims=True)
        ik = jnp.min(jnp.where(sel == mx, lane_e, float(N_EXPERTS)), axis=1, keepdims=True)
        hit = lane_e == ik
        wk = jnp.sum(jnp.where(hit, scores, 0.0), axis=1, keepdims=True)
        sel = jnp.where(hit, -jnp.inf, sel)
        idx_acc = jnp.where(lane_o == k, ik, idx_acc)
        w_acc = jnp.where(lane_o == k, wk, w_acc)
        w_sum = w_sum + wk
    idx_ref[...] = idx_acc.astype(I32)
    wt_ref[...] = ROUTED_SCALE * w_acc / w_sum

    sg = jnp.dot(h2b, wsg_ref[...], preferred_element_type=F32)
    su = jnp.dot(h2b, wsu_ref[...], preferred_element_type=F32)
    shd_ref[...] = jnp.dot((_silu(sg) * su).astype(BF16), wsd_ref[...], preferred_element_type=F32)


def _post(x, oa, ohg, hog, sga, sgb, g1, sh2, sc2, tiles_per_group, wts, tm):
    n = x.shape[0]
    r = g1.shape[1]
    row = lambda w: pl.BlockSpec((tm, w), lambda i: (i, 0))
    mod = pl.BlockSpec((1, r, D_MODEL), lambda i: (i // tiles_per_group, 0, 0))
    const = lambda a: pl.BlockSpec(a.shape, lambda i: (0, 0))
    sds = lambda w, dt: jax.ShapeDtypeStruct((n, w), dt)
    return pl.pallas_call(
        _post_kernel,
        grid=(n // tm,),
        in_specs=[row(D_MODEL), row(FOX_WIDTH), row(HG_WIDTH), row(HG_WIDTH), row(D_MODEL), row(D_MODEL),
                  mod, mod, mod] + [const(a) for a in wts],
        out_specs=[row(D_MODEL), row(D_MODEL // 2), row(LANES), row(LANES), row(D_MODEL)],
        out_shape=[sds(D_MODEL, F32), sds(D_MODEL // 2, I32), sds(LANES, I32), sds(LANES, F32),
                   sds(D_MODEL, F32)],
        compiler_params=_cparams(("parallel",)),
        name="post",
    )(x, oa, ohg, hog, sga, sgb, g1, sh2, sc2, *wts)


def _moe_up_kernel(be_ref, nv_ref, rows_ref, hp_ref, hs_ref, wg_ref, wu_ref, a_ref,
                   hv_sc, xt_sc, wgb_sc, wub_sc, sem, *, np4):
    b = pl.program_id(0)

    @pl.when(b == 0)
    def _():
        cp = pltpu.make_async_copy(hp_ref, hv_sc.at[pl.ds(0, np4)], sem.at[0])
        cs = pltpu.make_async_copy(hs_ref, hv_sc.at[pl.ds(np4, hs_ref.shape[0])], sem.at[1])
        cp.start()
        cs.start()
        cp.wait()
        cs.wait()

    @pl.when((b == 0) | (be_ref[b] != be_ref[jnp.maximum(b - 1, 0)]))
    def _():
        wgb_sc[...] = wg_ref[0].astype(BF16)
        wub_sc[...] = wu_ref[0].astype(BF16)

    @pl.when(b < nv_ref[0])
    def _():
        for m in range(MOE_BLOCK):
            t4 = pl.multiple_of(rows_ref[0, 0, m], 4)
            xt_sc[4 * m:4 * m + 4, :] = hv_sc[pl.ds(t4, 4), :]
        lo, hi = [], []
        for j in range(4):
            w = xt_sc[pl.ds(j, MOE_BLOCK, stride=4), :]
            lo.append(pltpu.bitcast(w << 16, F32))
            hi.append(pltpu.bitcast(w & jnp.int32(-65536), F32))
        x = jnp.concatenate(lo + hi, axis=1).astype(BF16)
        gate = jnp.dot(x, wgb_sc[...], preferred_element_type=F32)
        up = jnp.dot(x, wub_sc[...], preferred_element_type=F32)
        a_ref[...] = (_silu(gate) * up).astype(BF16)

    @pl.when(b >= nv_ref[0])
    def _():
        a_ref[...] = jnp.zeros(a_ref.shape, BF16)


def _moe_up(blk_e, n_valid, rows4, h2p_p, h2p_s, w_gate, w_up):
    nb = blk_e.shape[0]
    np4, ns4 = h2p_p.shape[0], h2p_s.shape[0]
    wspec = pl.BlockSpec((1, D_MODEL, EXPERT_DIM), lambda b, be, nv: (be[b], 0, 0))
    return pl.pallas_call(
        functools.partial(_moe_up_kernel, np4=np4),
        grid_spec=pltpu.PrefetchScalarGridSpec(
            num_scalar_prefetch=2,
            grid=(nb,),
            in_specs=[pl.BlockSpec((1, 1, MOE_BLOCK), lambda b, be, nv: (b, 0, 0), memory_space=pltpu.SMEM),
                      pl.BlockSpec(memory_space=pl.ANY), pl.BlockSpec(memory_space=pl.ANY), wspec, wspec],
            out_specs=pl.BlockSpec((MOE_BLOCK, EXPERT_DIM), lambda b, be, nv: (b, 0)),
            scratch_shapes=[pltpu.VMEM((np4 + ns4, LANES), I32), pltpu.VMEM((4 * MOE_BLOCK, LANES), I32),
                            pltpu.VMEM((D_MODEL, EXPERT_DIM), BF16), pltpu.VMEM((D_MODEL, EXPERT_DIM), BF16),
                            pltpu.SemaphoreType.DMA((2,))]),
        out_shape=jax.ShapeDtypeStruct((nb * MOE_BLOCK, EXPERT_DIM), BF16),
        compiler_params=_cparams(("arbitrary",)),
        name="moe_up",
    )(blk_e, n_valid, rows4, h2p_p, h2p_s, w_gate, w_up)


def _moe_down_kernel(be_ref, bh_ref, nv_ref, rows_ref, rw_ref, a_ref, wd_ref, out_ref,
                     acc_sc, y_sc, wdb_sc, sem, *, nh8, unroll):
    b = pl.program_id(0)
    nb = pl.num_programs(0)
    prev = jnp.maximum(b - 1, 0)
    nxt = jnp.minimum(b + 1, nb - 1)

    @pl.when((b == 0) | (bh_ref[b] != bh_ref[prev]))
    def _():
        acc_sc[...] = jnp.zeros(acc_sc.shape, F32)

    @pl.when((b == 0) | (be_ref[b] != be_ref[prev]))
    def _():
        wdb_sc[...] = wd_ref[0].astype(BF16)

    @pl.when(b < nv_ref[0])
    def _():
        y = jnp.dot(a_ref[...], wdb_sc[...], preferred_element_type=F32)
        tiles = D_MODEL // LANES
        for j in range(tiles):
            y_sc[pl.ds(j, MOE_BLOCK, stride=tiles), :] = y[:, j * LANES:(j + 1) * LANES]
        for m0 in range(0, MOE_BLOCK, unroll):
            new = []
            for m in range(m0, m0 + unroll):
                r8 = pl.multiple_of(rows_ref[0, 0, m], 8)
                new.append((r8, acc_sc[pl.ds(r8, 8), :] + rw_ref[0, 0, m] * y_sc[8 * m:8 * m + 8, :]))
            for r8, val in new:
                acc_sc[pl.ds(r8, 8), :] = val

    @pl.when((b == nb - 1) | (bh_ref[nxt] != bh_ref[b]))
    def _():
        cp = pltpu.make_async_copy(acc_sc.at[pl.ds(0, nh8)], out_ref.at[bh_ref[b]], sem.at[0])
        cp.start()
        cp.wait()


def _moe_down(blk_e, blk_h, n_valid, rows8, row_w, a, w_down, nh):
    nb = blk_e.shape[0]
    nh8 = nh * 8
    smem_blk = pl.BlockSpec((1, 1, MOE_BLOCK), lambda b, be, bh, nv: (b, 0, 0), memory_space=pltpu.SMEM)
    return pl.pallas_call(
        functools.partial(_moe_down_kernel, nh8=nh8, unroll=8),
        grid_spec=pltpu.PrefetchScalarGridSpec(
            num_scalar_prefetch=3,
            grid=(nb,),
            in_specs=[smem_blk, smem_blk,
                      pl.BlockSpec((MOE_BLOCK, EXPERT_DIM), lambda b, be, bh, nv: (b, 0)),
                      pl.BlockSpec((1, EXPERT_DIM, D_MODEL), lambda b, be, bh, nv: (be[b], 0, 0))],
            out_specs=pl.BlockSpec(memory_space=pl.ANY),
            scratch_shapes=[pltpu.VMEM((nh8 + 8, LANES), F32), pltpu.VMEM((8 * MOE_BLOCK, LANES), F32),
                            pltpu.VMEM((EXPERT_DIM, D_MODEL), BF16), pltpu.SemaphoreType.DMA((1,))]),
        out_shape=jax.ShapeDtypeStruct((N_HALVES, nh8, LANES), F32),
        compiler_params=_cparams(("arbitrary",)),
        name="moe_down",
    )(blk_e, blk_h, n_valid, rows8, row_w, a, w_down)


def _dispatch(idx, wts, n, nh):
    n_pairs = n * TOP_K
    n_groups = N_HALVES * N_EXPERTS
    nb = (n_pairs + n_groups * (MOE_BLOCK - 1)) // MOE_BLOCK
    n_rows = nb * MOE_BLOCK
    tok = jnp.arange(n_pairs, dtype=I32) // TOP_K
    half = tok // nh
    grp = half * N_EXPERTS + idx.reshape(n_pairs)
    order = jnp.argsort(grp)
    sorted_g = grp[order]
    counts = jnp.bincount(grp, length=n_groups)
    padded = (counts + MOE_BLOCK - 1) // MOE_BLOCK * MOE_BLOCK
    start = jnp.cumsum(counts) - counts
    pend = jnp.cumsum(padded)
    dest = (pend - padded)[sorted_g] + jnp.arange(n_pairs) - start[sorted_g]
    s_tok = tok[order]
    rows4 = jnp.zeros((n_rows,), I32).at[dest].set(s_tok * 4)
    rows8 = jnp.full((n_rows,), nh * 8, I32).at[dest].set((s_tok - half[order] * nh) * 8)
    row_w = jnp.zeros((n_rows,), F32).at[dest].set(wts.reshape(n_pairs)[order])
    n_valid = (pend[-1] // MOE_BLOCK).astype(I32)
    blk = jnp.minimum(jnp.arange(nb, dtype=I32), n_valid - 1)
    blk_g = jnp.minimum(jnp.searchsorted(pend, blk * MOE_BLOCK, side='right'), n_groups - 1).astype(I32)
    shape3 = (nb, 1, MOE_BLOCK)
    return (blk_g % N_EXPERTS, blk_g // N_EXPERTS, n_valid.reshape(1),
            rows4.reshape(shape3), rows8.reshape(shape3), row_w.reshape(shape3))


def _final_kernel(x1_ref, r_ref, s_ref, g2_ref, lg_ref, lb_ref, o_ref):
    y = r_ref[...] + s_ref[...]
    o_ref[...] = _ln(DEEPNORM_ALPHA * x1_ref[...] + g2_ref[0] * y) * lg_ref[...] + lb_ref[...]


def _final(x1, routed, row_off_tiles, shared, g2, tiles_per_group, ln_g, ln_b, tm):
    n = x1.shape[0]
    r = g2.shape[1]
    row = pl.BlockSpec((tm, D_MODEL), lambda i: (i, 0))
    return pl.pallas_call(
        _final_kernel,
        grid=(n // tm,),
        in_specs=[row, pl.BlockSpec((tm, D_MODEL), lambda i: (i + row_off_tiles, 0)), row,
                  pl.BlockSpec((1, r, D_MODEL), lambda i: (i // tiles_per_group, 0, 0)),
                  pl.BlockSpec((1, D_MODEL), lambda i: (0, 0)), pl.BlockSpec((1, D_MODEL), lambda i: (0, 0))],
        out_specs=row,
        out_shape=jax.ShapeDtypeStruct((n, D_MODEL), F32),
        compiler_params=_cparams(("parallel",)),
        name="final",
    )(x1, routed, shared, g2, ln_g, ln_b)


def kernel(x_prompt, x_sample, c_prompt, c_sample, cache_k, cache_v, cache_logf, state_hg, page_table, w_ada, b_ada, w_in, b_fox_f, hg_lb_logits, hg_norm_g, w_branch_a, w_branch_b, w_out, ln1_g, ln1_b, w_router, router_bias, w_exp_gate, w_exp_up, w_exp_down, w_sh_gate, w_sh_up, w_sh_down, ln2_g, ln2_b):
    assert w_ada.shape[0] == DEPTH and hg_lb_logits.shape[0] == DEPTH + 1
    b, t, d = x_prompt.shape
    bd, ts, _ = x_sample.shape
    n_p, n_s = b * t, bd * ts
    n = n_p + n_s
    n_phys = cache_k.shape[1]
    n_pages = page_table.shape[1]
    tm_p = 256
    tm_s = min(256, n_s)
    tq = min(256, t)
    assert n_p % tm_p == 0 and n_s % tm_s == 0 and t % tq == 0 and t % PAGE == 0 and n % N_HALVES == 0
    assert HG_CHUNK % ts == 0

    pad_f = jnp.zeros((d, LANES - FOX_HEADS), F32)
    w_all = jnp.concatenate([w_in[0][:, :OFF_F + FOX_HEADS], pad_f, w_in[0][:, OFF_F + FOX_HEADS:]], axis=1).astype(BF16)
    bf_pad = jnp.concatenate([b_fox_f[0], jnp.zeros((LANES - FOX_HEADS,), F32)]).reshape(1, LANES)
    post_w = (w_branch_a[0].astype(BF16), w_branch_b[0].astype(BF16), w_out[0].astype(BF16),
              jnp.tile(hg_norm_g[0], HG_HEADS).reshape(1, HG_WIDTH), ln1_g[0].reshape(1, d), ln1_b[0].reshape(1, d),
              w_router[0].astype(BF16), router_bias[0].reshape(1, N_EXPERTS),
              w_sh_gate[0].astype(BF16), w_sh_up[0].astype(BF16), w_sh_down[0].astype(BF16))
    m_cum, m_same = _cum_matrices()

    r_all = b + bd
    r_pad = -(-r_all // 8) * 8
    c_all = jnp.concatenate([c_prompt, c_sample, jnp.zeros((r_pad - r_all, d), F32)], axis=0)
    mod = _adaln(c_all, w_ada[0], b_ada[0])
    mod_p = [mod[:b, i * d:(i + 1) * d].reshape(b, 1, d) for i in range(6)]
    reps = tm_s // ts
    mod_s = [jnp.repeat(mod[b:b + bd, i * d:(i + 1) * d], ts, axis=0).reshape(n_s // tm_s, tm_s, d) for i in range(6)]
    tpg_p = t // tm_p

    (q_p, kb_p, vb_p, kf_p, vf_p, lf_p, hq_p, hf_p, hi_p, hog_p, sga_p, sgb_p) = _inproj(
        x_prompt.reshape(n_p, d), mod_p[0], mod_p[1], tpg_p, w_all, bf_pad, hg_lb_logits, tm_p)
    (q_s, kb_s, vb_s, kf_s, vf_s, lf_s, hq_s, hf_s, hi_s, hog_s, sga_s, sgb_s) = _inproj(
        x_sample.reshape(n_s, d), mod_s[0], mod_s[1], 1, w_all, bf_pad, hg_lb_logits, tm_s)

    bps = t // PAGE
    cum_p = _cum_seq(lf_p.reshape(n_p // PAGE, PAGE * FOX_HEADS), m_cum, m_same, bps)
    ck = cum_p.reshape(b, bps, FOX_HEADS, PAGE).transpose(0, 2, 1, 3).reshape(b, FOX_HEADS // 2, 2, t)
    oa_p = _fox_prompt(q_p.reshape(b, t, FOX_WIDTH), kb_p.reshape(b, t, FOX_WIDTH),
                       vb_p.reshape(b, t, FOX_WIDTH), ck, tq)

    tb = 512 if n_phys % 512 == 0 else n_phys
    cum_pages = _cum_pages(cache_logf[0].reshape(n_phys, PAGE * FOX_HEADS), m_cum, tb).reshape(n_phys, FOX_HEADS, PAGE)
    lf_new = jnp.pad(lf_s.reshape(bd, ts, FOX_HEADS), ((0, 0), (0, PAGE - ts), (0, 0)))
    cum_new = _cum_pages(lf_new.reshape(bd, PAGE * FOX_HEADS), m_cum, bd).reshape(bd, FOX_HEADS, PAGE)
    pad8 = lambda a: jnp.pad(a.reshape(bd, ts, FOX_WIDTH), ((0, 0), (0, 8 - ts), (0, 0)))
    pages = 8 if n_pages % 8 == 0 else n_pages
    oa_s = _fox_sample(page_table, q_s.reshape(bd, ts, FOX_WIDTH), pad8(kf_s), pad8(vf_s), cum_new,
                       cache_k[0].reshape(n_phys, PAGE, FOX_WIDTH), cache_v[0].reshape(n_phys, PAGE, FOX_WIDTH),
                       cum_pages, pages)

    tc = min(1024, t)
    ohg_p, st_p = _hgrn2(hq_p.reshape(b, t, HG_WIDTH), hf_p.reshape(b, t, HG_WIDTH), hi_p.reshape(b, t, HG_WIDTH),
                         jnp.zeros((b, HG_HEADS, HG_DIM, HG_DIM), F32), tc)
    padc = lambda a, v: jnp.pad(a.reshape(bd, ts, HG_WIDTH), ((0, 0), (0, HG_CHUNK - ts), (0, 0)), constant_values=v)
    ohg_s, st_s = _hgrn2(padc(hq_s, 0.0), padc(hf_s, 1.0), padc(hi_s, 0.0), state_hg[0], HG_CHUNK)
    ohg_s = ohg_s[:, :ts].reshape(n_s, HG_WIDTH)

    x1_p, h2p_p, idx_p, wt_p, shd_p = _post(
        x_prompt.reshape(n_p, d), oa_p.reshape(n_p, FOX_WIDTH), ohg_p.reshape(n_p, HG_WIDTH), hog_p, sga_p, sgb_p,
        mod_p[2], mod_p[3], mod_p[4], tpg_p, post_w, tm_p)
    x1_s, h2p_s, idx_s, wt_s, shd_s = _post(
        x_sample.reshape(n_s, d), oa_s.reshape(n_s, FOX_WIDTH), ohg_s, hog_s, sga_s, sgb_s,
        mod_s[2], mod_s[3], mod_s[4], 1, post_w, tm_s)

    nh = n // N_HALVES
    idx = jnp.concatenate([idx_p[:, :TOP_K], idx_s[:, :TOP_K]], axis=0)
    wts = jnp.concatenate([wt_p[:, :TOP_K], wt_s[:, :TOP_K]], axis=0)
    blk_e, blk_h, n_valid, rows4, rows8, row_w = _dispatch(idx, wts, n, nh)
    a = _moe_up(blk_e, n_valid, rows4, h2p_p.reshape(n_p * 4, LANES), h2p_s.reshape(n_s * 4, LANES),
                w_exp_gate[0], w_exp_up[0])
    routed = _moe_down(blk_e, blk_h, n_valid, rows8, row_w, a, w_exp_down[0], nh).reshape(n, d)

    lg, lb2 = ln2_g[0].reshape(1, d), ln2_b[0].reshape(1, d)
    y_p = _final(x1_p, routed, 0, shd_p, mod_p[5], tpg_p, lg, lb2, tm_p)
    y_s = _final(x1_s, routed, n_p // tm_s, shd_s, mod_s[5], 1, lg, lb2, tm_s)

    ldt, sdt = cache_logf.dtype, state_hg.dtype
    hd = (FOX_HEADS, FOX_HEAD_DIM)
    return (y_p.reshape(b, t, d), y_s.reshape(bd, ts, d),
            kf_p.reshape((1, b, t) + hd), vf_p.reshape((1, b, t) + hd),
            lf_p.reshape(1, b, t, FOX_HEADS).astype(ldt), st_p[None].astype(sdt),
            kf_s.reshape((1, bd, ts) + hd), vf_s.reshape((1, bd, ts) + hd),
            lf_s.reshape(1, bd, ts, FOX_HEADS).astype(ldt), st_s[None].astype(sdt))
```

```python
import functools

import jax
import jax.numpy as jnp
import numpy as np
from jax import lax
from jax.experimental import pallas as pl
from jax.experimental.pallas import tpu as pltpu

F32 = jnp.float32
BF16 = jnp.bfloat16
I32 = jnp.int32

D_MODEL = 1024
FOX_HEADS = 8
FOX_HEAD_DIM = 64
FOX_WIDTH = FOX_HEADS * FOX_HEAD_DIM
HG_HEADS = 4
HG_DIM = 128
HG_WIDTH = HG_HEADS * HG_DIM
HG_CHUNK = 32
N_EXPERTS = 256
TOP_K = 8
EXPERT_DIM = 256
ROUTED_SCALE = 2.5
DEPTH = 1
DEEPNORM_ALPHA = (2.0 * DEPTH) ** 0.25
LN_EPS = 1e-5
RMS_EPS = 1e-6
NEG_INF = -1e30
PAGE = 128
LANES = 128
MOE_BLOCK = 128
N_HALVES = 2
VMEM_LIMIT = 56 * 1024 * 1024

OFF_Q, OFF_K, OFF_V, OFF_F = 0, 512, 1024, 1536
OFF_HQ, OFF_HF, OFF_HI, OFF_HOG = 1664, 2176, 2688, 3200
OFF_GA, OFF_GB, W_ALL_COLS = 3712, 4736, 5760


def _cparams(sem):
    return pltpu.CompilerParams(dimension_semantics=sem, vmem_limit_bytes=VMEM_LIMIT)


def _ln(x):
    mu = jnp.mean(x, axis=-1, keepdims=True)
    xc = x - mu
    var = jnp.mean(xc * xc, axis=-1, keepdims=True)
    return xc * lax.rsqrt(var + LN_EPS)


def _silu(x):
    return x * jax.nn.sigmoid(x)


def _split3(x):
    hi = x.astype(BF16)
    r1 = x - hi.astype(F32)
    mid = r1.astype(BF16)
    lo = (r1 - mid.astype(F32)).astype(BF16)
    return hi, mid, lo


def _dot3(x, m):
    hi, mid, lo = _split3(x)
    d = lambda a: jnp.dot(a, m, preferred_element_type=F32)
    return d(hi) + d(mid) + d(lo)


def _dot_nt(a, b):
    return lax.dot_general(a, b, (((1,), (1,)), ((), ())), preferred_element_type=F32)


def _dot_tn(a, b):
    return lax.dot_general(a, b, (((0,), (0,)), ((), ())), preferred_element_type=F32)


def _adaln_kernel(c_ref, w_ref, b_ref, o_ref):
    s = _silu(c_ref[...]).astype(BF16)
    o_ref[...] = jnp.dot(s, w_ref[...].astype(BF16), preferred_element_type=F32) + b_ref[...]


def _adaln(c, w_ada, b_ada):
    r = c.shape[0]
    tn = 1024
    return pl.pallas_call(
        _adaln_kernel,
        grid=(6 * D_MODEL // tn,),
        in_specs=[pl.BlockSpec((r, D_MODEL), lambda j: (0, 0)),
                  pl.BlockSpec((D_MODEL, tn), lambda j: (0, j)),
                  pl.BlockSpec((1, tn), lambda j: (0, j))],
        out_specs=pl.BlockSpec((r, tn), lambda j: (0, j)),
        out_shape=jax.ShapeDtypeStruct((r, 6 * D_MODEL), F32),
        compiler_params=_cparams(("parallel",)),
        name="adaln",
    )(c, w_ada, b_ada.reshape(1, -1))


def _inproj_kernel(x_ref, sh_ref, sc_ref, w_ref, bf_ref, lbl_ref,
                   q_ref, kb_ref, vb_ref, kf_ref, vf_ref, lf_ref,
                   hq_ref, hf_ref, hi_ref, hog_ref, sga_ref, sgb_ref):
    h = _ln(x_ref[...]) * (1.0 + sc_ref[0]) + sh_ref[0]
    hb = h.astype(BF16)

    def proj(a, b):
        return jnp.dot(hb, w_ref[:, a:b], preferred_element_type=F32)

    q_ref[...] = (proj(OFF_Q, OFF_K) * (FOX_HEAD_DIM ** -0.5)).astype(BF16)
    k = proj(OFF_K, OFF_V)
    kf_ref[...] = k
    kb_ref[...] = k.astype(BF16)
    v = proj(OFF_V, OFF_F)
    vf_ref[...] = v
    vb_ref[...] = v.astype(BF16)
    logf = jax.nn.log_sigmoid(proj(OFF_F, OFF_HQ) + bf_ref[...])
    lf_ref[...] = logf[:, :FOX_HEADS]

    hq_ref[...] = _silu(proj(OFF_HQ, OFF_HF))
    l0 = lbl_ref[0:1, :]
    l1 = lbl_ref[1:2, :]
    mx = jnp.maximum(l0, l1)
    e0 = jnp.exp(l0 - mx)
    lb = e0 / (e0 + jnp.exp(l1 - mx))
    hf_ref[...] = lb + (1.0 - lb) * jax.nn.sigmoid(proj(OFF_HF, OFF_HI))
    hi_ref[...] = proj(OFF_HI, OFF_HOG)
    hog_ref[...] = _silu(proj(OFF_HOG, OFF_GA))
    sga_ref[...] = jax.nn.sigmoid(proj(OFF_GA, OFF_GB))
    sgb_ref[...] = jax.nn.sigmoid(proj(OFF_GB, W_ALL_COLS))


def _inproj(x, sh, sc, tiles_per_group, w_all, bf_pad, lbl, tm):
    n = x.shape[0]
    r = sh.shape[1]
    row = lambda w: pl.BlockSpec((tm, w), lambda i: (i, 0))
    mod = pl.BlockSpec((1, r, D_MODEL), lambda i: (i // tiles_per_group, 0, 0))
    const = lambda s: pl.BlockSpec(s, lambda i: (0, 0))
    sds = lambda w, dt: jax.ShapeDtypeStruct((n, w), dt)
    return pl.pallas_call(
        _inproj_kernel,
        grid=(n // tm,),
        in_specs=[row(D_MODEL), mod, mod,
                  pl.BlockSpec((D_MODEL, W_ALL_COLS), lambda i: (0, 0), pipeline_mode=pl.Buffered(1)),
                  const((1, LANES)), const((2, HG_WIDTH))],
        out_specs=[row(FOX_WIDTH)] * 5 + [row(FOX_HEADS)] + [row(HG_WIDTH)] * 4 + [row(D_MODEL)] * 2,
        out_shape=[sds(FOX_WIDTH, BF16)] * 3 + [sds(FOX_WIDTH, F32)] * 2 + [sds(FOX_HEADS, F32)]
        + [sds(HG_WIDTH, F32)] * 4 + [sds(D_MODEL, F32)] * 2,
        compiler_params=_cparams(("parallel",)),
        name="inproj",
    )(x, sh, sc, w_all, bf_pad, lbl)


def _upper_ones():
    r = np.arange(PAGE)
    return jnp.asarray(r[:, None] <= r[None, :], BF16)


def _cum_rows_kernel(x_ref, u_ref, o_ref):
    o_ref[...] = _dot3(x_ref[...], u_ref[...])


def _cum_rows(x, tb):
    n = x.shape[0]
    return pl.pallas_call(
        _cum_rows_kernel,
        grid=(n // tb,),
        in_specs=[pl.BlockSpec((tb, PAGE), lambda i: (i, 0)), pl.BlockSpec((PAGE, PAGE), lambda i: (0, 0))],
        out_specs=pl.BlockSpec((tb, PAGE), lambda i: (i, 0)),
        out_shape=jax.ShapeDtypeStruct((n, PAGE), F32),
        compiler_params=_cparams(("parallel",)),
        name="cum_rows",
    )(x, _upper_ones())


def _cum_seq_kernel(x_ref, u_ref, g_ref, o_ref):
    x = x_ref[...]
    within = _dot3(x, u_ref[...])
    tot = _dot3(x, jnp.ones((PAGE, PAGE), BF16))
    hi, mid, lo = _split3(tot)
    g = g_ref[...]
    d = lambda a: jnp.dot(g, a, preferred_element_type=F32)
    o_ref[...] = within + d(hi) + d(mid) + d(lo)


def _cum_seq(x, rows_per_seq):
    nr = x.shape[0]
    r = np.arange(nr)
    g = (r[:, None] // rows_per_seq == r[None, :] // rows_per_seq) & (r[None, :] < r[:, None])
    return pl.pallas_call(
        _cum_seq_kernel,
        out_shape=jax.ShapeDtypeStruct(x.shape, F32),
        compiler_params=pltpu.CompilerParams(vmem_limit_bytes=VMEM_LIMIT),
        name="cum_seq",
    )(x, _upper_ones(), jnp.asarray(g, BF16))


def _fox_prompt_kernel(q_ref, k_ref, v_ref, ck_ref, o_ref, m_sc, l_sc, acc_sc, *, tq):
    i = pl.program_id(2)
    q = q_ref[0]
    lo_lanes = lax.broadcasted_iota(I32, (tq, LANES), 1) < FOX_HEAD_DIM
    zero = jnp.zeros_like(q)
    qh = (jnp.where(lo_lanes, q, zero), jnp.where(lo_lanes, zero, q))
    q_off = pl.multiple_of(i * tq, tq)
    cq_rows = ck_ref[0, 0, :, pl.ds(q_off, tq)]
    cq = [jnp.transpose(jnp.broadcast_to(cq_rows[h:h + 1, :], (LANES, tq))) for h in range(2)]
    reps = tq // LANES
    row = lax.broadcasted_iota(I32, (tq, tq), 0)
    col = lax.broadcasted_iota(I32, (tq, tq), 1)

    m_sc[...] = jnp.full(m_sc.shape, NEG_INF, F32)
    l_sc[...] = jnp.zeros(l_sc.shape, F32)
    acc_sc[...] = jnp.zeros(acc_sc.shape, F32)

    def kv_step(j, diag):
        k_off = pl.multiple_of(j * tq, tq)
        kt = k_ref[0, pl.ds(k_off, tq), :]
        vt = v_ref[0, pl.ds(k_off, tq), :]
        ckt = ck_ref[0, 0, :, pl.ds(k_off, tq)]
        for h in range(2):
            s = _dot_nt(qh[h], kt)
            logits = s + jnp.concatenate([cq[h]] * reps, axis=1) - ckt[h:h + 1, :]
            if diag:
                logits = jnp.where(col <= row, logits, NEG_INF)
            m_prev = m_sc[h]
            m_new = jnp.maximum(m_prev, jnp.max(logits, axis=1, keepdims=True))
            p = jnp.exp(logits - jnp.concatenate([m_new] * reps, axis=1))
            alpha = jnp.exp(m_prev - m_new)
            l_sc[h] = alpha * l_sc[h] + jnp.sum(p, axis=1, keepdims=True)
            acc_sc[h] = alpha * acc_sc[h] + jnp.dot(p.astype(BF16), vt, preferred_element_type=F32)
            m_sc[h] = m_new

    def body(j, c):
        kv_step(j, False)
        return c

    lax.fori_loop(0, i, body, 0)
    kv_step(i, True)
    o = jnp.where(lo_lanes, acc_sc[0] / l_sc[0], acc_sc[1] / l_sc[1])
    o_ref[0] = o.astype(BF16)


def _fox_prompt(q, k, v, ck, tq):
    b, t, _ = q.shape
    pairs = FOX_HEADS // 2
    return pl.pallas_call(
        functools.partial(_fox_prompt_kernel, tq=tq),
        grid=(b, pairs, t // tq),
        in_specs=[pl.BlockSpec((1, tq, LANES), lambda bi, p, i: (bi, i, p)),
                  pl.BlockSpec((1, t, LANES), lambda bi, p, i: (bi, 0, p)),
                  pl.BlockSpec((1, t, LANES), lambda bi, p, i: (bi, 0, p)),
                  pl.BlockSpec((1, 1, 2, t), lambda bi, p, i: (bi, p, 0, 0))],
        out_specs=pl.BlockSpec((1, tq, LANES), lambda bi, p, i: (bi, i, p)),
        out_shape=jax.ShapeDtypeStruct((b, t, FOX_WIDTH), BF16),
        scratch_shapes=[pltpu.VMEM((2, tq, LANES), F32)] * 3,
        compiler_params=_cparams(("parallel", "parallel", "arbitrary")),
        name="fox_prompt",
    )(q, k, v, ck)


def _fox_sample_kernel(pt_ref, q_ref, kn_ref, vn_ref, cn_ref, *rest, pages):
    k_refs = rest[:pages]
    v_refs = rest[pages:2 * pages]
    c_refs = rest[2 * pages:3 * pages]
    o_ref, m_sc, l_sc, acc_sc, off_sc = rest[3 * pages:]
    g = pl.program_id(1)
    n_q = q_ref.shape[1]
    rows = n_q * FOX_HEADS

    @pl.when(g == 0)
    def _():
        m_sc[...] = jnp.full(m_sc.shape, NEG_INF, F32)
        l_sc[...] = jnp.zeros(l_sc.shape, F32)
        acc_sc[...] = jnp.zeros(acc_sc.shape, F32)
        off_sc[...] = jnp.zeros(off_sc.shape, F32)

    q4 = q_ref[0].astype(F32)
    head_of_lane = lax.broadcasted_iota(I32, (FOX_HEADS, FOX_WIDTH), 1) // FOX_HEAD_DIM
    bmask = head_of_lane == lax.broadcasted_iota(I32, (FOX_HEADS, FOX_WIDTH), 0)
    wq = jnp.concatenate(
        [jnp.where(bmask, jnp.broadcast_to(q4[t:t + 1, :], (FOX_HEADS, FOX_WIDTH)), 0.0) for t in range(n_q)], axis=0)

    def update(s_all, pv_fn):
        m_prev = m_sc[...]
        m_new = jnp.maximum(m_prev, jnp.max(s_all, axis=1, keepdims=True))
        p = jnp.exp(s_all - jnp.concatenate([m_new] * (s_all.shape[1] // LANES), axis=1))
        alpha = jnp.exp(m_prev - m_new)
        l_sc[...] = alpha * l_sc[...] + jnp.sum(p, axis=1, keepdims=True)
        acc_sc[...] = jnp.concatenate([alpha] * (FOX_WIDTH // LANES), axis=1) * acc_sc[...] + pv_fn(p)
        m_sc[...] = m_new

    off = off_sc[...]
    s_list = []
    for i in range(pages):
        within = c_refs[i][0]
        s = jnp.dot(wq, k_refs[i][0], preferred_element_type=F32)
        s_list.append(s - jnp.concatenate([off + within] * n_q, axis=0))
        off = off + jnp.broadcast_to(within[:, PAGE - 1:PAGE], (FOX_HEADS, PAGE))
    off_sc[...] = off

    def pv_pages(p):
        pv = _dot_nt(p[:, :PAGE], v_refs[0][0])
        for i in range(1, pages):
            pv = pv + _dot_nt(p[:, i * PAGE:(i + 1) * PAGE], v_refs[i][0])
        return pv

    update(jnp.concatenate(s_list, axis=1), pv_pages)

    @pl.when(g == pl.num_programs(1) - 1)
    def _():
        pad = jnp.zeros((PAGE - kn_ref.shape[1], FOX_WIDTH), F32)
        kn = jnp.concatenate([kn_ref[0], pad], axis=0)
        vn = jnp.concatenate([vn_ref[0], pad], axis=0)
        s = _dot_nt(wq, kn) - jnp.concatenate([off + cn_ref[0]] * n_q, axis=0)
        t_of_row = lax.broadcasted_iota(I32, (rows, PAGE), 0) // FOX_HEADS
        key = lax.broadcasted_iota(I32, (rows, PAGE), 1)
        update(jnp.where(key <= t_of_row, s, NEG_INF),
               lambda p: jnp.dot(p, vn, preferred_element_type=F32))
        o32 = acc_sc[...] / jnp.concatenate([l_sc[...]] * (FOX_WIDTH // LANES), axis=1)
        outs = []
        for t in range(n_q):
            blk = o32[t * FOX_HEADS:(t + 1) * FOX_HEADS, :]
            outs.append(jnp.sum(jnp.where(bmask, blk, 0.0), axis=0, keepdims=True))
        o_ref[0] = jnp.concatenate(outs, axis=0).astype(BF16)


def _fox_sample(page_table, q, k_new, v_new, cum_new, cache_k, cache_v, cum_pages, pages):
    bd, n_q, _ = q.shape
    n_pages = page_table.shape[1]
    seq = lambda s: pl.BlockSpec((1,) + s, lambda b, g, pt: (b, 0, 0))
    page = lambda s, i: pl.BlockSpec((1,) + s, lambda b, g, pt, i=i: (pt[b, g * pages + i], 0, 0))
    in_specs = ([seq((n_q, FOX_WIDTH)), seq(k_new.shape[1:]), seq(v_new.shape[1:]), seq((FOX_HEADS, PAGE))]
                + [page((FOX_WIDTH, PAGE), i) for i in range(pages)]
                + [page((FOX_WIDTH, PAGE), i) for i in range(pages)]
                + [page((FOX_HEADS, PAGE), i) for i in range(pages)])
    rows = n_q * FOX_HEADS
    return pl.pallas_call(
        functools.partial(_fox_sample_kernel, pages=pages),
        grid_spec=pltpu.PrefetchScalarGridSpec(
            num_scalar_prefetch=1,
            grid=(bd, n_pages // pages),
            in_specs=in_specs,
            out_specs=pl.BlockSpec((1, n_q, FOX_WIDTH), lambda b, g, pt: (b, 0, 0)),
            scratch_shapes=[pltpu.VMEM((rows, LANES), F32), pltpu.VMEM((rows, LANES), F32),
                            pltpu.VMEM((rows, FOX_WIDTH), F32), pltpu.VMEM((FOX_HEADS, PAGE), F32)]),
        out_shape=jax.ShapeDtypeStruct((bd, n_q, FOX_WIDTH), BF16),
        compiler_params=_cparams(("parallel", "arbitrary")),
        name="fox_sample",
    )(page_table, q, k_new, v_new, cum_new, *([cache_k] * pages), *([cache_v] * pages), *([cum_pages] * pages))


def _hgrn2_kernel(q_ref, f_ref, i_ref, s0_ref, o_ref, s_ref, st_sc, *, n_chunks):
    c = HG_CHUNK
    tci = pl.program_id(2)

    @pl.when(tci == 0)
    def _():
        st_sc[...] = jnp.transpose(s0_ref[0, 0])

    tril = lax.broadcasted_iota(I32, (c, c), 1) <= lax.broadcasted_iota(I32, (c, c), 0)
    ltri = tril.astype(BF16)

    def chunk(ci, carry):
        r0 = pl.multiple_of(ci * c, c)
        f = f_ref[0, pl.ds(r0, c), :]
        qc = q_ref[0, pl.ds(r0, c), :]
        ic = i_ref[0, pl.ds(r0, c), :].astype(BF16)
        kc = 1.0 - f
        hi, mid, lo = _split3(jnp.log(f))
        d = lambda a: jnp.dot(ltri, a, preferred_element_type=F32)
        g = d(hi) + d(mid) + d(lo)
        g_last = g[c - 1:c, :]
        q_dec = (qc * jnp.exp(g)).astype(BF16)
        k_inv = (kc * jnp.exp(-g)).astype(BF16)
        k_end = (kc * jnp.exp(g_last - g)).astype(BF16)
        a = jnp.where(tril, _dot_nt(q_dec, k_inv), 0.0)
        st = st_sc[...]
        o = jnp.dot(a.astype(BF16), ic, preferred_element_type=F32) + _dot_nt(q_dec, st.astype(BF16))
        o_ref[0, pl.ds(r0, c), :] = o
        st_sc[...] = st * jnp.exp(g_last) + _dot_tn(ic, k_end)
        return carry

    lax.fori_loop(0, n_chunks, chunk, 0)

    @pl.when(tci == pl.num_programs(2) - 1)
    def _():
        s_ref[0, 0] = jnp.transpose(st_sc[...])


def _hgrn2(q, f, i, s0, tc):
    b, t, _ = q.shape
    tok = pl.BlockSpec((1, tc, HG_DIM), lambda bi, h, ti: (bi, ti, h))
    st = pl.BlockSpec((1, 1, HG_DIM, HG_DIM), lambda bi, h, ti: (bi, h, 0, 0))
    return pl.pallas_call(
        functools.partial(_hgrn2_kernel, n_chunks=tc // HG_CHUNK),
        grid=(b, HG_HEADS, t // tc),
        in_specs=[tok, tok, tok, st],
        out_specs=[tok, st],
        out_shape=[jax.ShapeDtypeStruct((b, t, HG_WIDTH), F32),
                   jax.ShapeDtypeStruct((b, HG_HEADS, HG_DIM, HG_DIM), F32)],
        scratch_shapes=[pltpu.VMEM((HG_DIM, HG_DIM), F32)],
        compiler_params=_cparams(("parallel", "parallel", "arbitrary")),
        name="hgrn2",
    )(q, f, i, s0)


def _post_kernel(x_ref, oa_ref, ohg_ref, hog_ref, sga_ref, sgb_ref, g1_ref, sh2_ref, sc2_ref,
                 wa_ref, wb_ref, wo_ref, ng_ref, l1g_ref, l1b_ref, wr_ref, rb_ref,
                 wsg_ref, wsu_ref, wsd_ref,
                 x1_ref, h2p_ref, idx_ref, wt_ref, shd_ref):
    tm = x_ref.shape[0]
    ohg = ohg_ref[...]
    heads = []
    for h in range(HG_HEADS):
        oh = ohg[:, h * HG_DIM:(h + 1) * HG_DIM]
        heads.append(oh * lax.rsqrt(jnp.mean(oh * oh, axis=-1, keepdims=True) + RMS_EPS))
    ob = (jnp.concatenate(heads, axis=1) * ng_ref[...] * hog_ref[...]).astype(BF16)
    mixed = (sga_ref[...] * jnp.dot(oa_ref[...], wa_ref[...], preferred_element_type=F32)
             + sgb_ref[...] * jnp.dot(ob, wb_ref[...], preferred_element_type=F32))
    y = jnp.dot(mixed.astype(BF16), wo_ref[...], preferred_element_type=F32)
    x1 = _ln(DEEPNORM_ALPHA * x_ref[...] + g1_ref[0] * y) * l1g_ref[...] + l1b_ref[...]
    x1_ref[...] = x1
    h2 = _ln(x1) * (1.0 + sc2_ref[0]) + sh2_ref[0]
    h2b = h2.astype(BF16)

    half = D_MODEL // 2
    lo_bits = lax.shift_right_logical(pltpu.bitcast(h2b[:, :half].astype(F32), I32), 16)
    hi_bits = pltpu.bitcast(h2b[:, half:].astype(F32), I32) & jnp.int32(-65536)
    h2p_ref[...] = hi_bits | lo_bits

    scores = jax.nn.sigmoid(jnp.dot(h2b, wr_ref[...], preferred_element_type=F32))
    sel = scores + rb_ref[...]
    lane_e = lax.broadcasted_iota(I32, (tm, N_EXPERTS), 1).astype(F32)
    lane_o = lax.broadcasted_iota(I32, (tm, LANES), 1)
    idx_acc = jnp.zeros((tm, LANES), F32)
    w_acc = jnp.zeros((tm, LANES), F32)
    w_sum = jnp.zeros((tm, 1), F32)
    for k in range(TOP_K):
        mx = jnp.max(sel, axis=1, keepdims=True)
        ik = jnp.min(jnp.where(sel == mx, lane_e, float(N_EXPERTS)), axis=1, keepdims=True)
        hit = lane_e == ik
        wk = jnp.sum(jnp.where(hit, scores, 0.0), axis=1, keepdims=True)
        sel = jnp.where(hit, -jnp.inf, sel)
        idx_acc = jnp.where(lane_o == k, ik, idx_acc)
        w_acc = jnp.where(lane_o == k, wk, w_acc)
        w_sum = w_sum + wk
    idx_ref[...] = idx_acc.astype(I32)
    wt_ref[...] = ROUTED_SCALE * w_acc / w_sum

    sg = jnp.dot(h2b, wsg_ref[...], preferred_element_type=F32)
    su = jnp.dot(h2b, wsu_ref[...], preferred_element_type=F32)
    shd_ref[...] = jnp.dot((_silu(sg) * su).astype(BF16), wsd_ref[...], preferred_element_type=F32)


def _post(x, oa, ohg, hog, sga, sgb, g1, sh2, sc2, tiles_per_group, wts, tm):
    n = x.shape[0]
    r = g1.shape[1]
    row = lambda w: pl.BlockSpec((tm, w), lambda i: (i, 0))
    mod = pl.BlockSpec((1, r, D_MODEL), lambda i: (i // tiles_per_group, 0, 0))
    const = lambda a: pl.BlockSpec(a.shape, lambda i: (0, 0))
    sds = lambda w, dt: jax.ShapeDtypeStruct((n, w), dt)
    return pl.pallas_call(
        _post_kernel,
        grid=(n // tm,),
        in_specs=[row(D_MODEL), row(FOX_WIDTH), row(HG_WIDTH), row(HG_WIDTH), row(D_MODEL), row(D_MODEL),
                  mod, mod, mod] + [const(a) for a in wts],
        out_specs=[row(D_MODEL), row(D_MODEL // 2), row(LANES), row(LANES), row(D_MODEL)],
        out_shape=[sds(D_MODEL, F32), sds(D_MODEL // 2, I32), sds(LANES, I32), sds(LANES, F32),
                   sds(D_MODEL, F32)],
        compiler_params=_cparams(("parallel",)),
        name="post",
    )(x, oa, ohg, hog, sga, sgb, g1, sh2, sc2, *wts)


def _moe_up_kernel(be_ref, nv_ref, rows_ref, hp_ref, hs_ref, wg_ref, wu_ref, a_ref,
                   hv_sc, xt_sc, sem, *, np4):
    b = pl.program_id(0)

    @pl.when(b == 0)
    def _():
        cp = pltpu.make_async_copy(hp_ref, hv_sc.at[pl.ds(0, np4)], sem.at[0])
        cs = pltpu.make_async_copy(hs_ref, hv_sc.at[pl.ds(np4, hs_ref.shape[0])], sem.at[1])
        cp.start()
        cs.start()
        cp.wait()
        cs.wait()

    @pl.when(b < nv_ref[0])
    def _():
        for m in range(MOE_BLOCK):
            t4 = pl.multiple_of(rows_ref[0, 0, m], 4)
            xt_sc[4 * m:4 * m + 4, :] = hv_sc[pl.ds(t4, 4), :]
        lo, hi = [], []
        for j in range(4):
            w = xt_sc[pl.ds(j, MOE_BLOCK, stride=4), :]
            lo.append(pltpu.bitcast(w << 16, F32))
            hi.append(pltpu.bitcast(w & jnp.int32(-65536), F32))
        x = jnp.concatenate(lo + hi, axis=1)
        gate = jnp.dot(x, wg_ref[0], preferred_element_type=F32)
        up = jnp.dot(x, wu_ref[0], preferred_element_type=F32)
        a_ref[...] = (_silu(gate) * up).astype(BF16)

    @pl.when(b >= nv_ref[0])
    def _():
        a_ref[...] = jnp.zeros(a_ref.shape, BF16)


def _moe_up(blk_e, n_valid, rows4, h2p_p, h2p_s, w_gate, w_up):
    nb = blk_e.shape[0]
    np4, ns4 = h2p_p.shape[0], h2p_s.shape[0]
    wspec = pl.BlockSpec((1, D_MODEL, EXPERT_DIM), lambda b, be, nv: (be[b], 0, 0))
    return pl.pallas_call(
        functools.partial(_moe_up_kernel, np4=np4),
        grid_spec=pltpu.PrefetchScalarGridSpec(
            num_scalar_prefetch=2,
            grid=(nb,),
            in_specs=[pl.BlockSpec((1, 1, MOE_BLOCK), lambda b, be, nv: (b, 0, 0), memory_space=pltpu.SMEM),
                      pl.BlockSpec(memory_space=pl.ANY), pl.BlockSpec(memory_space=pl.ANY), wspec, wspec],
            out_specs=pl.BlockSpec((MOE_BLOCK, EXPERT_DIM), lambda b, be, nv: (b, 0)),
            scratch_shapes=[pltpu.VMEM((np4 + ns4, LANES), I32), pltpu.VMEM((4 * MOE_BLOCK, LANES), I32),
                            pltpu.SemaphoreType.DMA((2,))]),
        out_shape=jax.ShapeDtypeStruct((nb * MOE_BLOCK, EXPERT_DIM), BF16),
        compiler_params=_cparams(("arbitrary",)),
        name="moe_up",
    )(blk_e, n_valid, rows4, h2p_p, h2p_s, w_gate, w_up)


def _moe_down_kernel(be_ref, bh_ref, nv_ref, rows_ref, rw_ref, a_ref, wd_ref, out_ref,
                     acc_sc, y_sc, sem, *, nh8, unroll):
    b = pl.program_id(0)
    nb = pl.num_programs(0)
    prev = jnp.maximum(b - 1, 0)
    nxt = jnp.minimum(b + 1, nb - 1)

    @pl.when((b == 0) | (bh_ref[b] != bh_ref[prev]))
    def _():
        acc_sc[...] = jnp.zeros(acc_sc.shape, F32)

    @pl.when(b < nv_ref[0])
    def _():
        y = jnp.dot(a_ref[...].astype(F32), wd_ref[0], preferred_element_type=F32)
        tiles = D_MODEL // LANES
        for j in range(tiles):
            y_sc[pl.ds(j, MOE_BLOCK, stride=tiles), :] = y[:, j * LANES:(j + 1) * LANES]
        for m0 in range(0, MOE_BLOCK, unroll):
            new = []
            for m in range(m0, m0 + unroll):
                r8 = pl.multiple_of(rows_ref[0, 0, m], 8)
                new.append((r8, acc_sc[pl.ds(r8, 8), :] + rw_ref[0, 0, m] * y_sc[8 * m:8 * m + 8, :]))
            for r8, val in new:
                acc_sc[pl.ds(r8, 8), :] = val

    @pl.when((b == nb - 1) | (bh_ref[nxt] != bh_ref[b]))
    def _():
        cp = pltpu.make_async_copy(acc_sc.at[pl.ds(0, nh8)], out_ref.at[bh_ref[b]], sem.at[0])
        cp.start()
        cp.wait()


def _moe_down(blk_e, blk_h, n_valid, rows8, row_w, a, w_down, nh):
    nb = blk_e.shape[0]
    nh8 = nh * 8
    smem_blk = pl.BlockSpec((1, 1, MOE_BLOCK), lambda b, be, bh, nv: (b, 0, 0), memory_space=pltpu.SMEM)
    return pl.pallas_call(
        functools.partial(_moe_down_kernel, nh8=nh8, unroll=8),
        grid_spec=pltpu.PrefetchScalarGridSpec(
            num_scalar_prefetch=3,
            grid=(nb,),
            in_specs=[smem_blk, smem_blk,
                      pl.BlockSpec((MOE_BLOCK, EXPERT_DIM), lambda b, be, bh, nv: (b, 0)),
                      pl.BlockSpec((1, EXPERT_DIM, D_MODEL), lambda b, be, bh, nv: (be[b], 0, 0))],
            out_specs=pl.BlockSpec(memory_space=pl.ANY),
            scratch_shapes=[pltpu.VMEM((nh8 + 8, LANES), F32), pltpu.VMEM((8 * MOE_BLOCK, LANES), F32),
                            pltpu.SemaphoreType.DMA((1,))]),
        out_shape=jax.ShapeDtypeStruct((N_HALVES, nh8, LANES), F32),
        compiler_params=_cparams(("arbitrary",)),
        name="moe_down",
    )(blk_e, blk_h, n_valid, rows8, row_w, a, w_down)


def _dispatch(idx, wts, n, nh):
    n_pairs = n * TOP_K
    n_groups = N_HALVES * N_EXPERTS
    fill = MOE_BLOCK - 1
    n_rows = -(-(n_pairs + n_groups * fill) // MOE_BLOCK) * MOE_BLOCK
    nb = n_rows // MOE_BLOCK
    tok = jnp.arange(n_pairs, dtype=I32) // TOP_K
    grp = (tok // nh) * N_EXPERTS + idx.reshape(n_pairs)
    gid = jnp.arange(n_groups, dtype=I32)
    counts = jnp.sum((grp[:, None] == gid[None, :]).astype(I32), axis=0)
    need = (-counts) % MOE_BLOCK
    fill_key = jnp.where(jnp.arange(fill, dtype=I32)[None, :] < need[:, None], gid[:, None], n_groups)
    n_tail = n_rows - n_pairs - n_groups * fill
    keys = jnp.concatenate([grp, fill_key.reshape(-1), jnp.full((n_tail,), n_groups, I32)])
    n_fill = n_rows - n_pairs
    toks = jnp.concatenate([tok, jnp.full((n_fill,), -1, I32)])
    ws = jnp.concatenate([wts.reshape(n_pairs), jnp.zeros((n_fill,), F32)])
    s_key, s_tok, s_w = lax.sort((keys, toks, ws), num_keys=1)
    real = s_tok >= 0
    rows4 = jnp.where(real, s_tok * 4, 0)
    rows8 = jnp.where(real, (s_tok - (s_key // N_EXPERTS) * nh) * 8, nh * 8)
    blk_key = s_key[::MOE_BLOCK]
    valid = blk_key < n_groups
    n_valid = jnp.sum(valid.astype(I32))
    blk_e = jnp.where(valid, blk_key % N_EXPERTS, 0)
    blk_h = jnp.where(valid, blk_key // N_EXPERTS, N_HALVES - 1)
    shape3 = (nb, 1, MOE_BLOCK)
    return (blk_e, blk_h, n_valid.reshape(1), rows4.reshape(shape3), rows8.reshape(shape3), s_w.reshape(shape3))


def _final_kernel(x1_ref, r_ref, s_ref, g2_ref, lg_ref, lb_ref, o_ref):
    y = r_ref[...] + s_ref[...]
    o_ref[...] = _ln(DEEPNORM_ALPHA * x1_ref[...] + g2_ref[0] * y) * lg_ref[...] + lb_ref[...]


def _final(x1, routed, row_off_tiles, shared, g2, tiles_per_group, ln_g, ln_b, tm):
    n = x1.shape[0]
    r = g2.shape[1]
    row = pl.BlockSpec((tm, D_MODEL), lambda i: (i, 0))
    return pl.pallas_call(
        _final_kernel,
        grid=(n // tm,),
        in_specs=[row, pl.BlockSpec((tm, D_MODEL), lambda i: (i + row_off_tiles, 0)), row,
                  pl.BlockSpec((1, r, D_MODEL), lambda i: (i // tiles_per_group, 0, 0)),
                  pl.BlockSpec((1, D_MODEL), lambda i: (0, 0)), pl.BlockSpec((1, D_MODEL), lambda i: (0, 0))],
        out_specs=row,
        out_shape=jax.ShapeDtypeStruct((n, D_MODEL), F32),
        compiler_params=_cparams(("parallel",)),
        name="final",
    )(x1, routed, shared, g2, ln_g, ln_b)


def kernel(x_prompt, x_sample, c_prompt, c_sample, cache_k, cache_v, cache_logf, state_hg, page_table, w_ada, b_ada, w_in, b_fox_f, hg_lb_logits, hg_norm_g, w_branch_a, w_branch_b, w_out, ln1_g, ln1_b, w_router, router_bias, w_exp_gate, w_exp_up, w_exp_down, w_sh_gate, w_sh_up, w_sh_down, ln2_g, ln2_b):
    assert w_ada.shape[0] == DEPTH and hg_lb_logits.shape[0] == DEPTH + 1
    b, t, d = x_prompt.shape
    bd, ts, _ = x_sample.shape
    n_p, n_s = b * t, bd * ts
    n = n_p + n_s
    n_phys = cache_k.shape[1]
    n_pages = page_table.shape[1]
    tm_p = 256
    tm_s = min(256, n_s)
    tq = min(256, t)
    assert n_p % tm_p == 0 and n_s % tm_s == 0 and t % tq == 0 and t % PAGE == 0 and n % N_HALVES == 0
    assert HG_CHUNK % ts == 0

    pad_f = jnp.zeros((d, LANES - FOX_HEADS), F32)
    w_all = jnp.concatenate([w_in[0][:, :OFF_F + FOX_HEADS], pad_f, w_in[0][:, OFF_F + FOX_HEADS:]], axis=1).astype(BF16)
    bf_pad = jnp.concatenate([b_fox_f[0], jnp.zeros((LANES - FOX_HEADS,), F32)]).reshape(1, LANES)
    post_w = (w_branch_a[0].astype(BF16), w_branch_b[0].astype(BF16), w_out[0].astype(BF16),
              jnp.tile(hg_norm_g[0], HG_HEADS).reshape(1, HG_WIDTH), ln1_g[0].reshape(1, d), ln1_b[0].reshape(1, d),
              w_router[0].astype(BF16), router_bias[0].reshape(1, N_EXPERTS),
              w_sh_gate[0].astype(BF16), w_sh_up[0].astype(BF16), w_sh_down[0].astype(BF16))
    r_all = b + bd
    r_pad = -(-r_all // 8) * 8
    c_all = jnp.concatenate([c_prompt, c_sample, jnp.zeros((r_pad - r_all, d), F32)], axis=0)
    mod = _adaln(c_all, w_ada[0], b_ada[0])
    mod_p = [mod[:b, i * d:(i + 1) * d].reshape(b, 1, d) for i in range(6)]
    reps = tm_s // ts
    mod_s = [jnp.repeat(mod[b:b + bd, i * d:(i + 1) * d], ts, axis=0).reshape(n_s // tm_s, tm_s, d) for i in range(6)]
    tpg_p = t // tm_p

    (q_p, kb_p, vb_p, kf_p, vf_p, lf_p, hq_p, hf_p, hi_p, hog_p, sga_p, sgb_p) = _inproj(
        x_prompt.reshape(n_p, d), mod_p[0], mod_p[1], tpg_p, w_all, bf_pad, hg_lb_logits, tm_p)
    (q_s, kb_s, vb_s, kf_s, vf_s, lf_s, hq_s, hf_s, hi_s, hog_s, sga_s, sgb_s) = _inproj(
        x_sample.reshape(n_s, d), mod_s[0], mod_s[1], 1, w_all, bf_pad, hg_lb_logits, tm_s)

    lf_t = lf_p.reshape(b, t, FOX_HEADS).transpose(0, 2, 1)
    ck = _cum_seq(lf_t.reshape(b * FOX_HEADS * (t // PAGE), PAGE), t // PAGE).reshape(b, FOX_HEADS // 2, 2, t)
    oa_p = _fox_prompt(q_p.reshape(b, t, FOX_WIDTH), kb_p.reshape(b, t, FOX_WIDTH),
                       vb_p.reshape(b, t, FOX_WIDTH), ck, tq)

    rows_c = n_phys * FOX_HEADS
    tb = 4096 if rows_c % 4096 == 0 else rows_c
    cum_pages = _cum_rows(cache_logf[0].transpose(0, 2, 1).reshape(rows_c, PAGE), tb).reshape(n_phys, FOX_HEADS, PAGE)
    lf_new = jnp.pad(lf_s.reshape(bd, ts, FOX_HEADS).transpose(0, 2, 1), ((0, 0), (0, 0), (0, PAGE - ts)))
    cum_new = _cum_rows(lf_new.reshape(bd * FOX_HEADS, PAGE), bd * FOX_HEADS).reshape(bd, FOX_HEADS, PAGE)
    pad8 = lambda a: jnp.pad(a.reshape(bd, ts, FOX_WIDTH), ((0, 0), (0, 8 - ts), (0, 0)))
    pages = 8 if n_pages % 8 == 0 else n_pages
    k_t = cache_k[0].transpose(0, 2, 3, 1).reshape(n_phys, FOX_WIDTH, PAGE)
    v_t = cache_v[0].transpose(0, 2, 3, 1).reshape(n_phys, FOX_WIDTH, PAGE)
    oa_s = _fox_sample(page_table, q_s.reshape(bd, ts, FOX_WIDTH), pad8(kf_s), pad8(vf_s), cum_new,
                       k_t, v_t, cum_pages, pages)

    tc = min(1024, t)
    ohg_p, st_p = _hgrn2(hq_p.reshape(b, t, HG_WIDTH), hf_p.reshape(b, t, HG_WIDTH), hi_p.reshape(b, t, HG_WIDTH),
                         jnp.zeros((b, HG_HEADS, HG_DIM, HG_DIM), F32), tc)
    padc = lambda a, v: jnp.pad(a.reshape(bd, ts, HG_WIDTH), ((0, 0), (0, HG_CHUNK - ts), (0, 0)), constant_values=v)
    ohg_s, st_s = _hgrn2(padc(hq_s, 0.0), padc(hf_s, 1.0), padc(hi_s, 0.0), state_hg[0], HG_CHUNK)
    ohg_s = ohg_s[:, :ts].reshape(n_s, HG_WIDTH)

    x1_p, h2p_p, idx_p, wt_p, shd_p = _post(
        x_prompt.reshape(n_p, d), oa_p.reshape(n_p, FOX_WIDTH), ohg_p.reshape(n_p, HG_WIDTH), hog_p, sga_p, sgb_p,
        mod_p[2], mod_p[3], mod_p[4], tpg_p, post_w, tm_p)
    x1_s, h2p_s, idx_s, wt_s, shd_s = _post(
        x_sample.reshape(n_s, d), oa_s.reshape(n_s, FOX_WIDTH), ohg_s, hog_s, sga_s, sgb_s,
        mod_s[2], mod_s[3], mod_s[4], 1, post_w, tm_s)

    nh = n // N_HALVES
    idx = jnp.concatenate([idx_p[:, :TOP_K], idx_s[:, :TOP_K]], axis=0)
    wts = jnp.concatenate([wt_p[:, :TOP_K], wt_s[:, :TOP_K]], axis=0)
    blk_e, blk_h, n_valid, rows4, rows8, row_w = _dispatch(idx, wts, n, nh)
    a = _moe_up(blk_e, n_valid, rows4, h2p_p.reshape(n_p * 4, LANES), h2p_s.reshape(n_s * 4, LANES),
                w_exp_gate[0], w_exp_up[0])
    routed = _moe_down(blk_e, blk_h, n_valid, rows8, row_w, a, w_exp_down[0], nh).reshape(n, d)

    lg, lb2 = ln2_g[0].reshape(1, d), ln2_b[0].reshape(1, d)
    y_p = _final(x1_p, routed, 0, shd_p, mod_p[5], tpg_p, lg, lb2, tm_p)
    y_s = _final(x1_s, routed, n_p // tm_s, shd_s, mod_s[5], 1, lg, lb2, tm_s)

    ldt, sdt = cache_logf.dtype, state_hg.dtype
    hd = (FOX_HEADS, FOX_HEAD_DIM)
    return (y_p.reshape(b, t, d), y_s.reshape(bd, ts, d),
            kf_p.reshape((1, b, t) + hd), vf_p.reshape((1, b, t) + hd),
            lf_p.reshape(1, b, t, FOX_HEADS).astype(ldt), st_p[None].astype(sdt),
            kf_s.reshape((1, bd, ts) + hd), vf_s.reshape((1, bd, ts) + hd),
            lf_s.reshape(1, bd, ts, FOX_HEADS).astype(ldt), st_s[None].astype(sdt))
```

```python
import functools

import jax
import jax.numpy as jnp
import numpy as np
from jax import lax
from jax.experimental import pallas as pl
from jax.experimental.pallas import tpu as pltpu

F32 = jnp.float32
BF16 = jnp.bfloat16
I32 = jnp.int32

D_MODEL = 1024
FOX_HEADS = 8
FOX_HEAD_DIM = 64
FOX_WIDTH = FOX_HEADS * FOX_HEAD_DIM
HG_HEADS = 4
HG_DIM = 128
HG_WIDTH = HG_HEADS * HG_DIM
HG_CHUNK = 32
N_EXPERTS = 256
TOP_K = 8
EXPERT_DIM = 256
ROUTED_SCALE = 2.5
DEPTH = 1
DEEPNORM_ALPHA = (2.0 * DEPTH) ** 0.25
LN_EPS = 1e-5
RMS_EPS = 1e-6
NEG_INF = -1e30
PAGE = 128
LANES = 128
MOE_BLOCK = 128
N_HALVES = 2
RING_SLOTS = 4
VMEM_LIMIT = 56 * 1024 * 1024

OFF_Q, OFF_K, OFF_V, OFF_F = 0, 512, 1024, 1536
OFF_HQ, OFF_HF, OFF_HI, OFF_HOG = 1664, 2176, 2688, 3200
OFF_GA, OFF_GB, W_ALL_COLS = 3712, 4736, 5760


def _cparams(sem):
    return pltpu.CompilerParams(dimension_semantics=sem, vmem_limit_bytes=VMEM_LIMIT)


def _ln(x):
    mu = jnp.mean(x, axis=-1, keepdims=True)
    xc = x - mu
    var = jnp.mean(xc * xc, axis=-1, keepdims=True)
    return xc * lax.rsqrt(var + LN_EPS)


def _silu(x):
    return x * jax.nn.sigmoid(x)


def _split3(x):
    hi = x.astype(BF16)
    r1 = x - hi.astype(F32)
    mid = r1.astype(BF16)
    lo = (r1 - mid.astype(F32)).astype(BF16)
    return hi, mid, lo


def _dot3(x, m):
    hi, mid, lo = _split3(x)
    d = lambda a: jnp.dot(a, m, preferred_element_type=F32)
    return d(hi) + d(mid) + d(lo)


def _dot_nt(a, b):
    return lax.dot_general(a, b, (((1,), (1,)), ((), ())), preferred_element_type=F32)


def _dot_tn(a, b):
    return lax.dot_general(a, b, (((0,), (0,)), ((), ())), preferred_element_type=F32)


def _adaln_kernel(c_ref, w_ref, b_ref, o_ref):
    s = _silu(c_ref[...]).astype(BF16)
    o_ref[...] = jnp.dot(s, w_ref[...].astype(BF16), preferred_element_type=F32) + b_ref[...]


def _adaln(c, w_ada, b_ada):
    r = c.shape[0]
    tn = 1024
    return pl.pallas_call(
        _adaln_kernel,
        grid=(6 * D_MODEL // tn,),
        in_specs=[pl.BlockSpec((r, D_MODEL), lambda j: (0, 0)),
                  pl.BlockSpec((D_MODEL, tn), lambda j: (0, j)),
                  pl.BlockSpec((1, tn), lambda j: (0, j))],
        out_specs=pl.BlockSpec((r, tn), lambda j: (0, j)),
        out_shape=jax.ShapeDtypeStruct((r, 6 * D_MODEL), F32),
        compiler_params=_cparams(("parallel",)),
        name="adaln",
    )(c, w_ada, b_ada.reshape(1, -1))


def _inproj_kernel(x_ref, sh_ref, sc_ref, w_ref, bf_ref, lbl_ref,
                   q_ref, kb_ref, vb_ref, kf_ref, vf_ref, lf_ref,
                   hq_ref, hf_ref, hi_ref, hog_ref, sga_ref, sgb_ref):
    h = _ln(x_ref[...]) * (1.0 + sc_ref[0]) + sh_ref[0]
    hb = h.astype(BF16)

    def proj(a, b):
        return jnp.dot(hb, w_ref[:, a:b], preferred_element_type=F32)

    q_ref[...] = (proj(OFF_Q, OFF_K) * (FOX_HEAD_DIM ** -0.5)).astype(BF16)
    k = proj(OFF_K, OFF_V)
    kf_ref[...] = k
    kb_ref[...] = k.astype(BF16)
    v = proj(OFF_V, OFF_F)
    vf_ref[...] = v
    vb_ref[...] = v.astype(BF16)
    logf = jax.nn.log_sigmoid(proj(OFF_F, OFF_HQ) + bf_ref[...])
    lf_ref[...] = logf[:, :FOX_HEADS]

    hq_ref[...] = _silu(proj(OFF_HQ, OFF_HF))
    l0 = lbl_ref[0:1, :]
    l1 = lbl_ref[1:2, :]
    mx = jnp.maximum(l0, l1)
    e0 = jnp.exp(l0 - mx)
    lb = e0 / (e0 + jnp.exp(l1 - mx))
    hf_ref[...] = lb + (1.0 - lb) * jax.nn.sigmoid(proj(OFF_HF, OFF_HI))
    hi_ref[...] = proj(OFF_HI, OFF_HOG)
    hog_ref[...] = _silu(proj(OFF_HOG, OFF_GA))
    sga_ref[...] = jax.nn.sigmoid(proj(OFF_GA, OFF_GB))
    sgb_ref[...] = jax.nn.sigmoid(proj(OFF_GB, W_ALL_COLS))


def _inproj(x, sh, sc, tiles_per_group, w_all, bf_pad, lbl, tm):
    n = x.shape[0]
    r = sh.shape[1]
    row = lambda w: pl.BlockSpec((tm, w), lambda i: (i, 0))
    mod = pl.BlockSpec((1, r, D_MODEL), lambda i: (i // tiles_per_group, 0, 0))
    const = lambda s: pl.BlockSpec(s, lambda i: (0, 0))
    sds = lambda w, dt: jax.ShapeDtypeStruct((n, w), dt)
    return pl.pallas_call(
        _inproj_kernel,
        grid=(n // tm,),
        in_specs=[row(D_MODEL), mod, mod,
                  pl.BlockSpec((D_MODEL, W_ALL_COLS), lambda i: (0, 0), pipeline_mode=pl.Buffered(1)),
                  const((1, LANES)), const((2, HG_WIDTH))],
        out_specs=[row(FOX_WIDTH)] * 5 + [row(FOX_HEADS)] + [row(HG_WIDTH)] * 4 + [row(D_MODEL)] * 2,
        out_shape=[sds(FOX_WIDTH, BF16)] * 3 + [sds(FOX_WIDTH, F32)] * 2 + [sds(FOX_HEADS, F32)]
        + [sds(HG_WIDTH, F32)] * 4 + [sds(D_MODEL, F32)] * 2,
        compiler_params=_cparams(("parallel",)),
        name="inproj",
    )(x, sh, sc, w_all, bf_pad, lbl)


def _upper_ones():
    r = np.arange(PAGE)
    return jnp.asarray(r[:, None] <= r[None, :], BF16)


def _cum_rows_kernel(x_ref, u_ref, o_ref):
    o_ref[...] = _dot3(x_ref[...], u_ref[...])


def _cum_rows(x, tb):
    n = x.shape[0]
    return pl.pallas_call(
        _cum_rows_kernel,
        grid=(n // tb,),
        in_specs=[pl.BlockSpec((tb, PAGE), lambda i: (i, 0)), pl.BlockSpec((PAGE, PAGE), lambda i: (0, 0))],
        out_specs=pl.BlockSpec((tb, PAGE), lambda i: (i, 0)),
        out_shape=jax.ShapeDtypeStruct((n, PAGE), F32),
        compiler_params=_cparams(("parallel",)),
        name="cum_rows",
    )(x, _upper_ones())


def _cum_seq_kernel(x_ref, u_ref, g_ref, o_ref):
    x = x_ref[...]
    within = _dot3(x, u_ref[...])
    tot = _dot3(x, jnp.ones((PAGE, PAGE), BF16))
    hi, mid, lo = _split3(tot)
    g = g_ref[...]
    d = lambda a: jnp.dot(g, a, preferred_element_type=F32)
    o_ref[...] = within + d(hi) + d(mid) + d(lo)


def _cum_seq(x, rows_per_seq):
    nr = x.shape[0]
    r = np.arange(nr)
    g = (r[:, None] // rows_per_seq == r[None, :] // rows_per_seq) & (r[None, :] < r[:, None])
    return pl.pallas_call(
        _cum_seq_kernel,
        out_shape=jax.ShapeDtypeStruct(x.shape, F32),
        compiler_params=pltpu.CompilerParams(vmem_limit_bytes=VMEM_LIMIT),
        name="cum_seq",
    )(x, _upper_ones(), jnp.asarray(g, BF16))


def _fox_prompt_kernel(q_ref, k_ref, v_ref, ck_ref, o_ref, m_sc, l_sc, acc_sc, cq_sc, s_sc, p_sc, *, tq, strip):
    i = pl.program_id(2)
    q = q_ref[0]
    lo_lanes = lax.broadcasted_iota(I32, (tq, LANES), 1) < FOX_HEAD_DIM
    zero = jnp.zeros_like(q)
    qh = (jnp.where(lo_lanes, q, zero), jnp.where(lo_lanes, zero, q))
    q_off = pl.multiple_of(i * tq, tq)
    cq_rows = ck_ref[0, 0, :, pl.ds(q_off, tq)]
    for h in range(2):
        cq_sc[h] = jnp.transpose(jnp.broadcast_to(cq_rows[h:h + 1, :], (LANES, tq)))
    reps = tq // LANES
    row = lax.broadcasted_iota(I32, (strip, tq), 0)
    col = lax.broadcasted_iota(I32, (strip, tq), 1)

    m_sc[...] = jnp.full(m_sc.shape, NEG_INF, F32)
    l_sc[...] = jnp.zeros(l_sc.shape, F32)
    acc_sc[...] = jnp.zeros(acc_sc.shape, F32)

    def scores(j, slot):
        kt = k_ref[0, pl.ds(pl.multiple_of(j * tq, tq), tq), :]
        for h in range(2):
            s_sc[slot, h] = _dot_nt(qh[h], kt)

    scores(0, 0)

    def kv_step(j, slot, diag):
        if not diag:
            scores(j + 1, 1 - slot)
        k_off = pl.multiple_of(j * tq, tq)
        vt = v_ref[0, pl.ds(k_off, tq), :]
        ckt = ck_ref[0, 0, :, pl.ds(k_off, tq)]
        for h in range(2):
            ck_row = ckt[h:h + 1, :]
            for r0 in range(0, tq, strip):
                rs = slice(r0, r0 + strip)
                logits = s_sc[slot, h, rs, :] + jnp.concatenate([cq_sc[h, rs, :]] * reps, axis=1) - ck_row
                if diag:
                    logits = jnp.where(col <= row + r0, logits, NEG_INF)
                m_prev = m_sc[h, rs, :]
                m_new = jnp.maximum(m_prev, jnp.max(logits, axis=1, keepdims=True))
                p = jnp.exp(logits - jnp.concatenate([m_new] * reps, axis=1))
                alpha = jnp.exp(m_prev - m_new)
                l_sc[h, rs, :] = alpha * l_sc[h, rs, :] + jnp.sum(p, axis=1, keepdims=True)
                acc_sc[h, rs, :] = alpha * acc_sc[h, rs, :]
                m_sc[h, rs, :] = m_new
                p_sc[h, rs, :] = p.astype(BF16)
        for h in range(2):
            acc_sc[h] += jnp.dot(p_sc[h], vt, preferred_element_type=F32)

    def body(jj, c):
        kv_step(2 * jj, 0, False)
        kv_step(2 * jj + 1, 1, False)
        return c

    lax.fori_loop(0, i // 2, body, 0)

    @pl.when(i % 2 == 0)
    def _():
        kv_step(i, 0, True)

    @pl.when(i % 2 == 1)
    def _():
        kv_step(i - 1, 0, False)
        kv_step(i, 1, True)

    o = jnp.where(lo_lanes, acc_sc[0] / l_sc[0], acc_sc[1] / l_sc[1])
    o_ref[0] = o.astype(BF16)


def _fox_prompt(q, k, v, ck, tq):
    b, t, _ = q.shape
    pairs = FOX_HEADS // 2
    return pl.pallas_call(
        functools.partial(_fox_prompt_kernel, tq=tq, strip=32),
        grid=(b, pairs, t // tq),
        in_specs=[pl.BlockSpec((1, tq, LANES), lambda bi, p, i: (bi, i, p)),
                  pl.BlockSpec((1, t, LANES), lambda bi, p, i: (bi, 0, p)),
                  pl.BlockSpec((1, t, LANES), lambda bi, p, i: (bi, 0, p)),
                  pl.BlockSpec((1, 1, 2, t), lambda bi, p, i: (bi, p, 0, 0))],
        out_specs=pl.BlockSpec((1, tq, LANES), lambda bi, p, i: (bi, i, p)),
        out_shape=jax.ShapeDtypeStruct((b, t, FOX_WIDTH), BF16),
        scratch_shapes=[pltpu.VMEM((2, tq, LANES), F32)] * 4
        + [pltpu.VMEM((2, 2, tq, tq), F32), pltpu.VMEM((2, tq, tq), BF16)],
        compiler_params=_cparams(("parallel", "parallel", "arbitrary")),
        name="fox_prompt",
    )(q, k, v, ck)


def _fox_sample_kernel(pt_ref, q_ref, kn_ref, vn_ref, cn_ref, *rest, pages):
    k_refs = rest[:pages]
    v_refs = rest[pages:2 * pages]
    c_refs = rest[2 * pages:3 * pages]
    o_ref, m_sc, l_sc, acc_sc, off_sc = rest[3 * pages:]
    g = pl.program_id(1)
    n_q = q_ref.shape[1]
    rows = n_q * FOX_HEADS

    @pl.when(g == 0)
    def _():
        m_sc[...] = jnp.full(m_sc.shape, NEG_INF, F32)
        l_sc[...] = jnp.zeros(l_sc.shape, F32)
        acc_sc[...] = jnp.zeros(acc_sc.shape, F32)
        off_sc[...] = jnp.zeros(off_sc.shape, F32)

    q4 = q_ref[0].astype(F32)
    head_of_lane = lax.broadcasted_iota(I32, (FOX_HEADS, FOX_WIDTH), 1) // FOX_HEAD_DIM
    bmask = head_of_lane == lax.broadcasted_iota(I32, (FOX_HEADS, FOX_WIDTH), 0)
    wq = jnp.concatenate(
        [jnp.where(bmask, jnp.broadcast_to(q4[t:t + 1, :], (FOX_HEADS, FOX_WIDTH)), 0.0) for t in range(n_q)], axis=0)

    def update(s_all, pv_fn):
        m_prev = m_sc[...]
        m_new = jnp.maximum(m_prev, jnp.max(s_all, axis=1, keepdims=True))
        p = jnp.exp(s_all - jnp.concatenate([m_new] * (s_all.shape[1] // LANES), axis=1))
        alpha = jnp.exp(m_prev - m_new)
        l_sc[...] = alpha * l_sc[...] + jnp.sum(p, axis=1, keepdims=True)
        acc_sc[...] = jnp.concatenate([alpha] * (FOX_WIDTH // LANES), axis=1) * acc_sc[...] + pv_fn(p)
        m_sc[...] = m_new

    off = off_sc[...]
    s_list = []
    for i in range(pages):
        within = c_refs[i][0]
        s = jnp.dot(wq, k_refs[i][0], preferred_element_type=F32)
        s_list.append(s - jnp.concatenate([off + within] * n_q, axis=0))
        off = off + jnp.broadcast_to(within[:, PAGE - 1:PAGE], (FOX_HEADS, PAGE))
    off_sc[...] = off

    def pv_pages(p):
        pv = _dot_nt(p[:, :PAGE], v_refs[0][0])
        for i in range(1, pages):
            pv = pv + _dot_nt(p[:, i * PAGE:(i + 1) * PAGE], v_refs[i][0])
        return pv

    update(jnp.concatenate(s_list, axis=1), pv_pages)

    @pl.when(g == pl.num_programs(1) - 1)
    def _():
        pad = jnp.zeros((PAGE - kn_ref.shape[1], FOX_WIDTH), F32)
        kn = jnp.concatenate([kn_ref[0], pad], axis=0)
        vn = jnp.concatenate([vn_ref[0], pad], axis=0)
        s = _dot_nt(wq, kn) - jnp.concatenate([off + cn_ref[0]] * n_q, axis=0)
        t_of_row = lax.broadcasted_iota(I32, (rows, PAGE), 0) // FOX_HEADS
        key = lax.broadcasted_iota(I32, (rows, PAGE), 1)
        update(jnp.where(key <= t_of_row, s, NEG_INF),
               lambda p: jnp.dot(p, vn, preferred_element_type=F32))
        o32 = acc_sc[...] / jnp.concatenate([l_sc[...]] * (FOX_WIDTH // LANES), axis=1)
        outs = []
        for t in range(n_q):
            blk = o32[t * FOX_HEADS:(t + 1) * FOX_HEADS, :]
            outs.append(jnp.sum(jnp.where(bmask, blk, 0.0), axis=0, keepdims=True))
        o_ref[0] = jnp.concatenate(outs, axis=0).astype(BF16)


def _fox_sample(page_table, q, k_new, v_new, cum_new, cache_k, cache_v, cum_pages, pages):
    bd, n_q, _ = q.shape
    n_pages = page_table.shape[1]
    seq = lambda s: pl.BlockSpec((1,) + s, lambda b, g, pt: (b, 0, 0))
    page = lambda s, i: pl.BlockSpec((1,) + s, lambda b, g, pt, i=i: (pt[b, g * pages + i], 0, 0))
    in_specs = ([seq((n_q, FOX_WIDTH)), seq(k_new.shape[1:]), seq(v_new.shape[1:]), seq((FOX_HEADS, PAGE))]
                + [page((FOX_WIDTH, PAGE), i) for i in range(pages)]
                + [page((FOX_WIDTH, PAGE), i) for i in range(pages)]
                + [page((FOX_HEADS, PAGE), i) for i in range(pages)])
    rows = n_q * FOX_HEADS
    return pl.pallas_call(
        functools.partial(_fox_sample_kernel, pages=pages),
        grid_spec=pltpu.PrefetchScalarGridSpec(
            num_scalar_prefetch=1,
            grid=(bd, n_pages // pages),
            in_specs=in_specs,
            out_specs=pl.BlockSpec((1, n_q, FOX_WIDTH), lambda b, g, pt: (b, 0, 0)),
            scratch_shapes=[pltpu.VMEM((rows, LANES), F32), pltpu.VMEM((rows, LANES), F32),
                            pltpu.VMEM((rows, FOX_WIDTH), F32), pltpu.VMEM((FOX_HEADS, PAGE), F32)]),
        out_shape=jax.ShapeDtypeStruct((bd, n_q, FOX_WIDTH), BF16),
        compiler_params=_cparams(("parallel", "arbitrary")),
        name="fox_sample",
    )(page_table, q, k_new, v_new, cum_new, *([cache_k] * pages), *([cache_v] * pages), *([cum_pages] * pages))


def _hgrn2_kernel(q_ref, f_ref, i_ref, s0_ref, o_ref, s_ref, st_sc, *, n_chunks, nb):
    c = HG_CHUNK
    tci = pl.program_id(1)
    chains = [(bi, h) for bi in range(nb) for h in range(HG_HEADS)]

    @pl.when(tci == 0)
    def _():
        for n, (bi, h) in enumerate(chains):
            st_sc[n] = jnp.transpose(s0_ref[bi, h])

    tril = lax.broadcasted_iota(I32, (c, c), 1) <= lax.broadcasted_iota(I32, (c, c), 0)
    ltri = tril.astype(BF16)

    def chunk(ci, carry):
        r0 = pl.multiple_of(ci * c, c)
        for n, (bi, h) in enumerate(chains):
            ls = slice(h * HG_DIM, (h + 1) * HG_DIM)
            f = f_ref[bi, pl.ds(r0, c), ls]
            qc = q_ref[bi, pl.ds(r0, c), ls]
            ic = i_ref[bi, pl.ds(r0, c), ls].astype(BF16)
            kc = 1.0 - f
            hi, mid, lo = _split3(jnp.log(f))
            d = lambda a: jnp.dot(ltri, a, preferred_element_type=F32)
            g = d(hi) + d(mid) + d(lo)
            g_last = g[c - 1:c, :]
            q_dec = (qc * jnp.exp(g)).astype(BF16)
            k_inv = (kc * jnp.exp(-g)).astype(BF16)
            k_end = (kc * jnp.exp(g_last - g)).astype(BF16)
            a = jnp.where(tril, _dot_nt(q_dec, k_inv), 0.0)
            st = st_sc[n]
            o = jnp.dot(a.astype(BF16), ic, preferred_element_type=F32) + _dot_nt(q_dec, st.astype(BF16))
            o_ref[bi, pl.ds(r0, c), ls] = o
            st_sc[n] = st * jnp.exp(g_last) + _dot_tn(ic, k_end)
        return carry

    lax.fori_loop(0, n_chunks, chunk, 0)

    @pl.when(tci == pl.num_programs(1) - 1)
    def _():
        for n, (bi, h) in enumerate(chains):
            s_ref[bi, h] = jnp.transpose(st_sc[n])


def _hgrn2(q, f, i, s0, tc, nb):
    b, t, _ = q.shape
    tok = pl.BlockSpec((nb, tc, HG_WIDTH), lambda g, ti: (g, ti, 0))
    st = pl.BlockSpec((nb, HG_HEADS, HG_DIM, HG_DIM), lambda g, ti: (g, 0, 0, 0))
    return pl.pallas_call(
        functools.partial(_hgrn2_kernel, n_chunks=tc // HG_CHUNK, nb=nb),
        grid=(b // nb, t // tc),
        in_specs=[tok, tok, tok, st],
        out_specs=[tok, st],
        out_shape=[jax.ShapeDtypeStruct((b, t, HG_WIDTH), F32),
                   jax.ShapeDtypeStruct((b, HG_HEADS, HG_DIM, HG_DIM), F32)],
        scratch_shapes=[pltpu.VMEM((nb * HG_HEADS, HG_DIM, HG_DIM), F32)],
        compiler_params=_cparams(("parallel", "arbitrary")),
        name="hgrn2",
    )(q, f, i, s0)


def _post_kernel(x_ref, oa_ref, ohg_ref, hog_ref, sga_ref, sgb_ref, g1_ref, sh2_ref, sc2_ref,
                 wa_ref, wb_ref, wo_ref, ng_ref, l1g_ref, l1b_ref, wr_ref, rb_ref,
                 wsg_ref, wsu_ref, wsd_ref,
                 x1_ref, h2p_ref, idx_ref, wt_ref, shd_ref):
    tm = x_ref.shape[0]
    ohg = ohg_ref[...]
    heads = []
    for h in range(HG_HEADS):
        oh = ohg[:, h * HG_DIM:(h + 1) * HG_DIM]
        heads.append(oh * lax.rsqrt(jnp.mean(oh * oh, axis=-1, keepdims=True) + RMS_EPS))
    ob = (jnp.concatenate(heads, axis=1) * ng_ref[...] * hog_ref[...]).astype(BF16)
    mixed = (sga_ref[...] * jnp.dot(oa_ref[...], wa_ref[...], preferred_element_type=F32)
             + sgb_ref[...] * jnp.dot(ob, wb_ref[...], preferred_element_type=F32))
    y = jnp.dot(mixed.astype(BF16), wo_ref[...], preferred_element_type=F32)
    x1 = _ln(DEEPNORM_ALPHA * x_ref[...] + g1_ref[0] * y) * l1g_ref[...] + l1b_ref[...]
    x1_ref[...] = x1
    h2 = _ln(x1) * (1.0 + sc2_ref[0]) + sh2_ref[0]
    h2b = h2.astype(BF16)

    half = D_MODEL // 2
    lo_bits = lax.shift_right_logical(pltpu.bitcast(h2b[:, :half].astype(F32), I32), 16)
    hi_bits = pltpu.bitcast(h2b[:, half:].astype(F32), I32) & jnp.int32(-65536)
    h2p_ref[...] = hi_bits | lo_bits

    scores = jax.nn.sigmoid(jnp.dot(h2b, wr_ref[...], preferred_element_type=F32))
    sel = scores + rb_ref[...]
    lane_e = lax.broadcasted_iota(I32, (tm, N_EXPERTS), 1).astype(F32)
    lane_o = lax.broadcasted_iota(I32, (tm, LANES), 1)
    idx_acc = jnp.zeros((tm, LANES), F32)
    w_acc = jnp.zeros((tm, LANES), F32)
    w_sum = jnp.zeros((tm, 1), F32)
    for k in range(TOP_K):
        mx = jnp.max(sel, axis=1, keepdims=True)
        ik = jnp.min(jnp.where(sel == mx, lane_e, float(N_EXPERTS)), axis=1, keepdims=True)
        hit = lane_e == ik
        wk = jnp.sum(jnp.where(hit, scores, 0.0), axis=1, keepdims=True)
        sel = jnp.where(hit, -jnp.inf, sel)
        idx_acc = jnp.where(lane_o == k, ik, idx_acc)
        w_acc = jnp.where(lane_o == k, wk, w_acc)
        w_sum = w_sum + wk
    idx_ref[...] = idx_acc.astype(I32)
    wt_ref[...] = ROUTED_SCALE * w_acc / w_sum

    sg = jnp.dot(h2b, wsg_ref[...], preferred_element_type=F32)
    su = jnp.dot(h2b, wsu_ref[...], preferred_element_type=F32)
    shd_ref[...] = jnp.dot((_silu(sg) * su).astype(BF16), wsd_ref[...], preferred_element_type=F32)


def _post(x, oa, ohg, hog, sga, sgb, g1, sh2, sc2, tiles_per_group, wts, tm):
    n = x.shape[0]
    r = g1.shape[1]
    row = lambda w: pl.BlockSpec((tm, w), lambda i: (i, 0))
    mod = pl.BlockSpec((1, r, D_MODEL), lambda i: (i // tiles_per_group, 0, 0))
    const = lambda a: pl.BlockSpec(a.shape, lambda i: (0, 0))
    sds = lambda w, dt: jax.ShapeDtypeStruct((n, w), dt)
    return pl.pallas_call(
        _post_kernel,
        grid=(n // tm,),
        in_specs=[row(D_MODEL), row(FOX_WIDTH), row(HG_WIDTH), row(HG_WIDTH), row(D_MODEL), row(D_MODEL),
                  mod, mod, mod] + [const(a) for a in wts],
        out_specs=[row(D_MODEL), row(D_MODEL // 2), row(LANES), row(LANES), row(D_MODEL)],
        out_shape=[sds(D_MODEL, F32), sds(D_MODEL // 2, I32), sds(LANES, I32), sds(LANES, F32),
                   sds(D_MODEL, F32)],
        compiler_params=_cparams(("parallel",)),
        name="post",
    )(x, oa, ohg, hog, sga, sgb, g1, sh2, sc2, *wts)


def _ring_copy(src_hbm, dst, sems, blk, slot):
    return pltpu.make_async_copy(src_hbm.at[blk], dst.at[slot], sems.at[slot])


def _ring_advance(b, nb, srcs, fetch_blk):
    ahead = RING_SLOTS - 1

    @pl.when(b == 0)
    def _():
        for i in range(ahead):
            for src, dst, sems in srcs:
                _ring_copy(src, dst, sems, fetch_blk(i), i).start()

    slot = b % RING_SLOTS
    for src, dst, sems in srcs:
        _ring_copy(src, dst, sems, fetch_blk(b), slot).wait()

    @pl.when(b + ahead < nb)
    def _():
        for src, dst, sems in srcs:
            _ring_copy(src, dst, sems, fetch_blk(b + ahead), (b + ahead) % RING_SLOTS).start()

    return slot


def _moe_up_kernel(be_ref, nv_ref, rows_hbm, hp_ref, hs_ref, wg_ref, wu_ref, a_ref,
                   hv_sc, xt_sc, rows_sm, sem, rsem, *, np4):
    b = pl.program_id(0)

    @pl.when(b == 0)
    def _():
        cp = pltpu.make_async_copy(hp_ref, hv_sc.at[pl.ds(0, np4)], sem.at[0])
        cs = pltpu.make_async_copy(hs_ref, hv_sc.at[pl.ds(np4, hs_ref.shape[0])], sem.at[1])
        cp.start()
        cs.start()
        cp.wait()
        cs.wait()

    slot = _ring_advance(b, pl.num_programs(0), [(rows_hbm, rows_sm, rsem)], lambda i: i)

    @pl.when(b < nv_ref[0])
    def _():
        for m in range(MOE_BLOCK):
            t4 = pl.multiple_of(rows_sm[slot, 0, m], 4)
            xt_sc[4 * m:4 * m + 4, :] = hv_sc[pl.ds(t4, 4), :]
        lo, hi = [], []
        for j in range(4):
            w = xt_sc[pl.ds(j, MOE_BLOCK, stride=4), :]
            lo.append(pltpu.bitcast(w << 16, F32))
            hi.append(pltpu.bitcast(w & jnp.int32(-65536), F32))
        x = jnp.concatenate(lo + hi, axis=1)
        gate = jnp.dot(x, wg_ref[0], preferred_element_type=F32)
        up = jnp.dot(x, wu_ref[0], preferred_element_type=F32)
        a_ref[...] = (_silu(gate) * up).astype(BF16)

    @pl.when(b >= nv_ref[0])
    def _():
        a_ref[...] = jnp.zeros(a_ref.shape, BF16)


def _moe_up(blk_e, n_valid, rows4, h2p_p, h2p_s, w_gate, w_up):
    nb = blk_e.shape[0]
    assert nb >= RING_SLOTS
    np4, ns4 = h2p_p.shape[0], h2p_s.shape[0]
    wspec = pl.BlockSpec((1, D_MODEL, EXPERT_DIM), lambda b, be, nv: (be[b], 0, 0))
    hbm = pl.BlockSpec(memory_space=pl.ANY)
    return pl.pallas_call(
        functools.partial(_moe_up_kernel, np4=np4),
        grid_spec=pltpu.PrefetchScalarGridSpec(
            num_scalar_prefetch=2,
            grid=(nb,),
            in_specs=[hbm, hbm, hbm, wspec, wspec],
            out_specs=pl.BlockSpec((MOE_BLOCK, EXPERT_DIM), lambda b, be, nv: (b, 0)),
            scratch_shapes=[pltpu.VMEM((np4 + ns4, LANES), I32), pltpu.VMEM((4 * MOE_BLOCK, LANES), I32),
                            pltpu.SMEM((RING_SLOTS, 1, MOE_BLOCK), I32),
                            pltpu.SemaphoreType.DMA((2,)), pltpu.SemaphoreType.DMA((RING_SLOTS,))]),
        out_shape=jax.ShapeDtypeStruct((nb * MOE_BLOCK, EXPERT_DIM), BF16),
        compiler_params=_cparams(("arbitrary",)),
        name="moe_up",
    )(blk_e, n_valid, rows4, h2p_p, h2p_s, w_gate, w_up)


def _moe_down_kernel(be_ref, bh_ref, nv_ref, sb_ref, rows_hbm, rw_hbm, a_hbm, wd_ref, out_ref,
                     acc_sc, y_sc, rows_sm, rw_sm, a_sc, sem, rsem, wsem, asem, *, nh8, unroll):
    b = pl.program_id(0)
    nb = pl.num_programs(0)
    prev = jnp.maximum(b - 1, 0)
    nxt = jnp.minimum(b + 1, nb - 1)

    @pl.when((b == 0) | (bh_ref[b] != bh_ref[prev]))
    def _():
        acc_sc[...] = jnp.zeros(acc_sc.shape, F32)

    slot = _ring_advance(b, nb, [(rows_hbm, rows_sm, rsem), (rw_hbm, rw_sm, wsem), (a_hbm, a_sc, asem)],
                         lambda i: sb_ref[i])

    @pl.when(b < nv_ref[0])
    def _():
        y = jnp.dot(a_sc[slot].astype(F32), wd_ref[0], preferred_element_type=F32)
        tiles = D_MODEL // LANES
        for j in range(tiles):
            y_sc[pl.ds(j, MOE_BLOCK, stride=tiles), :] = y[:, j * LANES:(j + 1) * LANES]
        for m0 in range(0, MOE_BLOCK, unroll):
            new = []
            for m in range(m0, m0 + unroll):
                r8 = pl.multiple_of(rows_sm[slot, 0, m], 8)
                new.append((r8, acc_sc[pl.ds(r8, 8), :] + rw_sm[slot, 0, m] * y_sc[8 * m:8 * m + 8, :]))
            for r8, val in new:
                acc_sc[pl.ds(r8, 8), :] = val

    @pl.when((b == nb - 1) | (bh_ref[nxt] != bh_ref[b]))
    def _():
        cp = pltpu.make_async_copy(acc_sc.at[pl.ds(0, nh8)], out_ref.at[bh_ref[b]], sem.at[0])
        cp.start()
        cp.wait()


def _moe_down(blk_e, blk_h, n_valid, src_blk, rows8, row_w, a, w_down, nh):
    nb = blk_e.shape[0]
    assert nb >= RING_SLOTS
    nh8 = nh * 8
    hbm = pl.BlockSpec(memory_space=pl.ANY)
    ring_sems = pltpu.SemaphoreType.DMA((RING_SLOTS,))
    return pl.pallas_call(
        functools.partial(_moe_down_kernel, nh8=nh8, unroll=8),
        grid_spec=pltpu.PrefetchScalarGridSpec(
            num_scalar_prefetch=4,
            grid=(nb,),
            in_specs=[hbm, hbm, hbm,
                      pl.BlockSpec((1, EXPERT_DIM, D_MODEL), lambda b, be, bh, nv, sb: (be[b], 0, 0))],
            out_specs=pl.BlockSpec(memory_space=pl.ANY),
            scratch_shapes=[pltpu.VMEM((nh8 + 8, LANES), F32), pltpu.VMEM((8 * MOE_BLOCK, LANES), F32),
                            pltpu.SMEM((RING_SLOTS, 1, MOE_BLOCK), I32), pltpu.SMEM((RING_SLOTS, 1, MOE_BLOCK), F32),
                            pltpu.VMEM((RING_SLOTS, MOE_BLOCK, EXPERT_DIM), BF16),
                            pltpu.SemaphoreType.DMA((1,)), ring_sems, ring_sems, ring_sems]),
        out_shape=jax.ShapeDtypeStruct((N_HALVES, nh8, LANES), F32),
        compiler_params=_cparams(("arbitrary",)),
        name="moe_down",
    )(blk_e, blk_h, n_valid, src_blk, rows8, row_w, a.reshape(nb, MOE_BLOCK, EXPERT_DIM), w_down)


def _dispatch(idx, wts, n, nh):
    n_pairs = n * TOP_K
    n_groups = N_HALVES * N_EXPERTS
    fill = MOE_BLOCK - 1
    n_rows = -(-(n_pairs + n_groups * fill) // MOE_BLOCK) * MOE_BLOCK
    nb = n_rows // MOE_BLOCK
    tok = jnp.arange(n_pairs, dtype=I32) // TOP_K
    grp = idx.reshape(n_pairs) * N_HALVES + tok // nh
    gid = jnp.arange(n_groups, dtype=I32)
    counts = jnp.sum((grp[:, None] == gid[None, :]).astype(I32), axis=0)
    need = (-counts) % MOE_BLOCK
    fill_key = jnp.where(jnp.arange(fill, dtype=I32)[None, :] < need[:, None], gid[:, None], n_groups)
    n_tail = n_rows - n_pairs - n_groups * fill
    keys = jnp.concatenate([grp, fill_key.reshape(-1), jnp.full((n_tail,), n_groups, I32)])
    n_fill = n_rows - n_pairs
    toks = jnp.concatenate([tok, jnp.full((n_fill,), -1, I32)])
    ws = jnp.concatenate([wts.reshape(n_pairs), jnp.zeros((n_fill,), F32)])
    s_key, s_tok, s_w = lax.sort((keys, toks, ws), num_keys=1)
    real = s_tok >= 0
    rows4 = jnp.where(real, s_tok * 4, 0)
    rows8 = jnp.where(real, (s_tok - (s_key % N_HALVES) * nh) * 8, nh * 8)
    blk_key = s_key[::MOE_BLOCK]
    valid = blk_key < n_groups
    n_valid = jnp.sum(valid.astype(I32))
    blk_e = jnp.where(valid, blk_key // N_HALVES, 0)
    blk_h = jnp.where(valid, blk_key % N_HALVES, N_HALVES - 1)
    blk_id = jnp.arange(nb, dtype=I32)
    _, src_blk = lax.sort((jnp.where(valid, blk_h * N_EXPERTS + blk_e, n_groups), blk_id), num_keys=1)
    shape3 = (nb, 1, MOE_BLOCK)
    return (blk_e, blk_h, n_valid.reshape(1), src_blk, blk_e[src_blk], blk_h[src_blk],
            rows4.reshape(shape3), rows8.reshape(shape3), s_w.reshape(shape3))


def _final_kernel(x1_ref, r_ref, s_ref, g2_ref, lg_ref, lb_ref, o_ref):
    y = r_ref[...] + s_ref[...]
    o_ref[...] = _ln(DEEPNORM_ALPHA * x1_ref[...] + g2_ref[0] * y) * lg_ref[...] + lb_ref[...]


def _final(x1, routed, row_off_tiles, shared, g2, tiles_per_group, ln_g, ln_b, tm):
    n = x1.shape[0]
    r = g2.shape[1]
    row = pl.BlockSpec((tm, D_MODEL), lambda i: (i, 0))
    return pl.pallas_call(
        _final_kernel,
        grid=(n // tm,),
        in_specs=[row, pl.BlockSpec((tm, D_MODEL), lambda i: (i + row_off_tiles, 0)), row,
                  pl.BlockSpec((1, r, D_MODEL), lambda i: (i // tiles_per_group, 0, 0)),
                  pl.BlockSpec((1, D_MODEL), lambda i: (0, 0)), pl.BlockSpec((1, D_MODEL), lambda i: (0, 0))],
        out_specs=row,
        out_shape=jax.ShapeDtypeStruct((n, D_MODEL), F32),
        compiler_params=_cparams(("parallel",)),
        name="final",
    )(x1, routed, shared, g2, ln_g, ln_b)


def kernel(x_prompt, x_sample, c_prompt, c_sample, cache_k, cache_v, cache_logf, state_hg, page_table, w_ada, b_ada, w_in, b_fox_f, hg_lb_logits, hg_norm_g, w_branch_a, w_branch_b, w_out, ln1_g, ln1_b, w_router, router_bias, w_exp_gate, w_exp_up, w_exp_down, w_sh_gate, w_sh_up, w_sh_down, ln2_g, ln2_b):
    assert w_ada.shape[0] == DEPTH and hg_lb_logits.shape[0] == DEPTH + 1
    b, t, d = x_prompt.shape
    bd, ts, _ = x_sample.shape
    n_p, n_s = b * t, bd * ts
    n = n_p + n_s
    n_phys = cache_k.shape[1]
    n_pages = page_table.shape[1]
    tm_p = 256
    tm_s = min(256, n_s)
    tq = min(256, t)
    assert n_p % tm_p == 0 and n_s % tm_s == 0 and t % tq == 0 and t % PAGE == 0 and n % N_HALVES == 0
    assert HG_CHUNK % ts == 0

    pad_f = jnp.zeros((d, LANES - FOX_HEADS), F32)
    w_all = jnp.concatenate([w_in[0][:, :OFF_F + FOX_HEADS], pad_f, w_in[0][:, OFF_F + FOX_HEADS:]], axis=1).astype(BF16)
    bf_pad = jnp.concatenate([b_fox_f[0], jnp.zeros((LANES - FOX_HEADS,), F32)]).reshape(1, LANES)
    post_w = (w_branch_a[0].astype(BF16), w_branch_b[0].astype(BF16), w_out[0].astype(BF16),
              jnp.tile(hg_norm_g[0], HG_HEADS).reshape(1, HG_WIDTH), ln1_g[0].reshape(1, d), ln1_b[0].reshape(1, d),
              w_router[0].astype(BF16), router_bias[0].reshape(1, N_EXPERTS),
              w_sh_gate[0].astype(BF16), w_sh_up[0].astype(BF16), w_sh_down[0].astype(BF16))
    r_all = b + bd
    r_pad = -(-r_all // 8) * 8
    c_all = jnp.concatenate([c_prompt, c_sample, jnp.zeros((r_pad - r_all, d), F32)], axis=0)
    mod = _adaln(c_all, w_ada[0], b_ada[0])
    mod_p = [mod[:b, i * d:(i + 1) * d].reshape(b, 1, d) for i in range(6)]
    reps = tm_s // ts
    mod_s = [jnp.repeat(mod[b:b + bd, i * d:(i + 1) * d], ts, axis=0).reshape(n_s // tm_s, tm_s, d) for i in range(6)]
    tpg_p = t // tm_p

    (q_p, kb_p, vb_p, kf_p, vf_p, lf_p, hq_p, hf_p, hi_p, hog_p, sga_p, sgb_p) = _inproj(
        x_prompt.reshape(n_p, d), mod_p[0], mod_p[1], tpg_p, w_all, bf_pad, hg_lb_logits, tm_p)
    (q_s, kb_s, vb_s, kf_s, vf_s, lf_s, hq_s, hf_s, hi_s, hog_s, sga_s, sgb_s) = _inproj(
        x_sample.reshape(n_s, d), mod_s[0], mod_s[1], 1, w_all, bf_pad, hg_lb_logits, tm_s)

    lf_t = lf_p.reshape(b, t, FOX_HEADS).transpose(0, 2, 1)
    ck = _cum_seq(lf_t.reshape(b * FOX_HEADS * (t // PAGE), PAGE), t // PAGE).reshape(b, FOX_HEADS // 2, 2, t)
    oa_p = _fox_prompt(q_p.reshape(b, t, FOX_WIDTH), kb_p.reshape(b, t, FOX_WIDTH),
                       vb_p.reshape(b, t, FOX_WIDTH), ck, tq)

    rows_c = n_phys * FOX_HEADS
    tb = 4096 if rows_c % 4096 == 0 else rows_c
    cum_pages = _cum_rows(cache_logf[0].transpose(0, 2, 1).reshape(rows_c, PAGE), tb).reshape(n_phys, FOX_HEADS, PAGE)
    lf_new = jnp.pad(lf_s.reshape(bd, ts, FOX_HEADS).transpose(0, 2, 1), ((0, 0), (0, 0), (0, PAGE - ts)))
    cum_new = _cum_rows(lf_new.reshape(bd * FOX_HEADS, PAGE), bd * FOX_HEADS).reshape(bd, FOX_HEADS, PAGE)
    pad8 = lambda a: jnp.pad(a.reshape(bd, ts, FOX_WIDTH), ((0, 0), (0, 8 - ts), (0, 0)))
    pages = 16 if n_pages % 16 == 0 else n_pages
    k_t = cache_k[0].transpose(0, 2, 3, 1).reshape(n_phys, FOX_WIDTH, PAGE)
    v_t = cache_v[0].transpose(0, 2, 3, 1).reshape(n_phys, FOX_WIDTH, PAGE)
    oa_s = _fox_sample(page_table, q_s.reshape(bd, ts, FOX_WIDTH), pad8(kf_s), pad8(vf_s), cum_new,
                       k_t, v_t, cum_pages, pages)

    tc = min(512, t)
    nb_p = 2 if b % 2 == 0 else 1
    nb_s = 2 if bd % 2 == 0 else 1
    ohg_p, st_p = _hgrn2(hq_p.reshape(b, t, HG_WIDTH), hf_p.reshape(b, t, HG_WIDTH), hi_p.reshape(b, t, HG_WIDTH),
                         jnp.zeros((b, HG_HEADS, HG_DIM, HG_DIM), F32), tc, nb_p)
    padc = lambda a, v: jnp.pad(a.reshape(bd, ts, HG_WIDTH), ((0, 0), (0, HG_CHUNK - ts), (0, 0)), constant_values=v)
    ohg_s, st_s = _hgrn2(padc(hq_s, 0.0), padc(hf_s, 1.0), padc(hi_s, 0.0), state_hg[0], HG_CHUNK, nb_s)
    ohg_s = ohg_s[:, :ts].reshape(n_s, HG_WIDTH)

    x1_p, h2p_p, idx_p, wt_p, shd_p = _post(
        x_prompt.reshape(n_p, d), oa_p.reshape(n_p, FOX_WIDTH), ohg_p.reshape(n_p, HG_WIDTH), hog_p, sga_p, sgb_p,
        mod_p[2], mod_p[3], mod_p[4], tpg_p, post_w, tm_p)
    x1_s, h2p_s, idx_s, wt_s, shd_s = _post(
        x_sample.reshape(n_s, d), oa_s.reshape(n_s, FOX_WIDTH), ohg_s, hog_s, sga_s, sgb_s,
        mod_s[2], mod_s[3], mod_s[4], 1, post_w, tm_s)

    nh = n // N_HALVES
    idx = jnp.concatenate([idx_p[:, :TOP_K], idx_s[:, :TOP_K]], axis=0)
    wts = jnp.concatenate([wt_p[:, :TOP_K], wt_s[:, :TOP_K]], axis=0)
    blk_e, blk_h, n_valid, src_blk, blk_e2, blk_h2, rows4, rows8, row_w = _dispatch(idx, wts, n, nh)
    a = _moe_up(blk_e, n_valid, rows4, h2p_p.reshape(n_p * 4, LANES), h2p_s.reshape(n_s * 4, LANES),
                w_exp_gate[0], w_exp_up[0])
    routed = _moe_down(blk_e2, blk_h2, n_valid, src_blk, rows8, row_w, a, w_exp_down[0], nh).reshape(n, d)

    lg, lb2 = ln2_g[0].reshape(1, d), ln2_b[0].reshape(1, d)
    y_p = _final(x1_p, routed, 0, shd_p, mod_p[5], tpg_p, lg, lb2, tm_p)
    y_s = _final(x1_s, routed, n_p // tm_s, shd_s, mod_s[5], 1, lg, lb2, tm_s)

    ldt, sdt = cache_logf.dtype, state_hg.dtype
    hd = (FOX_HEADS, FOX_HEAD_DIM)
    return (y_p.reshape(b, t, d), y_s.reshape(bd, ts, d),
            kf_p.reshape((1, b, t) + hd), vf_p.reshape((1, b, t) + hd),
            lf_p.reshape(1, b, t, FOX_HEADS).astype(ldt), st_p[None].astype(sdt),
            kf_s.reshape((1, bd, ts) + hd), vf_s.reshape((1, bd, ts) + hd),
            lf_s.reshape(1, bd, ts, FOX_HEADS).astype(ldt), st_s[None].astype(sdt))
```

```python
import functools

import jax
import jax.numpy as jnp
import numpy as np
from jax import lax
from jax.experimental import pallas as pl
from jax.experimental.pallas import tpu as pltpu

F32 = jnp.float32
BF16 = jnp.bfloat16
I32 = jnp.int32

D_MODEL = 1024
FOX_HEADS = 8
FOX_HEAD_DIM = 64
FOX_WIDTH = FOX_HEADS * FOX_HEAD_DIM
HG_HEADS = 4
HG_DIM = 128
HG_WIDTH = HG_HEADS * HG_DIM
HG_CHUNK = 32
N_EXPERTS = 256
TOP_K = 8
EXPERT_DIM = 256
ROUTED_SCALE = 2.5
DEPTH = 1
DEEPNORM_ALPHA = (2.0 * DEPTH) ** 0.25
LN_EPS = 1e-5
RMS_EPS = 1e-6
NEG_INF = -1e30
PAGE = 128
LANES = 128
MOE_BLOCK = 128
N_HALVES = 2
RING_SLOTS = 4
PAGE_BUFFERS = 3
TOK_FILLER = (1 << 15) - 1
VMEM_LIMIT = 56 * 1024 * 1024

OFF_Q, OFF_K, OFF_V, OFF_F = 0, 512, 1024, 1536
OFF_HQ, OFF_HF, OFF_HI, OFF_HOG = 1664, 2176, 2688, 3200
OFF_GA, OFF_GB, W_ALL_COLS = 3712, 4736, 5760


def _cparams(sem):
    return pltpu.CompilerParams(dimension_semantics=sem, vmem_limit_bytes=VMEM_LIMIT)


def _ln(x):
    mu = jnp.mean(x, axis=-1, keepdims=True)
    xc = x - mu
    var = jnp.mean(xc * xc, axis=-1, keepdims=True)
    return xc * lax.rsqrt(var + LN_EPS)


def _silu(x):
    return x * jax.nn.sigmoid(x)


def _split3(x):
    hi = x.astype(BF16)
    r1 = x - hi.astype(F32)
    mid = r1.astype(BF16)
    lo = (r1 - mid.astype(F32)).astype(BF16)
    return hi, mid, lo


def _dot3(x, m):
    hi, mid, lo = _split3(x)
    d = lambda a: jnp.dot(a, m, preferred_element_type=F32)
    return d(hi) + d(mid) + d(lo)


def _dot_nt(a, b):
    return lax.dot_general(a, b, (((1,), (1,)), ((), ())), preferred_element_type=F32)


def _dot_tn(a, b):
    return lax.dot_general(a, b, (((0,), (0,)), ((), ())), preferred_element_type=F32)


def _adaln_kernel(c_ref, w_ref, b_ref, o_ref):
    s = _silu(c_ref[...]).astype(BF16)
    o_ref[...] = jnp.dot(s, w_ref[...].astype(BF16), preferred_element_type=F32) + b_ref[...]


def _adaln(c, w_ada, b_ada):
    r = c.shape[0]
    tn = 1024
    return pl.pallas_call(
        _adaln_kernel,
        grid=(6 * D_MODEL // tn,),
        in_specs=[pl.BlockSpec((r, D_MODEL), lambda j: (0, 0)),
                  pl.BlockSpec((D_MODEL, tn), lambda j: (0, j)),
                  pl.BlockSpec((1, tn), lambda j: (0, j))],
        out_specs=pl.BlockSpec((r, tn), lambda j: (0, j)),
        out_shape=jax.ShapeDtypeStruct((r, 6 * D_MODEL), F32),
        compiler_params=_cparams(("parallel",)),
        name="adaln",
    )(c, w_ada, b_ada.reshape(1, -1))


def _inproj_kernel(x_ref, sh_ref, sc_ref, w_ref, bf_ref, lbl_ref,
                   q_ref, kb_ref, vb_ref, kf_ref, vf_ref, lf_ref,
                   hq_ref, hf_ref, hi_ref, hog_ref, sga_ref, sgb_ref):
    h = _ln(x_ref[...]) * (1.0 + sc_ref[0]) + sh_ref[0]
    hb = h.astype(BF16)

    def proj(a, b):
        return jnp.dot(hb, w_ref[:, a:b], preferred_element_type=F32)

    q_ref[...] = (proj(OFF_Q, OFF_K) * (FOX_HEAD_DIM ** -0.5)).astype(BF16)
    k = proj(OFF_K, OFF_V)
    kf_ref[...] = k
    kb_ref[...] = k.astype(BF16)
    v = proj(OFF_V, OFF_F)
    vf_ref[...] = v
    vb_ref[...] = v.astype(BF16)
    logf = jax.nn.log_sigmoid(proj(OFF_F, OFF_HQ) + bf_ref[...])
    lf_ref[...] = logf[:, :FOX_HEADS]

    hq_ref[...] = _silu(proj(OFF_HQ, OFF_HF))
    l0 = lbl_ref[0:1, :]
    l1 = lbl_ref[1:2, :]
    mx = jnp.maximum(l0, l1)
    e0 = jnp.exp(l0 - mx)
    lb = e0 / (e0 + jnp.exp(l1 - mx))
    hf_ref[...] = lb + (1.0 - lb) * jax.nn.sigmoid(proj(OFF_HF, OFF_HI))
    hi_ref[...] = proj(OFF_HI, OFF_HOG)
    hog_ref[...] = _silu(proj(OFF_HOG, OFF_GA))
    sga_ref[...] = jax.nn.sigmoid(proj(OFF_GA, OFF_GB))
    sgb_ref[...] = jax.nn.sigmoid(proj(OFF_GB, W_ALL_COLS))


def _inproj(x, sh, sc, tiles_per_group, w_all, bf_pad, lbl, tm):
    n = x.shape[0]
    r = sh.shape[1]
    row = lambda w: pl.BlockSpec((tm, w), lambda i: (i, 0))
    mod = pl.BlockSpec((1, r, D_MODEL), lambda i: (i // tiles_per_group, 0, 0))
    const = lambda s: pl.BlockSpec(s, lambda i: (0, 0))
    sds = lambda w, dt: jax.ShapeDtypeStruct((n, w), dt)
    return pl.pallas_call(
        _inproj_kernel,
        grid=(n // tm,),
        in_specs=[row(D_MODEL), mod, mod,
                  pl.BlockSpec((D_MODEL, W_ALL_COLS), lambda i: (0, 0), pipeline_mode=pl.Buffered(1)),
                  const((1, LANES)), const((2, HG_WIDTH))],
        out_specs=[row(FOX_WIDTH)] * 5 + [row(FOX_HEADS)] + [row(HG_WIDTH)] * 4 + [row(D_MODEL)] * 2,
        out_shape=[sds(FOX_WIDTH, BF16)] * 3 + [sds(FOX_WIDTH, F32)] * 2 + [sds(FOX_HEADS, F32)]
        + [sds(HG_WIDTH, F32)] * 4 + [sds(D_MODEL, F32)] * 2,
        compiler_params=_cparams(("parallel",)),
        name="inproj",
    )(x, sh, sc, w_all, bf_pad, lbl)


def _upper_ones():
    r = np.arange(PAGE)
    return jnp.asarray(r[:, None] <= r[None, :], BF16)


def _cum_rows_kernel(x_ref, u_ref, o_ref):
    o_ref[...] = _dot3(x_ref[...], u_ref[...])


def _cum_rows(x, tb):
    n = x.shape[0]
    return pl.pallas_call(
        _cum_rows_kernel,
        grid=(n // tb,),
        in_specs=[pl.BlockSpec((tb, PAGE), lambda i: (i, 0)), pl.BlockSpec((PAGE, PAGE), lambda i: (0, 0))],
        out_specs=pl.BlockSpec((tb, PAGE), lambda i: (i, 0)),
        out_shape=jax.ShapeDtypeStruct((n, PAGE), F32),
        compiler_params=_cparams(("parallel",)),
        name="cum_rows",
    )(x, _upper_ones())


def _cum_seq_kernel(x_ref, u_ref, g_ref, o_ref):
    x = x_ref[...]
    within = _dot3(x, u_ref[...])
    tot = _dot3(x, jnp.ones((PAGE, PAGE), BF16))
    hi, mid, lo = _split3(tot)
    g = g_ref[...]
    d = lambda a: jnp.dot(g, a, preferred_element_type=F32)
    o_ref[...] = within + d(hi) + d(mid) + d(lo)


def _cum_seq(x, rows_per_seq):
    nr = x.shape[0]
    r = np.arange(nr)
    g = (r[:, None] // rows_per_seq == r[None, :] // rows_per_seq) & (r[None, :] < r[:, None])
    return pl.pallas_call(
        _cum_seq_kernel,
        out_shape=jax.ShapeDtypeStruct(x.shape, F32),
        compiler_params=pltpu.CompilerParams(vmem_limit_bytes=VMEM_LIMIT),
        name="cum_seq",
    )(x, _upper_ones(), jnp.asarray(g, BF16))


def _fox_prompt_kernel(q_ref, k_ref, v_ref, ck_ref, o_ref, m_sc, l_sc, acc_sc, cq_sc, s_sc, p_sc, *, tq, strip):
    i = pl.program_id(2)
    q = q_ref[0]
    lo_lanes = lax.broadcasted_iota(I32, (tq, LANES), 1) < FOX_HEAD_DIM
    zero = jnp.zeros_like(q)
    qh = (jnp.where(lo_lanes, q, zero), jnp.where(lo_lanes, zero, q))
    q_off = pl.multiple_of(i * tq, tq)
    cq_rows = ck_ref[0, 0, :, pl.ds(q_off, tq)]
    for h in range(2):
        cq_sc[h] = jnp.transpose(jnp.broadcast_to(cq_rows[h:h + 1, :], (LANES, tq)))
    reps = tq // LANES
    row = lax.broadcasted_iota(I32, (strip, tq), 0)
    col = lax.broadcasted_iota(I32, (strip, tq), 1)

    m_sc[...] = jnp.full(m_sc.shape, NEG_INF, F32)
    l_sc[...] = jnp.zeros(l_sc.shape, F32)
    acc_sc[...] = jnp.zeros(acc_sc.shape, F32)

    def scores(j, slot):
        kt = k_ref[0, pl.ds(pl.multiple_of(j * tq, tq), tq), :]
        for h in range(2):
            s_sc[slot, h] = _dot_nt(qh[h], kt)

    scores(0, 0)

    def kv_step(j, slot, diag):
        if not diag:
            scores(j + 1, 1 - slot)
        k_off = pl.multiple_of(j * tq, tq)
        vt = v_ref[0, pl.ds(k_off, tq), :]
        ckt = ck_ref[0, 0, :, pl.ds(k_off, tq)]
        for h in range(2):
            ck_row = ckt[h:h + 1, :]
            for r0 in range(0, tq, strip):
                rs = slice(r0, r0 + strip)
                logits = s_sc[slot, h, rs, :] + jnp.concatenate([cq_sc[h, rs, :]] * reps, axis=1) - ck_row
                if diag:
                    logits = jnp.where(col <= row + r0, logits, NEG_INF)
                m_prev = m_sc[h, rs, :]
                m_new = jnp.maximum(m_prev, jnp.max(logits, axis=1, keepdims=True))
                p = jnp.exp(logits - jnp.concatenate([m_new] * reps, axis=1))
                alpha = jnp.exp(m_prev - m_new)
                l_sc[h, rs, :] = alpha * l_sc[h, rs, :] + jnp.sum(p, axis=1, keepdims=True)
                acc_sc[h, rs, :] = alpha * acc_sc[h, rs, :]
                m_sc[h, rs, :] = m_new
                p_sc[h, rs, :] = p.astype(BF16)
        for h in range(2):
            acc_sc[h] += jnp.dot(p_sc[h], vt, preferred_element_type=F32)

    def body(jj, c):
        kv_step(2 * jj, 0, False)
        kv_step(2 * jj + 1, 1, False)
        return c

    lax.fori_loop(0, i // 2, body, 0)

    @pl.when(i % 2 == 0)
    def _():
        kv_step(i, 0, True)

    @pl.when(i % 2 == 1)
    def _():
        kv_step(i - 1, 0, False)
        kv_step(i, 1, True)

    o = jnp.where(lo_lanes, acc_sc[0] / l_sc[0], acc_sc[1] / l_sc[1])
    o_ref[0] = o.astype(BF16)


def _fox_prompt(q, k, v, ck, tq):
    b, t, _ = q.shape
    pairs = FOX_HEADS // 2
    return pl.pallas_call(
        functools.partial(_fox_prompt_kernel, tq=tq, strip=32),
        grid=(b, pairs, t // tq),
        in_specs=[pl.BlockSpec((1, tq, LANES), lambda bi, p, i: (bi, i, p)),
                  pl.BlockSpec((1, t, LANES), lambda bi, p, i: (bi, 0, p)),
                  pl.BlockSpec((1, t, LANES), lambda bi, p, i: (bi, 0, p)),
                  pl.BlockSpec((1, 1, 2, t), lambda bi, p, i: (bi, p, 0, 0))],
        out_specs=pl.BlockSpec((1, tq, LANES), lambda bi, p, i: (bi, i, p)),
        out_shape=jax.ShapeDtypeStruct((b, t, FOX_WIDTH), BF16),
        scratch_shapes=[pltpu.VMEM((2, tq, LANES), F32)] * 4
        + [pltpu.VMEM((2, 2, tq, tq), F32), pltpu.VMEM((2, tq, tq), BF16)],
        compiler_params=_cparams(("parallel", "parallel", "arbitrary")),
        name="fox_prompt",
    )(q, k, v, ck)


def _fox_sample_kernel(pt_ref, q_ref, kn_ref, vn_ref, cn_ref, k_hbm, v_hbm, c_hbm, o_ref,
                       kbuf, vbuf, cbuf, m_sc, l_sc, acc_sc, off_sc, ksem, vsem, csem, *, pages):
    g = pl.program_id(1)
    groups = pl.num_programs(1)
    n_q = q_ref.shape[1]
    rows = n_q * FOX_HEADS

    step = pl.program_id(0) * groups + g
    n_steps = pl.num_programs(0) * groups
    ahead = PAGE_BUFFERS - 1

    def page_copies(s, slot):
        sb, sg = s // groups, s % groups
        copies = []
        for i in range(pages):
            pg = pt_ref[sb, sg * pages + i]
            copies += [pltpu.make_async_copy(k_hbm.at[pg], kbuf.at[slot, i], ksem.at[slot]),
                       pltpu.make_async_copy(v_hbm.at[pg], vbuf.at[slot, i], vsem.at[slot]),
                       pltpu.make_async_copy(c_hbm.at[pg], cbuf.at[slot, i], csem.at[slot])]
        return copies

    @pl.when(step == 0)
    def _():
        for s in range(ahead):
            for cp in page_copies(s, s):
                cp.start()

    slot = step % PAGE_BUFFERS
    for cp in page_copies(step, slot):
        cp.wait()

    @pl.when(step + ahead < n_steps)
    def _():
        for cp in page_copies(step + ahead, (step + ahead) % PAGE_BUFFERS):
            cp.start()

    @pl.when(g == 0)
    def _():
        m_sc[...] = jnp.full(m_sc.shape, NEG_INF, F32)
        l_sc[...] = jnp.zeros(l_sc.shape, F32)
        acc_sc[...] = jnp.zeros(acc_sc.shape, F32)
        off_sc[...] = jnp.zeros(off_sc.shape, F32)

    q4 = q_ref[0].astype(F32)
    head_of_lane = lax.broadcasted_iota(I32, (FOX_HEADS, FOX_WIDTH), 1) // FOX_HEAD_DIM
    bmask = head_of_lane == lax.broadcasted_iota(I32, (FOX_HEADS, FOX_WIDTH), 0)
    wq = jnp.concatenate(
        [jnp.where(bmask, jnp.broadcast_to(q4[t:t + 1, :], (FOX_HEADS, FOX_WIDTH)), 0.0) for t in range(n_q)], axis=0)

    def update(s_all, pv_fn):
        m_prev = m_sc[...]
        m_new = jnp.maximum(m_prev, jnp.max(s_all, axis=1, keepdims=True))
        p = jnp.exp(s_all - jnp.concatenate([m_new] * (s_all.shape[1] // LANES), axis=1))
        alpha = jnp.exp(m_prev - m_new)
        l_sc[...] = alpha * l_sc[...] + jnp.sum(p, axis=1, keepdims=True)
        acc_sc[...] = jnp.concatenate([alpha] * (FOX_WIDTH // LANES), axis=1) * acc_sc[...] + pv_fn(p)
        m_sc[...] = m_new

    off = off_sc[...]
    s_list = []
    for i in range(pages):
        within = cbuf[slot, i]
        s = jnp.dot(wq, kbuf[slot, i], preferred_element_type=F32)
        s_list.append(s - jnp.concatenate([off + within] * n_q, axis=0))
        off = off + jnp.broadcast_to(within[:, PAGE - 1:PAGE], (FOX_HEADS, PAGE))
    off_sc[...] = off

    def pv_pages(p):
        pv = _dot_nt(p[:, :PAGE], vbuf[slot, 0])
        for i in range(1, pages):
            pv = pv + _dot_nt(p[:, i * PAGE:(i + 1) * PAGE], vbuf[slot, i])
        return pv

    update(jnp.concatenate(s_list, axis=1), pv_pages)

    @pl.when(g == pl.num_programs(1) - 1)
    def _():
        pad = jnp.zeros((PAGE - kn_ref.shape[1], FOX_WIDTH), F32)
        kn = jnp.concatenate([kn_ref[0], pad], axis=0)
        vn = jnp.concatenate([vn_ref[0], pad], axis=0)
        s = _dot_nt(wq, kn) - jnp.concatenate([off + cn_ref[0]] * n_q, axis=0)
        t_of_row = lax.broadcasted_iota(I32, (rows, PAGE), 0) // FOX_HEADS
        key = lax.broadcasted_iota(I32, (rows, PAGE), 1)
        update(jnp.where(key <= t_of_row, s, NEG_INF),
               lambda p: jnp.dot(p, vn, preferred_element_type=F32))
        o32 = acc_sc[...] / jnp.concatenate([l_sc[...]] * (FOX_WIDTH // LANES), axis=1)
        outs = []
        for t in range(n_q):
            blk = o32[t * FOX_HEADS:(t + 1) * FOX_HEADS, :]
            outs.append(jnp.sum(jnp.where(bmask, blk, 0.0), axis=0, keepdims=True))
        o_ref[0] = jnp.concatenate(outs, axis=0).astype(BF16)


def _fox_sample(page_table, q, k_new, v_new, cum_new, cache_k, cache_v, cum_pages, pages):
    bd, n_q, _ = q.shape
    n_pages = page_table.shape[1]
    seq = lambda s: pl.BlockSpec((1,) + s, lambda b, g, pt: (b, 0, 0))
    hbm = pl.BlockSpec(memory_space=pl.ANY)
    rows = n_q * FOX_HEADS
    assert bd * (n_pages // pages) >= PAGE_BUFFERS
    page_sems = pltpu.SemaphoreType.DMA((PAGE_BUFFERS,))
    return pl.pallas_call(
        functools.partial(_fox_sample_kernel, pages=pages),
        grid_spec=pltpu.PrefetchScalarGridSpec(
            num_scalar_prefetch=1,
            grid=(bd, n_pages // pages),
            in_specs=[seq((n_q, FOX_WIDTH)), seq(k_new.shape[1:]), seq(v_new.shape[1:]), seq((FOX_HEADS, PAGE)),
                      hbm, hbm, hbm],
            out_specs=pl.BlockSpec((1, n_q, FOX_WIDTH), lambda b, g, pt: (b, 0, 0)),
            scratch_shapes=[pltpu.VMEM((PAGE_BUFFERS, pages, FOX_WIDTH, PAGE), F32),
                            pltpu.VMEM((PAGE_BUFFERS, pages, FOX_WIDTH, PAGE), F32),
                            pltpu.VMEM((PAGE_BUFFERS, pages, FOX_HEADS, PAGE), F32),
                            pltpu.VMEM((rows, LANES), F32), pltpu.VMEM((rows, LANES), F32),
                            pltpu.VMEM((rows, FOX_WIDTH), F32), pltpu.VMEM((FOX_HEADS, PAGE), F32),
                            page_sems, page_sems, page_sems]),
        out_shape=jax.ShapeDtypeStruct((bd, n_q, FOX_WIDTH), BF16),
        compiler_params=_cparams(("arbitrary", "arbitrary")),
        name="fox_sample",
    )(page_table, q, k_new, v_new, cum_new, cache_k, cache_v, cum_pages)


def _hgrn2_kernel(q_ref, f_ref, i_ref, s0_ref, o_ref, s_ref, st_sc, *, n_chunks, nb):
    c = HG_CHUNK
    tci = pl.program_id(1)
    chains = [(bi, h) for bi in range(nb) for h in range(HG_HEADS)]

    @pl.when(tci == 0)
    def _():
        for n, (bi, h) in enumerate(chains):
            st_sc[n] = jnp.transpose(s0_ref[bi, h])

    tril = lax.broadcasted_iota(I32, (c, c), 1) <= lax.broadcasted_iota(I32, (c, c), 0)
    ltri = tril.astype(BF16)

    def chunk(ci, carry):
        r0 = pl.multiple_of(ci * c, c)
        for n, (bi, h) in enumerate(chains):
            ls = slice(h * HG_DIM, (h + 1) * HG_DIM)
            f = f_ref[bi, pl.ds(r0, c), ls]
            qc = q_ref[bi, pl.ds(r0, c), ls]
            ic = i_ref[bi, pl.ds(r0, c), ls].astype(BF16)
            kc = 1.0 - f
            hi, mid, lo = _split3(jnp.log(f))
            d = lambda a: jnp.dot(ltri, a, preferred_element_type=F32)
            g = d(hi) + d(mid) + d(lo)
            g_last = g[c - 1:c, :]
            q_dec = (qc * jnp.exp(g)).astype(BF16)
            k_inv = (kc * jnp.exp(-g)).astype(BF16)
            k_end = (kc * jnp.exp(g_last - g)).astype(BF16)
            a = jnp.where(tril, _dot_nt(q_dec, k_inv), 0.0)
            st = st_sc[n]
            o = jnp.dot(a.astype(BF16), ic, preferred_element_type=F32) + _dot_nt(q_dec, st.astype(BF16))
            o_ref[bi, pl.ds(r0, c), ls] = o
            st_sc[n] = st * jnp.exp(g_last) + _dot_tn(ic, k_end)
        return carry

    lax.fori_loop(0, n_chunks, chunk, 0)

    @pl.when(tci == pl.num_programs(1) - 1)
    def _():
        for n, (bi, h) in enumerate(chains):
            s_ref[bi, h] = jnp.transpose(st_sc[n])


def _hgrn2(q, f, i, s0, tc, nb):
    b, t, _ = q.shape
    tok = pl.BlockSpec((nb, tc, HG_WIDTH), lambda g, ti: (g, ti, 0))
    st = pl.BlockSpec((nb, HG_HEADS, HG_DIM, HG_DIM), lambda g, ti: (g, 0, 0, 0))
    return pl.pallas_call(
        functools.partial(_hgrn2_kernel, n_chunks=tc // HG_CHUNK, nb=nb),
        grid=(b // nb, t // tc),
        in_specs=[tok, tok, tok, st],
        out_specs=[tok, st],
        out_shape=[jax.ShapeDtypeStruct((b, t, HG_WIDTH), F32),
                   jax.ShapeDtypeStruct((b, HG_HEADS, HG_DIM, HG_DIM), F32)],
        scratch_shapes=[pltpu.VMEM((nb * HG_HEADS, HG_DIM, HG_DIM), F32)],
        compiler_params=_cparams(("parallel", "arbitrary")),
        name="hgrn2",
    )(q, f, i, s0)


def _post_kernel(x_ref, oa_ref, ohg_ref, hog_ref, sga_ref, sgb_ref, g1_ref, sh2_ref, sc2_ref,
                 wa_ref, wb_ref, wo_ref, ng_ref, l1g_ref, l1b_ref, wr_ref, rb_ref,
                 wsg_ref, wsu_ref, wsd_ref,
                 x1_ref, h2p_ref, idx_ref, wt_ref, shd_ref):
    tm = x_ref.shape[0]
    ohg = ohg_ref[...]
    heads = []
    for h in range(HG_HEADS):
        oh = ohg[:, h * HG_DIM:(h + 1) * HG_DIM]
        heads.append(oh * lax.rsqrt(jnp.mean(oh * oh, axis=-1, keepdims=True) + RMS_EPS))
    ob = (jnp.concatenate(heads, axis=1) * ng_ref[...] * hog_ref[...]).astype(BF16)
    mixed = (sga_ref[...] * jnp.dot(oa_ref[...], wa_ref[...], preferred_element_type=F32)
             + sgb_ref[...] * jnp.dot(ob, wb_ref[...], preferred_element_type=F32))
    y = jnp.dot(mixed.astype(BF16), wo_ref[...], preferred_element_type=F32)
    x1 = _ln(DEEPNORM_ALPHA * x_ref[...] + g1_ref[0] * y) * l1g_ref[...] + l1b_ref[...]
    x1_ref[...] = x1
    h2 = _ln(x1) * (1.0 + sc2_ref[0]) + sh2_ref[0]
    h2b = h2.astype(BF16)

    half = D_MODEL // 2
    lo_bits = lax.shift_right_logical(pltpu.bitcast(h2b[:, :half].astype(F32), I32), 16)
    hi_bits = pltpu.bitcast(h2b[:, half:].astype(F32), I32) & jnp.int32(-65536)
    h2p_ref[...] = hi_bits | lo_bits

    scores = jax.nn.sigmoid(jnp.dot(h2b, wr_ref[...], preferred_element_type=F32))
    sel = scores + rb_ref[...]
    lane_e = lax.broadcasted_iota(I32, (tm, N_EXPERTS), 1).astype(F32)
    lane_o = lax.broadcasted_iota(I32, (tm, LANES), 1)
    idx_acc = jnp.zeros((tm, LANES), F32)
    w_acc = jnp.zeros((tm, LANES), F32)
    w_sum = jnp.zeros((tm, 1), F32)
    for k in range(TOP_K):
        mx = jnp.max(sel, axis=1, keepdims=True)
        ik = jnp.min(jnp.where(sel == mx, lane_e, float(N_EXPERTS)), axis=1, keepdims=True)
        hit = lane_e == ik
        wk = jnp.sum(jnp.where(hit, scores, 0.0), axis=1, keepdims=True)
        sel = jnp.where(hit, -jnp.inf, sel)
        idx_acc = jnp.where(lane_o == k, ik, idx_acc)
        w_acc = jnp.where(lane_o == k, wk, w_acc)
        w_sum = w_sum + wk
    idx_ref[...] = idx_acc.astype(I32)
    wt_ref[...] = ROUTED_SCALE * w_acc / w_sum

    sg = jnp.dot(h2b, wsg_ref[...], preferred_element_type=F32)
    su = jnp.dot(h2b, wsu_ref[...], preferred_element_type=F32)
    shd_ref[...] = jnp.dot((_silu(sg) * su).astype(BF16), wsd_ref[...], preferred_element_type=F32)


def _post(x, oa, ohg, hog, sga, sgb, g1, sh2, sc2, tiles_per_group, wts, tm):
    n = x.shape[0]
    r = g1.shape[1]
    row = lambda w: pl.BlockSpec((tm, w), lambda i: (i, 0))
    mod = pl.BlockSpec((1, r, D_MODEL), lambda i: (i // tiles_per_group, 0, 0))
    const = lambda a: pl.BlockSpec(a.shape, lambda i: (0, 0))
    sds = lambda w, dt: jax.ShapeDtypeStruct((n, w), dt)
    return pl.pallas_call(
        _post_kernel,
        grid=(n // tm,),
        in_specs=[row(D_MODEL), row(FOX_WIDTH), row(HG_WIDTH), row(HG_WIDTH), row(D_MODEL), row(D_MODEL),
                  mod, mod, mod] + [const(a) for a in wts],
        out_specs=[row(D_MODEL), row(D_MODEL // 2), row(LANES), row(LANES), row(D_MODEL)],
        out_shape=[sds(D_MODEL, F32), sds(D_MODEL // 2, I32), sds(LANES, I32), sds(LANES, F32),
                   sds(D_MODEL, F32)],
        compiler_params=_cparams(("parallel",)),
        name="post",
    )(x, oa, ohg, hog, sga, sgb, g1, sh2, sc2, *wts)


def _ring_copy(src_hbm, dst, sems, blk, slot):
    return pltpu.make_async_copy(src_hbm.at[blk], dst.at[slot], sems.at[slot])


def _ring_advance(b, nb, srcs, fetch_blk):
    ahead = RING_SLOTS - 1

    @pl.when(b == 0)
    def _():
        for i in range(ahead):
            for src, dst, sems in srcs:
                _ring_copy(src, dst, sems, fetch_blk(i), i).start()

    slot = b % RING_SLOTS
    for src, dst, sems in srcs:
        _ring_copy(src, dst, sems, fetch_blk(b), slot).wait()

    @pl.when(b + ahead < nb)
    def _():
        for src, dst, sems in srcs:
            _ring_copy(src, dst, sems, fetch_blk(b + ahead), (b + ahead) % RING_SLOTS).start()

    return slot


def _moe_up_kernel(be_ref, nv_ref, rows_hbm, hp_ref, hs_ref, wg_ref, wu_ref, a_ref,
                   hv_sc, xt_sc, rows_sm, sem, rsem, *, np4):
    b = pl.program_id(0)

    @pl.when(b == 0)
    def _():
        cp = pltpu.make_async_copy(hp_ref, hv_sc.at[pl.ds(0, np4)], sem.at[0])
        cs = pltpu.make_async_copy(hs_ref, hv_sc.at[pl.ds(np4, hs_ref.shape[0])], sem.at[1])
        cp.start()
        cs.start()
        cp.wait()
        cs.wait()

    slot = _ring_advance(b, pl.num_programs(0), [(rows_hbm, rows_sm, rsem)], lambda i: i)

    @pl.when(b < nv_ref[0])
    def _():
        for m in range(MOE_BLOCK):
            t4 = pl.multiple_of(rows_sm[slot, 0, m], 4)
            xt_sc[4 * m:4 * m + 4, :] = hv_sc[pl.ds(t4, 4), :]
        lo, hi = [], []
        for j in range(4):
            w = xt_sc[pl.ds(j, MOE_BLOCK, stride=4), :]
            lo.append(pltpu.bitcast(w << 16, F32))
            hi.append(pltpu.bitcast(w & jnp.int32(-65536), F32))
        x = jnp.concatenate(lo + hi, axis=1)
        gate = jnp.dot(x, wg_ref[0], preferred_element_type=F32)
        up = jnp.dot(x, wu_ref[0], preferred_element_type=F32)
        a_ref[...] = (_silu(gate) * up).astype(BF16)

    @pl.when(b >= nv_ref[0])
    def _():
        a_ref[...] = jnp.zeros(a_ref.shape, BF16)


def _moe_up(blk_e, n_valid, rows4, h2p_p, h2p_s, w_gate, w_up):
    nb = blk_e.shape[0]
    assert nb >= RING_SLOTS
    np4, ns4 = h2p_p.shape[0], h2p_s.shape[0]
    wspec = pl.BlockSpec((1, D_MODEL, EXPERT_DIM), lambda b, be, nv: (be[b], 0, 0))
    hbm = pl.BlockSpec(memory_space=pl.ANY)
    return pl.pallas_call(
        functools.partial(_moe_up_kernel, np4=np4),
        grid_spec=pltpu.PrefetchScalarGridSpec(
            num_scalar_prefetch=2,
            grid=(nb,),
            in_specs=[hbm, hbm, hbm, wspec, wspec],
            out_specs=pl.BlockSpec((MOE_BLOCK, EXPERT_DIM), lambda b, be, nv: (b, 0)),
            scratch_shapes=[pltpu.VMEM((np4 + ns4, LANES), I32), pltpu.VMEM((4 * MOE_BLOCK, LANES), I32),
                            pltpu.SMEM((RING_SLOTS, 1, MOE_BLOCK), I32),
                            pltpu.SemaphoreType.DMA((2,)), pltpu.SemaphoreType.DMA((RING_SLOTS,))]),
        out_shape=jax.ShapeDtypeStruct((nb * MOE_BLOCK, EXPERT_DIM), BF16),
        compiler_params=_cparams(("arbitrary",)),
        name="moe_up",
    )(blk_e, n_valid, rows4, h2p_p, h2p_s, w_gate, w_up)


def _moe_down_kernel(be_ref, bh_ref, nv_ref, sb_ref, rows_hbm, rw_hbm, a_hbm, wd_ref, out_ref,
                     acc_sc, y_sc, rows_sm, rw_sm, a_sc, sem, rsem, wsem, asem, *, nh8, unroll):
    b = pl.program_id(0)
    nb = pl.num_programs(0)
    prev = jnp.maximum(b - 1, 0)
    nxt = jnp.minimum(b + 1, nb - 1)

    @pl.when((b == 0) | (bh_ref[b] != bh_ref[prev]))
    def _():
        acc_sc[...] = jnp.zeros(acc_sc.shape, F32)

    slot = _ring_advance(b, nb, [(rows_hbm, rows_sm, rsem), (rw_hbm, rw_sm, wsem), (a_hbm, a_sc, asem)],
                         lambda i: sb_ref[i])

    @pl.when(b < nv_ref[0])
    def _():
        y = jnp.dot(a_sc[slot].astype(F32), wd_ref[0], preferred_element_type=F32)
        tiles = D_MODEL // LANES
        for j in range(tiles):
            y_sc[pl.ds(j, MOE_BLOCK, stride=tiles), :] = y[:, j * LANES:(j + 1) * LANES]
        for m0 in range(0, MOE_BLOCK, unroll):
            new = []
            for m in range(m0, m0 + unroll):
                r8 = pl.multiple_of(rows_sm[slot, 0, m], 8)
                new.append((r8, acc_sc[pl.ds(r8, 8), :] + rw_sm[slot, 0, m] * y_sc[8 * m:8 * m + 8, :]))
            for r8, val in new:
                acc_sc[pl.ds(r8, 8), :] = val

    @pl.when((b == nb - 1) | (bh_ref[nxt] != bh_ref[b]))
    def _():
        cp = pltpu.make_async_copy(acc_sc.at[pl.ds(0, nh8)], out_ref.at[bh_ref[b]], sem.at[0])
        cp.start()
        cp.wait()


def _moe_down(blk_e, blk_h, n_valid, src_blk, rows8, row_w, a, w_down, nh):
    nb = blk_e.shape[0]
    assert nb >= RING_SLOTS
    nh8 = nh * 8
    hbm = pl.BlockSpec(memory_space=pl.ANY)
    ring_sems = pltpu.SemaphoreType.DMA((RING_SLOTS,))
    return pl.pallas_call(
        functools.partial(_moe_down_kernel, nh8=nh8, unroll=8),
        grid_spec=pltpu.PrefetchScalarGridSpec(
            num_scalar_prefetch=4,
            grid=(nb,),
            in_specs=[hbm, hbm, hbm,
                      pl.BlockSpec((1, EXPERT_DIM, D_MODEL), lambda b, be, bh, nv, sb: (be[b], 0, 0))],
            out_specs=pl.BlockSpec(memory_space=pl.ANY),
            scratch_shapes=[pltpu.VMEM((nh8 + 8, LANES), F32), pltpu.VMEM((8 * MOE_BLOCK, LANES), F32),
                            pltpu.SMEM((RING_SLOTS, 1, MOE_BLOCK), I32), pltpu.SMEM((RING_SLOTS, 1, MOE_BLOCK), F32),
                            pltpu.VMEM((RING_SLOTS, MOE_BLOCK, EXPERT_DIM), BF16),
                            pltpu.SemaphoreType.DMA((1,)), ring_sems, ring_sems, ring_sems]),
        out_shape=jax.ShapeDtypeStruct((N_HALVES, nh8, LANES), F32),
        compiler_params=_cparams(("arbitrary",)),
        name="moe_down",
    )(blk_e, blk_h, n_valid, src_blk, rows8, row_w, a.reshape(nb, MOE_BLOCK, EXPERT_DIM), w_down)


def _dispatch(idx, wts, n, nh):
    n_pairs = n * TOP_K
    n_groups = N_HALVES * N_EXPERTS
    fill = MOE_BLOCK - 1
    n_rows = -(-(n_pairs + n_groups * fill) // MOE_BLOCK) * MOE_BLOCK
    nb = n_rows // MOE_BLOCK
    tok = jnp.arange(n_pairs, dtype=I32) // TOP_K
    grp = idx.reshape(n_pairs) * N_HALVES + tok // nh
    gid = jnp.arange(n_groups, dtype=I32)
    counts = jnp.sum((grp[:, None] == gid[None, :]).astype(I32), axis=0)
    need = (-counts) % MOE_BLOCK
    fill_key = jnp.where(jnp.arange(fill, dtype=I32)[None, :] < need[:, None], gid[:, None], n_groups)
    n_tail = n_rows - n_pairs - n_groups * fill
    keys = jnp.concatenate([grp, fill_key.reshape(-1), jnp.full((n_tail,), n_groups, I32)])
    n_fill = n_rows - n_pairs
    assert n < TOK_FILLER
    toks = jnp.concatenate([tok, jnp.full((n_fill,), TOK_FILLER, I32)])
    ws = jnp.concatenate([wts.reshape(n_pairs), jnp.zeros((n_fill,), F32)])
    s_word, s_w = lax.sort((keys * (TOK_FILLER + 1) + toks, ws), num_keys=1)
    s_key, s_tok = s_word // (TOK_FILLER + 1), s_word % (TOK_FILLER + 1)
    real = s_tok != TOK_FILLER
    rows4 = jnp.where(real, s_tok * 4, 0)
    rows8 = jnp.where(real, (s_tok - (s_key % N_HALVES) * nh) * 8, nh * 8)
    blk_key = s_key[::MOE_BLOCK]
    valid = blk_key < n_groups
    n_valid = jnp.sum(valid.astype(I32))
    blk_e = jnp.where(valid, blk_key // N_HALVES, 0)
    blk_h = jnp.where(valid, blk_key % N_HALVES, N_HALVES - 1)
    blk_id = jnp.arange(nb, dtype=I32)
    _, src_blk = lax.sort((jnp.where(valid, blk_h * N_EXPERTS + blk_e, n_groups), blk_id), num_keys=1)
    shape3 = (nb, 1, MOE_BLOCK)
    return (blk_e, blk_h, n_valid.reshape(1), src_blk, blk_e[src_blk], blk_h[src_blk],
            rows4.reshape(shape3), rows8.reshape(shape3), s_w.reshape(shape3))


def _final_kernel(x1_ref, r_ref, s_ref, g2_ref, lg_ref, lb_ref, o_ref):
    y = r_ref[...] + s_ref[...]
    o_ref[...] = _ln(DEEPNORM_ALPHA * x1_ref[...] + g2_ref[0] * y) * lg_ref[...] + lb_ref[...]


def _final(x1, routed, row_off_tiles, shared, g2, tiles_per_group, ln_g, ln_b, tm):
    n = x1.shape[0]
    r = g2.shape[1]
    row = pl.BlockSpec((tm, D_MODEL), lambda i: (i, 0))
    return pl.pallas_call(
        _final_kernel,
        grid=(n // tm,),
        in_specs=[row, pl.BlockSpec((tm, D_MODEL), lambda i: (i + row_off_tiles, 0)), row,
                  pl.BlockSpec((1, r, D_MODEL), lambda i: (i // tiles_per_group, 0, 0)),
                  pl.BlockSpec((1, D_MODEL), lambda i: (0, 0)), pl.BlockSpec((1, D_MODEL), lambda i: (0, 0))],
        out_specs=row,
        out_shape=jax.ShapeDtypeStruct((n, D_MODEL), F32),
        compiler_params=_cparams(("parallel",)),
        name="final",
    )(x1, routed, shared, g2, ln_g, ln_b)


def kernel(x_prompt, x_sample, c_prompt, c_sample, cache_k, cache_v, cache_logf, state_hg, page_table, w_ada, b_ada, w_in, b_fox_f, hg_lb_logits, hg_norm_g, w_branch_a, w_branch_b, w_out, ln1_g, ln1_b, w_router, router_bias, w_exp_gate, w_exp_up, w_exp_down, w_sh_gate, w_sh_up, w_sh_down, ln2_g, ln2_b):
    assert w_ada.shape[0] == DEPTH and hg_lb_logits.shape[0] == DEPTH + 1
    b, t, d = x_prompt.shape
    bd, ts, _ = x_sample.shape
    n_p, n_s = b * t, bd * ts
    n = n_p + n_s
    n_phys = cache_k.shape[1]
    n_pages = page_table.shape[1]
    tm_p = 256
    tm_s = min(256, n_s)
    tq = min(256, t)
    assert n_p % tm_p == 0 and n_s % tm_s == 0 and t % tq == 0 and t % PAGE == 0 and n % N_HALVES == 0
    assert HG_CHUNK % ts == 0

    pad_f = jnp.zeros((d, LANES - FOX_HEADS), F32)
    w_all = jnp.concatenate([w_in[0][:, :OFF_F + FOX_HEADS], pad_f, w_in[0][:, OFF_F + FOX_HEADS:]], axis=1).astype(BF16)
    bf_pad = jnp.concatenate([b_fox_f[0], jnp.zeros((LANES - FOX_HEADS,), F32)]).reshape(1, LANES)
    post_w = (w_branch_a[0].astype(BF16), w_branch_b[0].astype(BF16), w_out[0].astype(BF16),
              jnp.tile(hg_norm_g[0], HG_HEADS).reshape(1, HG_WIDTH), ln1_g[0].reshape(1, d), ln1_b[0].reshape(1, d),
              w_router[0].astype(BF16), router_bias[0].reshape(1, N_EXPERTS),
              w_sh_gate[0].astype(BF16), w_sh_up[0].astype(BF16), w_sh_down[0].astype(BF16))
    r_all = b + bd
    r_pad = -(-r_all // 8) * 8
    c_all = jnp.concatenate([c_prompt, c_sample, jnp.zeros((r_pad - r_all, d), F32)], axis=0)
    mod = _adaln(c_all, w_ada[0], b_ada[0])
    mod_p = [mod[:b, i * d:(i + 1) * d].reshape(b, 1, d) for i in range(6)]
    reps = tm_s // ts
    mod_s = [jnp.repeat(mod[b:b + bd, i * d:(i + 1) * d], ts, axis=0).reshape(n_s // tm_s, tm_s, d) for i in range(6)]
    tpg_p = t // tm_p

    (q_p, kb_p, vb_p, kf_p, vf_p, lf_p, hq_p, hf_p, hi_p, hog_p, sga_p, sgb_p) = _inproj(
        x_prompt.reshape(n_p, d), mod_p[0], mod_p[1], tpg_p, w_all, bf_pad, hg_lb_logits, tm_p)
    (q_s, kb_s, vb_s, kf_s, vf_s, lf_s, hq_s, hf_s, hi_s, hog_s, sga_s, sgb_s) = _inproj(
        x_sample.reshape(n_s, d), mod_s[0], mod_s[1], 1, w_all, bf_pad, hg_lb_logits, tm_s)

    lf_t = lf_p.reshape(b, t, FOX_HEADS).transpose(0, 2, 1)
    ck = _cum_seq(lf_t.reshape(b * FOX_HEADS * (t // PAGE), PAGE), t // PAGE).reshape(b, FOX_HEADS // 2, 2, t)
    oa_p = _fox_prompt(q_p.reshape(b, t, FOX_WIDTH), kb_p.reshape(b, t, FOX_WIDTH),
                       vb_p.reshape(b, t, FOX_WIDTH), ck, tq)

    rows_c = n_phys * FOX_HEADS
    tb = 4096 if rows_c % 4096 == 0 else rows_c
    cum_pages = _cum_rows(cache_logf[0].transpose(0, 2, 1).reshape(rows_c, PAGE), tb).reshape(n_phys, FOX_HEADS, PAGE)
    lf_new = jnp.pad(lf_s.reshape(bd, ts, FOX_HEADS).transpose(0, 2, 1), ((0, 0), (0, 0), (0, PAGE - ts)))
    cum_new = _cum_rows(lf_new.reshape(bd * FOX_HEADS, PAGE), bd * FOX_HEADS).reshape(bd, FOX_HEADS, PAGE)
    pad8 = lambda a: jnp.pad(a.reshape(bd, ts, FOX_WIDTH), ((0, 0), (0, 8 - ts), (0, 0)))
    pages = 16 if n_pages % 16 == 0 else n_pages
    k_t = cache_k[0].transpose(0, 2, 3, 1).reshape(n_phys, FOX_WIDTH, PAGE)
    v_t = cache_v[0].transpose(0, 2, 3, 1).reshape(n_phys, FOX_WIDTH, PAGE)
    oa_s = _fox_sample(page_table, q_s.reshape(bd, ts, FOX_WIDTH), pad8(kf_s), pad8(vf_s), cum_new,
                       k_t, v_t, cum_pages, pages)

    tc = min(512, t)
    nb_p = 2 if b % 2 == 0 else 1
    nb_s = 2 if bd % 2 == 0 else 1
    ohg_p, st_p = _hgrn2(hq_p.reshape(b, t, HG_WIDTH), hf_p.reshape(b, t, HG_WIDTH), hi_p.reshape(b, t, HG_WIDTH),
                         jnp.zeros((b, HG_HEADS, HG_DIM, HG_DIM), F32), tc, nb_p)
    padc = lambda a, v: jnp.pad(a.reshape(bd, ts, HG_WIDTH), ((0, 0), (0, HG_CHUNK - ts), (0, 0)), constant_values=v)
    ohg_s, st_s = _hgrn2(padc(hq_s, 0.0), padc(hf_s, 1.0), padc(hi_s, 0.0), state_hg[0], HG_CHUNK, nb_s)
    ohg_s = ohg_s[:, :ts].reshape(n_s, HG_WIDTH)

    x1_p, h2p_p, idx_p, wt_p, shd_p = _post(
        x_prompt.reshape(n_p, d), oa_p.reshape(n_p, FOX_WIDTH), ohg_p.reshape(n_p, HG_WIDTH), hog_p, sga_p, sgb_p,
        mod_p[2], mod_p[3], mod_p[4], tpg_p, post_w, tm_p)
    x1_s, h2p_s, idx_s, wt_s, shd_s = _post(
        x_sample.reshape(n_s, d), oa_s.reshape(n_s, FOX_WIDTH), ohg_s, hog_s, sga_s, sgb_s,
        mod_s[2], mod_s[3], mod_s[4], 1, post_w, tm_s)

    nh = n // N_HALVES
    idx = jnp.concatenate([idx_p[:, :TOP_K], idx_s[:, :TOP_K]], axis=0)
    wts = jnp.concatenate([wt_p[:, :TOP_K], wt_s[:, :TOP_K]], axis=0)
    blk_e, blk_h, n_valid, src_blk, blk_e2, blk_h2, rows4, rows8, row_w = _dispatch(idx, wts, n, nh)
    a = _moe_up(blk_e, n_valid, rows4, h2p_p.reshape(n_p * 4, LANES), h2p_s.reshape(n_s * 4, LANES),
                w_exp_gate[0], w_exp_up[0])
    routed = _moe_down(blk_e2, blk_h2, n_valid, src_blk, rows8, row_w, a, w_exp_down[0], nh).reshape(n, d)

    lg, lb2 = ln2_g[0].reshape(1, d), ln2_b[0].reshape(1, d)
    y_p = _final(x1_p, routed, 0, shd_p, mod_p[5], tpg_p, lg, lb2, tm_p)
    y_s = _final(x1_s, routed, n_p // tm_s, shd_s, mod_s[5], 1, lg, lb2, tm_s)

    ldt, sdt = cache_logf.dtype, state_hg.dtype
    hd = (FOX_HEADS, FOX_HEAD_DIM)
    return (y_p.reshape(b, t, d), y_s.reshape(bd, ts, d),
            kf_p.reshape((1, b, t) + hd), vf_p.reshape((1, b, t) + hd),
            lf_p.reshape(1, b, t, FOX_HEADS).astype(ldt), st_p[None].astype(sdt),
            kf_s.reshape((1, bd, ts) + hd), vf_s.reshape((1, bd, ts) + hd),
            lf_s.reshape(1, bd, ts, FOX_HEADS).astype(ldt), st_s[None].astype(sdt))
```

```python
import functools

import jax
import jax.numpy as jnp
import numpy as np
from jax import lax
from jax.experimental import pallas as pl
from jax.experimental.pallas import tpu as pltpu

F32 = jnp.float32
BF16 = jnp.bfloat16
I32 = jnp.int32

D_MODEL = 1024
FOX_HEADS = 8
FOX_HEAD_DIM = 64
FOX_WIDTH = FOX_HEADS * FOX_HEAD_DIM
HG_HEADS = 4
HG_DIM = 128
HG_WIDTH = HG_HEADS * HG_DIM
HG_CHUNK = 32
N_EXPERTS = 256
TOP_K = 8
EXPERT_DIM = 256
ROUTED_SCALE = 2.5
DEPTH = 1
DEEPNORM_ALPHA = (2.0 * DEPTH) ** 0.25
LN_EPS = 1e-5
RMS_EPS = 1e-6
NEG_INF = -1e30
PAGE = 128
LANES = 128
MOE_ROWS = 64
MOE_CHUNKS = (3, 2, 4, 1)
SCATTER_BATCH = 4
RING_SLOTS = 4
PAGE_BUFFERS = 3
TOK_FILLER = (1 << 15) - 1
VMEM_LIMIT = 56 * 1024 * 1024

OFF_Q, OFF_K, OFF_V, OFF_F = 0, 512, 1024, 1536
OFF_HQ, OFF_HF, OFF_HI, OFF_HOG = 1664, 2176, 2688, 3200
OFF_GA, OFF_GB, W_ALL_COLS = 3712, 4736, 5760


def _cparams(sem):
    return pltpu.CompilerParams(dimension_semantics=sem, vmem_limit_bytes=VMEM_LIMIT)


def _ln(x):
    mu = jnp.mean(x, axis=-1, keepdims=True)
    xc = x - mu
    var = jnp.mean(xc * xc, axis=-1, keepdims=True)
    return xc * lax.rsqrt(var + LN_EPS)


def _silu(x):
    return x * jax.nn.sigmoid(x)


def _split3(x):
    hi = x.astype(BF16)
    r1 = x - hi.astype(F32)
    mid = r1.astype(BF16)
    lo = (r1 - mid.astype(F32)).astype(BF16)
    return hi, mid, lo


def _dot3(x, m):
    hi, mid, lo = _split3(x)
    d = lambda a: jnp.dot(a, m, preferred_element_type=F32)
    return d(hi) + d(mid) + d(lo)


def _dot_nt(a, b):
    return lax.dot_general(a, b, (((1,), (1,)), ((), ())), preferred_element_type=F32)


def _dot_tn(a, b):
    return lax.dot_general(a, b, (((0,), (0,)), ((), ())), preferred_element_type=F32)


def _adaln_kernel(c_ref, w_ref, b_ref, o_ref):
    s = _silu(c_ref[...]).astype(BF16)
    o_ref[...] = jnp.dot(s, w_ref[...].astype(BF16), preferred_element_type=F32) + b_ref[...]


def _adaln(c, w_ada, b_ada):
    r = c.shape[0]
    tn = 1024
    return pl.pallas_call(
        _adaln_kernel,
        grid=(6 * D_MODEL // tn,),
        in_specs=[pl.BlockSpec((r, D_MODEL), lambda j: (0, 0)),
                  pl.BlockSpec((D_MODEL, tn), lambda j: (0, j)),
                  pl.BlockSpec((1, tn), lambda j: (0, j))],
        out_specs=pl.BlockSpec((r, tn), lambda j: (0, j)),
        out_shape=jax.ShapeDtypeStruct((r, 6 * D_MODEL), F32),
        compiler_params=_cparams(("parallel",)),
        name="adaln",
    )(c, w_ada, b_ada.reshape(1, -1))


def _inproj_kernel(x_ref, sh_ref, sc_ref, w_ref, bf_ref, lbl_ref,
                   q_ref, kb_ref, vb_ref, kf_ref, vf_ref, lf_ref,
                   hq_ref, hf_ref, hi_ref, hog_ref, sga_ref, sgb_ref):
    h = _ln(x_ref[...]) * (1.0 + sc_ref[0]) + sh_ref[0]
    hb = h.astype(BF16)

    def proj(a, b):
        return jnp.dot(hb, w_ref[:, a:b], preferred_element_type=F32)

    q_ref[...] = (proj(OFF_Q, OFF_K) * (FOX_HEAD_DIM ** -0.5)).astype(BF16)
    k = proj(OFF_K, OFF_V)
    kf_ref[...] = k
    kb_ref[...] = k.astype(BF16)
    v = proj(OFF_V, OFF_F)
    vf_ref[...] = v
    vb_ref[...] = v.astype(BF16)
    logf = jax.nn.log_sigmoid(proj(OFF_F, OFF_HQ) + bf_ref[...])
    lf_ref[...] = logf[:, :FOX_HEADS]

    hq_ref[...] = _silu(proj(OFF_HQ, OFF_HF))
    l0 = lbl_ref[0:1, :]
    l1 = lbl_ref[1:2, :]
    mx = jnp.maximum(l0, l1)
    e0 = jnp.exp(l0 - mx)
    lb = e0 / (e0 + jnp.exp(l1 - mx))
    hf_ref[...] = lb + (1.0 - lb) * jax.nn.sigmoid(proj(OFF_HF, OFF_HI))
    hi_ref[...] = proj(OFF_HI, OFF_HOG)
    hog_ref[...] = _silu(proj(OFF_HOG, OFF_GA))
    sga_ref[...] = jax.nn.sigmoid(proj(OFF_GA, OFF_GB))
    sgb_ref[...] = jax.nn.sigmoid(proj(OFF_GB, W_ALL_COLS))


def _inproj(x, sh, sc, tiles_per_group, w_all, bf_pad, lbl, tm):
    n = x.shape[0]
    r = sh.shape[1]
    row = lambda w: pl.BlockSpec((tm, w), lambda i: (i, 0))
    mod = pl.BlockSpec((1, r, D_MODEL), lambda i: (i // tiles_per_group, 0, 0))
    const = lambda s: pl.BlockSpec(s, lambda i: (0, 0))
    sds = lambda w, dt: jax.ShapeDtypeStruct((n, w), dt)
    return pl.pallas_call(
        _inproj_kernel,
        grid=(n // tm,),
        in_specs=[row(D_MODEL), mod, mod,
                  pl.BlockSpec((D_MODEL, W_ALL_COLS), lambda i: (0, 0), pipeline_mode=pl.Buffered(1)),
                  const((1, LANES)), const((2, HG_WIDTH))],
        out_specs=[row(FOX_WIDTH)] * 5 + [row(FOX_HEADS)] + [row(HG_WIDTH)] * 4 + [row(D_MODEL)] * 2,
        out_shape=[sds(FOX_WIDTH, BF16)] * 3 + [sds(FOX_WIDTH, F32)] * 2 + [sds(FOX_HEADS, F32)]
        + [sds(HG_WIDTH, F32)] * 4 + [sds(D_MODEL, F32)] * 2,
        compiler_params=_cparams(("parallel",)),
        name="inproj",
    )(x, sh, sc, w_all, bf_pad, lbl)


def _upper_ones():
    r = np.arange(PAGE)
    return jnp.asarray(r[:, None] <= r[None, :], BF16)


def _cum_rows_kernel(x_ref, u_ref, o_ref):
    o_ref[...] = _dot3(x_ref[...], u_ref[...])


def _cum_rows(x, tb):
    n = x.shape[0]
    return pl.pallas_call(
        _cum_rows_kernel,
        grid=(n // tb,),
        in_specs=[pl.BlockSpec((tb, PAGE), lambda i: (i, 0)), pl.BlockSpec((PAGE, PAGE), lambda i: (0, 0))],
        out_specs=pl.BlockSpec((tb, PAGE), lambda i: (i, 0)),
        out_shape=jax.ShapeDtypeStruct((n, PAGE), F32),
        compiler_params=_cparams(("parallel",)),
        name="cum_rows",
    )(x, _upper_ones())


def _cum_seq_kernel(x_ref, u_ref, g_ref, o_ref):
    x = x_ref[...]
    within = _dot3(x, u_ref[...])
    tot = _dot3(x, jnp.ones((PAGE, PAGE), BF16))
    hi, mid, lo = _split3(tot)
    g = g_ref[...]
    d = lambda a: jnp.dot(g, a, preferred_element_type=F32)
    o_ref[...] = within + d(hi) + d(mid) + d(lo)


def _cum_seq(x, rows_per_seq):
    nr = x.shape[0]
    r = np.arange(nr)
    g = (r[:, None] // rows_per_seq == r[None, :] // rows_per_seq) & (r[None, :] < r[:, None])
    return pl.pallas_call(
        _cum_seq_kernel,
        out_shape=jax.ShapeDtypeStruct(x.shape, F32),
        compiler_params=pltpu.CompilerParams(vmem_limit_bytes=VMEM_LIMIT),
        name="cum_seq",
    )(x, _upper_ones(), jnp.asarray(g, BF16))


def _fox_prompt_kernel(q_ref, k_ref, v_ref, ck_ref, o_ref, m_sc, l_sc, acc_sc, cq_sc, s_sc, p_sc, *, tq, strip):
    i = pl.program_id(2)
    q = q_ref[0]
    lo_lanes = lax.broadcasted_iota(I32, (tq, LANES), 1) < FOX_HEAD_DIM
    zero = jnp.zeros_like(q)
    qh = (jnp.where(lo_lanes, q, zero), jnp.where(lo_lanes, zero, q))
    q_off = pl.multiple_of(i * tq, tq)
    cq_rows = ck_ref[0, 0, :, pl.ds(q_off, tq)]
    for h in range(2):
        cq_sc[h] = jnp.transpose(jnp.broadcast_to(cq_rows[h:h + 1, :], (LANES, tq)))
    reps = tq // LANES
    row = lax.broadcasted_iota(I32, (strip, tq), 0)
    col = lax.broadcasted_iota(I32, (strip, tq), 1)

    m_sc[...] = jnp.full(m_sc.shape, NEG_INF, F32)
    l_sc[...] = jnp.zeros(l_sc.shape, F32)
    acc_sc[...] = jnp.zeros(acc_sc.shape, F32)

    def scores(j, slot):
        kt = k_ref[0, pl.ds(pl.multiple_of(j * tq, tq), tq), :]
        for h in range(2):
            s_sc[slot, h] = _dot_nt(qh[h], kt)

    scores(0, 0)

    def kv_step(j, slot, diag):
        if not diag:
            scores(j + 1, 1 - slot)
        k_off = pl.multiple_of(j * tq, tq)
        vt = v_ref[0, pl.ds(k_off, tq), :]
        ckt = ck_ref[0, 0, :, pl.ds(k_off, tq)]
        for h in range(2):
            ck_row = ckt[h:h + 1, :]
            for r0 in range(0, tq, strip):
                rs = slice(r0, r0 + strip)
                logits = s_sc[slot, h, rs, :] + jnp.concatenate([cq_sc[h, rs, :]] * reps, axis=1) - ck_row
                if diag:
                    logits = jnp.where(col <= row + r0, logits, NEG_INF)
                m_prev = m_sc[h, rs, :]
                m_new = jnp.maximum(m_prev, jnp.max(logits, axis=1, keepdims=True))
                p = jnp.exp(logits - jnp.concatenate([m_new] * reps, axis=1))
                alpha = jnp.exp(m_prev - m_new)
                l_sc[h, rs, :] = alpha * l_sc[h, rs, :] + jnp.sum(p, axis=1, keepdims=True)
                acc_sc[h, rs, :] = alpha * acc_sc[h, rs, :]
                m_sc[h, rs, :] = m_new
                p_sc[h, rs, :] = p.astype(BF16)
        for h in range(2):
            acc_sc[h] += jnp.dot(p_sc[h], vt, preferred_element_type=F32)

    def body(jj, c):
        kv_step(2 * jj, 0, False)
        kv_step(2 * jj + 1, 1, False)
        return c

    lax.fori_loop(0, i // 2, body, 0)

    @pl.when(i % 2 == 0)
    def _():
        kv_step(i, 0, True)

    @pl.when(i % 2 == 1)
    def _():
        kv_step(i - 1, 0, False)
        kv_step(i, 1, True)

    o = jnp.where(lo_lanes, acc_sc[0] / l_sc[0], acc_sc[1] / l_sc[1])
    o_ref[0] = o.astype(BF16)


def _fox_prompt(q, k, v, ck, tq):
    b, t, _ = q.shape
    pairs = FOX_HEADS // 2
    return pl.pallas_call(
        functools.partial(_fox_prompt_kernel, tq=tq, strip=32),
        grid=(b, pairs, t // tq),
        in_specs=[pl.BlockSpec((1, tq, LANES), lambda bi, p, i: (bi, i, p)),
                  pl.BlockSpec((1, t, LANES), lambda bi, p, i: (bi, 0, p)),
                  pl.BlockSpec((1, t, LANES), lambda bi, p, i: (bi, 0, p)),
                  pl.BlockSpec((1, 1, 2, t), lambda bi, p, i: (bi, p, 0, 0))],
        out_specs=pl.BlockSpec((1, tq, LANES), lambda bi, p, i: (bi, i, p)),
        out_shape=jax.ShapeDtypeStruct((b, t, FOX_WIDTH), BF16),
        scratch_shapes=[pltpu.VMEM((2, tq, LANES), F32)] * 4
        + [pltpu.VMEM((2, 2, tq, tq), F32), pltpu.VMEM((2, tq, tq), BF16)],
        compiler_params=_cparams(("parallel", "parallel", "arbitrary")),
        name="fox_prompt",
    )(q, k, v, ck)


def _fox_sample_kernel(pt_ref, q_ref, kn_ref, vn_ref, cn_ref, k_hbm, v_hbm, c_hbm, o_ref,
                       kbuf, vbuf, cbuf, m_sc, l_sc, acc_sc, off_sc, ksem, vsem, csem, *, pages):
    g = pl.program_id(1)
    groups = pl.num_programs(1)
    n_q = q_ref.shape[1]
    rows = n_q * FOX_HEADS

    step = pl.program_id(0) * groups + g
    n_steps = pl.num_programs(0) * groups
    ahead = PAGE_BUFFERS - 1

    def page_copies(s, slot):
        sb, sg = s // groups, s % groups
        copies = []
        for i in range(pages):
            pg = pt_ref[sb, sg * pages + i]
            copies += [pltpu.make_async_copy(k_hbm.at[pg], kbuf.at[slot, i], ksem.at[slot]),
                       pltpu.make_async_copy(v_hbm.at[pg], vbuf.at[slot, i], vsem.at[slot]),
                       pltpu.make_async_copy(c_hbm.at[pg], cbuf.at[slot, i], csem.at[slot])]
        return copies

    @pl.when(step == 0)
    def _():
        for s in range(ahead):
            for cp in page_copies(s, s):
                cp.start()

    slot = step % PAGE_BUFFERS
    for cp in page_copies(step, slot):
        cp.wait()

    @pl.when(step + ahead < n_steps)
    def _():
        for cp in page_copies(step + ahead, (step + ahead) % PAGE_BUFFERS):
            cp.start()

    @pl.when(g == 0)
    def _():
        m_sc[...] = jnp.full(m_sc.shape, NEG_INF, F32)
        l_sc[...] = jnp.zeros(l_sc.shape, F32)
        acc_sc[...] = jnp.zeros(acc_sc.shape, F32)
        off_sc[...] = jnp.zeros(off_sc.shape, F32)

    q4 = q_ref[0].astype(F32)
    head_of_lane = lax.broadcasted_iota(I32, (FOX_HEADS, FOX_WIDTH), 1) // FOX_HEAD_DIM
    bmask = head_of_lane == lax.broadcasted_iota(I32, (FOX_HEADS, FOX_WIDTH), 0)
    wq = jnp.concatenate(
        [jnp.where(bmask, jnp.broadcast_to(q4[t:t + 1, :], (FOX_HEADS, FOX_WIDTH)), 0.0) for t in range(n_q)], axis=0)

    def update(s_all, pv_fn):
        m_prev = m_sc[...]
        m_new = jnp.maximum(m_prev, jnp.max(s_all, axis=1, keepdims=True))
        p = jnp.exp(s_all - jnp.concatenate([m_new] * (s_all.shape[1] // LANES), axis=1))
        alpha = jnp.exp(m_prev - m_new)
        l_sc[...] = alpha * l_sc[...] + jnp.sum(p, axis=1, keepdims=True)
        acc_sc[...] = jnp.concatenate([alpha] * (FOX_WIDTH // LANES), axis=1) * acc_sc[...] + pv_fn(p)
        m_sc[...] = m_new

    off = off_sc[...]
    s_list = []
    for i in range(pages):
        within = cbuf[slot, i]
        s = jnp.dot(wq, kbuf[slot, i], preferred_element_type=F32)
        s_list.append(s - jnp.concatenate([off + within] * n_q, axis=0))
        off = off + jnp.broadcast_to(within[:, PAGE - 1:PAGE], (FOX_HEADS, PAGE))
    off_sc[...] = off

    def pv_pages(p):
        pv = _dot_nt(p[:, :PAGE], vbuf[slot, 0])
        for i in range(1, pages):
            pv = pv + _dot_nt(p[:, i * PAGE:(i + 1) * PAGE], vbuf[slot, i])
        return pv

    update(jnp.concatenate(s_list, axis=1), pv_pages)

    @pl.when(g == pl.num_programs(1) - 1)
    def _():
        pad = jnp.zeros((PAGE - kn_ref.shape[1], FOX_WIDTH), F32)
        kn = jnp.concatenate([kn_ref[0], pad], axis=0)
        vn = jnp.concatenate([vn_ref[0], pad], axis=0)
        s = _dot_nt(wq, kn) - jnp.concatenate([off + cn_ref[0]] * n_q, axis=0)
        t_of_row = lax.broadcasted_iota(I32, (rows, PAGE), 0) // FOX_HEADS
        key = lax.broadcasted_iota(I32, (rows, PAGE), 1)
        update(jnp.where(key <= t_of_row, s, NEG_INF),
               lambda p: jnp.dot(p, vn, preferred_element_type=F32))
        o32 = acc_sc[...] / jnp.concatenate([l_sc[...]] * (FOX_WIDTH // LANES), axis=1)
        outs = []
        for t in range(n_q):
            blk = o32[t * FOX_HEADS:(t + 1) * FOX_HEADS, :]
            outs.append(jnp.sum(jnp.where(bmask, blk, 0.0), axis=0, keepdims=True))
        o_ref[0] = jnp.concatenate(outs, axis=0).astype(BF16)


def _fox_sample(page_table, q, k_new, v_new, cum_new, cache_k, cache_v, cum_pages, pages):
    bd, n_q, _ = q.shape
    n_pages = page_table.shape[1]
    seq = lambda s: pl.BlockSpec((1,) + s, lambda b, g, pt: (b, 0, 0))
    hbm = pl.BlockSpec(memory_space=pl.ANY)
    rows = n_q * FOX_HEADS
    assert bd * (n_pages // pages) >= PAGE_BUFFERS
    page_sems = pltpu.SemaphoreType.DMA((PAGE_BUFFERS,))
    return pl.pallas_call(
        functools.partial(_fox_sample_kernel, pages=pages),
        grid_spec=pltpu.PrefetchScalarGridSpec(
            num_scalar_prefetch=1,
            grid=(bd, n_pages // pages),
            in_specs=[seq((n_q, FOX_WIDTH)), seq(k_new.shape[1:]), seq(v_new.shape[1:]), seq((FOX_HEADS, PAGE)),
                      hbm, hbm, hbm],
            out_specs=pl.BlockSpec((1, n_q, FOX_WIDTH), lambda b, g, pt: (b, 0, 0)),
            scratch_shapes=[pltpu.VMEM((PAGE_BUFFERS, pages, FOX_WIDTH, PAGE), F32),
                            pltpu.VMEM((PAGE_BUFFERS, pages, FOX_WIDTH, PAGE), F32),
                            pltpu.VMEM((PAGE_BUFFERS, pages, FOX_HEADS, PAGE), F32),
                            pltpu.VMEM((rows, LANES), F32), pltpu.VMEM((rows, LANES), F32),
                            pltpu.VMEM((rows, FOX_WIDTH), F32), pltpu.VMEM((FOX_HEADS, PAGE), F32),
                            page_sems, page_sems, page_sems]),
        out_shape=jax.ShapeDtypeStruct((bd, n_q, FOX_WIDTH), BF16),
        compiler_params=_cparams(("arbitrary", "arbitrary")),
        name="fox_sample",
    )(page_table, q, k_new, v_new, cum_new, cache_k, cache_v, cum_pages)


def _hgrn2_kernel(q_ref, f_ref, i_ref, s0_ref, o_ref, s_ref, st_sc, *, n_chunks, nb):
    c = HG_CHUNK
    tci = pl.program_id(1)
    chains = [(bi, h) for bi in range(nb) for h in range(HG_HEADS)]

    @pl.when(tci == 0)
    def _():
        for n, (bi, h) in enumerate(chains):
            st_sc[n] = jnp.transpose(s0_ref[bi, h])

    tril = lax.broadcasted_iota(I32, (c, c), 1) <= lax.broadcasted_iota(I32, (c, c), 0)
    ltri = tril.astype(BF16)

    def chunk(ci, carry):
        r0 = pl.multiple_of(ci * c, c)
        for n, (bi, h) in enumerate(chains):
            ls = slice(h * HG_DIM, (h + 1) * HG_DIM)
            f = f_ref[bi, pl.ds(r0, c), ls]
            qc = q_ref[bi, pl.ds(r0, c), ls]
            ic = i_ref[bi, pl.ds(r0, c), ls].astype(BF16)
            kc = 1.0 - f
            hi, mid, lo = _split3(jnp.log(f))
            d = lambda a: jnp.dot(ltri, a, preferred_element_type=F32)
            g = d(hi) + d(mid) + d(lo)
            g_last = g[c - 1:c, :]
            q_dec = (qc * jnp.exp(g)).astype(BF16)
            k_inv = (kc * jnp.exp(-g)).astype(BF16)
            k_end = (kc * jnp.exp(g_last - g)).astype(BF16)
            a = jnp.where(tril, _dot_nt(q_dec, k_inv), 0.0)
            st = st_sc[n]
            o = jnp.dot(a.astype(BF16), ic, preferred_element_type=F32) + _dot_nt(q_dec, st.astype(BF16))
            o_ref[bi, pl.ds(r0, c), ls] = o
            st_sc[n] = st * jnp.exp(g_last) + _dot_tn(ic, k_end)
        return carry

    lax.fori_loop(0, n_chunks, chunk, 0)

    @pl.when(tci == pl.num_programs(1) - 1)
    def _():
        for n, (bi, h) in enumerate(chains):
            s_ref[bi, h] = jnp.transpose(st_sc[n])


def _hgrn2(q, f, i, s0, tc, nb):
    b, t, _ = q.shape
    tok = pl.BlockSpec((nb, tc, HG_WIDTH), lambda g, ti: (g, ti, 0))
    st = pl.BlockSpec((nb, HG_HEADS, HG_DIM, HG_DIM), lambda g, ti: (g, 0, 0, 0))
    return pl.pallas_call(
        functools.partial(_hgrn2_kernel, n_chunks=tc // HG_CHUNK, nb=nb),
        grid=(b // nb, t // tc),
        in_specs=[tok, tok, tok, st],
        out_specs=[tok, st],
        out_shape=[jax.ShapeDtypeStruct((b, t, HG_WIDTH), F32),
                   jax.ShapeDtypeStruct((b, HG_HEADS, HG_DIM, HG_DIM), F32)],
        scratch_shapes=[pltpu.VMEM((nb * HG_HEADS, HG_DIM, HG_DIM), F32)],
        compiler_params=_cparams(("parallel", "arbitrary")),
        name="hgrn2",
    )(q, f, i, s0)


def _post_kernel(x_ref, oa_ref, ohg_ref, hog_ref, sga_ref, sgb_ref, g1_ref, sh2_ref, sc2_ref,
                 wa_ref, wb_ref, wo_ref, ng_ref, l1g_ref, l1b_ref, wr_ref, rb_ref,
                 wsg_ref, wsu_ref, wsd_ref,
                 x1_ref, h2p_ref, idx_ref, wt_ref, shd_ref):
    tm = x_ref.shape[0]
    ohg = ohg_ref[...]
    heads = []
    for h in range(HG_HEADS):
        oh = ohg[:, h * HG_DIM:(h + 1) * HG_DIM]
        heads.append(oh * lax.rsqrt(jnp.mean(oh * oh, axis=-1, keepdims=True) + RMS_EPS))
    ob = (jnp.concatenate(heads, axis=1) * ng_ref[...] * hog_ref[...]).astype(BF16)
    mixed = (sga_ref[...] * jnp.dot(oa_ref[...], wa_ref[...], preferred_element_type=F32)
             + sgb_ref[...] * jnp.dot(ob, wb_ref[...], preferred_element_type=F32))
    y = jnp.dot(mixed.astype(BF16), wo_ref[...], preferred_element_type=F32)
    x1 = _ln(DEEPNORM_ALPHA * x_ref[...] + g1_ref[0] * y) * l1g_ref[...] + l1b_ref[...]
    x1_ref[...] = x1
    h2 = _ln(x1) * (1.0 + sc2_ref[0]) + sh2_ref[0]
    h2b = h2.astype(BF16)

    half = D_MODEL // 2
    lo_bits = lax.shift_right_logical(pltpu.bitcast(h2b[:, :half].astype(F32), I32), 16)
    hi_bits = pltpu.bitcast(h2b[:, half:].astype(F32), I32) & jnp.int32(-65536)
    h2p_ref[...] = hi_bits | lo_bits

    scores = jax.nn.sigmoid(jnp.dot(h2b, wr_ref[...], preferred_element_type=F32))
    sel = scores + rb_ref[...]
    lane_e = lax.broadcasted_iota(I32, (tm, N_EXPERTS), 1).astype(F32)
    lane_o = lax.broadcasted_iota(I32, (tm, LANES), 1)
    idx_acc = jnp.zeros((tm, LANES), F32)
    w_acc = jnp.zeros((tm, LANES), F32)
    w_sum = jnp.zeros((tm, 1), F32)
    for k in range(TOP_K):
        mx = jnp.max(sel, axis=1, keepdims=True)
        ik = jnp.min(jnp.where(sel == mx, lane_e, float(N_EXPERTS)), axis=1, keepdims=True)
        hit = lane_e == ik
        wk = jnp.sum(jnp.where(hit, scores, 0.0), axis=1, keepdims=True)
        sel = jnp.where(hit, -jnp.inf, sel)
        idx_acc = jnp.where(lane_o == k, ik, idx_acc)
        w_acc = jnp.where(lane_o == k, wk, w_acc)
        w_sum = w_sum + wk
    idx_ref[...] = idx_acc.astype(I32)
    wt_ref[...] = ROUTED_SCALE * w_acc / w_sum

    sg = jnp.dot(h2b, wsg_ref[...], preferred_element_type=F32)
    su = jnp.dot(h2b, wsu_ref[...], preferred_element_type=F32)
    shd_ref[...] = jnp.dot((_silu(sg) * su).astype(BF16), wsd_ref[...], preferred_element_type=F32)


def _post(x, oa, ohg, hog, sga, sgb, g1, sh2, sc2, tiles_per_group, wts, tm):
    n = x.shape[0]
    r = g1.shape[1]
    row = lambda w: pl.BlockSpec((tm, w), lambda i: (i, 0))
    mod = pl.BlockSpec((1, r, D_MODEL), lambda i: (i // tiles_per_group, 0, 0))
    const = lambda a: pl.BlockSpec(a.shape, lambda i: (0, 0))
    sds = lambda w, dt: jax.ShapeDtypeStruct((n, w), dt)
    return pl.pallas_call(
        _post_kernel,
        grid=(n // tm,),
        in_specs=[row(D_MODEL), row(FOX_WIDTH), row(HG_WIDTH), row(HG_WIDTH), row(D_MODEL), row(D_MODEL),
                  mod, mod, mod] + [const(a) for a in wts],
        out_specs=[row(D_MODEL), row(D_MODEL // 2), row(LANES), row(LANES), row(D_MODEL)],
        out_shape=[sds(D_MODEL, F32), sds(D_MODEL // 2, I32), sds(LANES, I32), sds(LANES, F32),
                   sds(D_MODEL, F32)],
        compiler_params=_cparams(("parallel",)),
        name="post",
    )(x, oa, ohg, hog, sga, sgb, g1, sh2, sc2, *wts)


def _ring_copy(src_hbm, dst, sems, blk, slot):
    return pltpu.make_async_copy(src_hbm.at[blk], dst.at[slot], sems.at[slot])


def _moe_kernel(off_ref, nv_ref, g4_hbm, s8_hbm, rw_hbm, h2_hbm, wg_hbm, wu_hbm, wd_hbm, out_ref,
                hv_sc, acc_sc, xt_sc, y_sc, wg_sc, wu_sc, wd_sc, g4_sm, s8_sm, rw_sm,
                sem, gsem, ssem, rsem, wsem, *, ct, n_chunks):
    n_valid = nv_ref[0]
    streams = [(g4_hbm, g4_sm, gsem), (s8_hbm, s8_sm, ssem), (rw_hbm, rw_sm, rsem)]

    def weight_copies(e, slot):
        return [pltpu.make_async_copy(wg_hbm.at[e], wg_sc.at[slot], wsem.at[0, slot]),
                pltpu.make_async_copy(wu_hbm.at[e], wu_sc.at[slot], wsem.at[1, slot]),
                pltpu.make_async_copy(wd_hbm.at[e], wd_sc.at[slot], wsem.at[2, slot])]

    for j in range(RING_SLOTS):
        for src, dst, sems in streams:
            _ring_copy(src, dst, sems, j, j).start()
    for cp in weight_copies(0, 0):
        cp.start()

    def arm(j0, n_blk, wslot):
        m_rows = n_blk * MOE_ROWS
        slots = [(j0 + k) % RING_SLOTS for k in range(n_blk)]
        for k in range(n_blk):
            for src, dst, sems in streams:
                _ring_copy(src, dst, sems, j0 + k, slots[k]).wait()
        for r in range(m_rows):
            t4 = pl.multiple_of(g4_sm[slots[r // MOE_ROWS], 0, r % MOE_ROWS], 4)
            xt_sc[4 * r:4 * r + 4, :] = hv_sc[pl.ds(t4, 4), :]
        lo, hi = [], []
        for j in range(4):
            w = xt_sc[pl.ds(j, m_rows, stride=4), :]
            lo.append(pltpu.bitcast(w << 16, F32))
            hi.append(pltpu.bitcast(w & jnp.int32(-65536), F32))
        x = jnp.concatenate(lo + hi, axis=1)
        gate = jnp.dot(x, wg_sc[wslot], preferred_element_type=F32)
        up = jnp.dot(x, wu_sc[wslot], preferred_element_type=F32)
        y = jnp.dot(_silu(gate) * up, wd_sc[wslot], preferred_element_type=F32)
        tiles = D_MODEL // LANES
        for j in range(tiles):
            y_sc[pl.ds(j, m_rows, stride=tiles), :] = y[:, j * LANES:(j + 1) * LANES]
        for r0 in range(0, m_rows, SCATTER_BATCH):
            new = []
            for r in range(r0, r0 + SCATTER_BATCH):
                k, m = r // MOE_ROWS, r % MOE_ROWS
                r8 = pl.multiple_of(s8_sm[slots[k], 0, m], 8)
                new.append((r8, acc_sc[pl.ds(r8, 8), :] + rw_sm[slots[k], 0, m] * y_sc[8 * r:8 * r + 8, :]))
            for r8, val in new:
                acc_sc[pl.ds(r8, 8), :] = val
        for k in range(n_blk):

            @pl.when(j0 + k + RING_SLOTS < n_valid)
            def _():
                for src, dst, sems in streams:
                    _ring_copy(src, dst, sems, j0 + k + RING_SLOTS, slots[k]).start()

    def chunk(c, carry):
        cp = pltpu.make_async_copy(h2_hbm.at[pl.ds(pl.multiple_of(c * (ct * 4), 8), ct * 4)], hv_sc, sem.at[0])
        cp.start()
        acc_sc[...] = jnp.zeros(acc_sc.shape, F32)
        cp.wait()

        def expert(e, carry2):
            wslot = e % 2
            for wcp in weight_copies(e, wslot):
                wcp.wait()
            nxt = e + 1

            @pl.when((nxt < N_EXPERTS) | (c + 1 < n_chunks))
            def _():
                for wcp in weight_copies(nxt % N_EXPERTS, 1 - wslot):
                    wcp.start()

            g = c * N_EXPERTS + e
            b0 = off_ref[g]
            n_blk = off_ref[g + 1] - b0

            def pair(i, carry3):
                arm(b0 + 2 * i, 2, wslot)
                return carry3

            lax.fori_loop(0, n_blk // 2, pair, 0)

            @pl.when(n_blk % 2 == 1)
            def _():
                arm(b0 + n_blk - 1, 1, wslot)

            return carry2

        lax.fori_loop(0, N_EXPERTS, expert, 0)
        out = pltpu.make_async_copy(acc_sc.at[pl.ds(0, ct * 8)], out_ref.at[c], sem.at[0])
        out.start()
        out.wait()
        return carry

    lax.fori_loop(0, n_chunks, chunk, 0)


def _moe(blk_off, n_valid, g4_rows, s8_rows, row_w, h2p, w_gate, w_up, w_down, n_chunks, ct):
    assert N_EXPERTS % 2 == 0
    hbm = pl.BlockSpec(memory_space=pl.ANY)
    ring_sems = pltpu.SemaphoreType.DMA((RING_SLOTS,))
    ring_i32 = pltpu.SMEM((RING_SLOTS, 1, MOE_ROWS), I32)
    m_max = 2 * MOE_ROWS
    return pl.pallas_call(
        functools.partial(_moe_kernel, ct=ct, n_chunks=n_chunks),
        grid_spec=pltpu.PrefetchScalarGridSpec(
            num_scalar_prefetch=2,
            grid=(1,),
            in_specs=[hbm] * 7,
            out_specs=pl.BlockSpec(memory_space=pl.ANY),
            scratch_shapes=[pltpu.VMEM((ct * 4, LANES), I32), pltpu.VMEM(((ct + 1) * 8, LANES), F32),
                            pltpu.VMEM((4 * m_max, LANES), I32), pltpu.VMEM((8 * m_max, LANES), F32),
                            pltpu.VMEM((2, D_MODEL, EXPERT_DIM), F32), pltpu.VMEM((2, D_MODEL, EXPERT_DIM), F32),
                            pltpu.VMEM((2, EXPERT_DIM, D_MODEL), F32),
                            ring_i32, ring_i32, pltpu.SMEM((RING_SLOTS, 1, MOE_ROWS), F32),
                            pltpu.SemaphoreType.DMA((1,)), ring_sems, ring_sems, ring_sems,
                            pltpu.SemaphoreType.DMA((3, 2))]),
        out_shape=jax.ShapeDtypeStruct((n_chunks, ct * 8, LANES), F32),
        compiler_params=_cparams(("arbitrary",)),
        name="moe",
    )(blk_off, n_valid, g4_rows, s8_rows, row_w, h2p, w_gate, w_up, w_down)


def _dispatch(idx, wts, n, n_chunks):
    ct = n // n_chunks
    n_pairs = n * TOP_K
    n_groups = n_chunks * N_EXPERTS
    fill = MOE_ROWS - 1
    n_rows = -(-(n_pairs + n_groups * fill) // MOE_ROWS) * MOE_ROWS
    tok = jnp.arange(n_pairs, dtype=I32) // TOP_K
    grp = (tok // ct) * N_EXPERTS + idx.reshape(n_pairs)
    gid = jnp.arange(n_groups, dtype=I32)
    counts = jnp.sum((grp[:, None] == gid[None, :]).astype(I32), axis=0)
    need = (-counts) % MOE_ROWS
    fill_key = jnp.where(jnp.arange(fill, dtype=I32)[None, :] < need[:, None], gid[:, None], n_groups)
    n_tail = n_rows - n_pairs - n_groups * fill
    keys = jnp.concatenate([grp, fill_key.reshape(-1), jnp.full((n_tail,), n_groups, I32)])
    n_fill = n_rows - n_pairs
    assert ct < TOK_FILLER and n_pairs // MOE_ROWS >= RING_SLOTS
    toks = jnp.concatenate([tok % ct, jnp.full((n_fill,), ct, I32)])
    ws = jnp.concatenate([wts.reshape(n_pairs), jnp.zeros((n_fill,), F32)])
    s_word, s_w = lax.sort((keys * (TOK_FILLER + 1) + toks, ws), num_keys=1)
    s_tok = s_word % (TOK_FILLER + 1)
    blk_cnt = (counts + need) // MOE_ROWS
    blk_off = jnp.concatenate([jnp.zeros((1,), I32), jnp.cumsum(blk_cnt).astype(I32)])
    shape3 = (n_rows // MOE_ROWS, 1, MOE_ROWS)
    gather4 = jnp.where(s_tok == ct, 0, s_tok * 4)
    scatter8 = s_tok * 8
    return blk_off, blk_off[-1:], gather4.reshape(shape3), scatter8.reshape(shape3), s_w.reshape(shape3)


def _final_kernel(x1_ref, r_ref, s_ref, g2_ref, lg_ref, lb_ref, o_ref):
    y = r_ref[...] + s_ref[...]
    o_ref[...] = _ln(DEEPNORM_ALPHA * x1_ref[...] + g2_ref[0] * y) * lg_ref[...] + lb_ref[...]


def _final(x1, routed, row_off_tiles, shared, g2, tiles_per_group, ln_g, ln_b, tm):
    n = x1.shape[0]
    r = g2.shape[1]
    row = pl.BlockSpec((tm, D_MODEL), lambda i: (i, 0))
    return pl.pallas_call(
        _final_kernel,
        grid=(n // tm,),
        in_specs=[row, pl.BlockSpec((tm, D_MODEL), lambda i: (i + row_off_tiles, 0)), row,
                  pl.BlockSpec((1, r, D_MODEL), lambda i: (i // tiles_per_group, 0, 0)),
                  pl.BlockSpec((1, D_MODEL), lambda i: (0, 0)), pl.BlockSpec((1, D_MODEL), lambda i: (0, 0))],
        out_specs=row,
        out_shape=jax.ShapeDtypeStruct((n, D_MODEL), F32),
        compiler_params=_cparams(("parallel",)),
        name="final",
    )(x1, routed, shared, g2, ln_g, ln_b)


def kernel(x_prompt, x_sample, c_prompt, c_sample, cache_k, cache_v, cache_logf, state_hg, page_table, w_ada, b_ada, w_in, b_fox_f, hg_lb_logits, hg_norm_g, w_branch_a, w_branch_b, w_out, ln1_g, ln1_b, w_router, router_bias, w_exp_gate, w_exp_up, w_exp_down, w_sh_gate, w_sh_up, w_sh_down, ln2_g, ln2_b):
    assert w_ada.shape[0] == DEPTH and hg_lb_logits.shape[0] == DEPTH + 1
    b, t, d = x_prompt.shape
    bd, ts, _ = x_sample.shape
    n_p, n_s = b * t, bd * ts
    n = n_p + n_s
    n_phys = cache_k.shape[1]
    n_pages = page_table.shape[1]
    tm_p = 256
    tm_s = min(256, n_s)
    tq = min(256, t)
    assert n_p % tm_p == 0 and n_s % tm_s == 0 and t % tq == 0 and t % PAGE == 0
    assert HG_CHUNK % ts == 0

    pad_f = jnp.zeros((d, LANES - FOX_HEADS), F32)
    w_all = jnp.concatenate([w_in[0][:, :OFF_F + FOX_HEADS], pad_f, w_in[0][:, OFF_F + FOX_HEADS:]], axis=1).astype(BF16)
    bf_pad = jnp.concatenate([b_fox_f[0], jnp.zeros((LANES - FOX_HEADS,), F32)]).reshape(1, LANES)
    post_w = (w_branch_a[0].astype(BF16), w_branch_b[0].astype(BF16), w_out[0].astype(BF16),
              jnp.tile(hg_norm_g[0], HG_HEADS).reshape(1, HG_WIDTH), ln1_g[0].reshape(1, d), ln1_b[0].reshape(1, d),
              w_router[0].astype(BF16), router_bias[0].reshape(1, N_EXPERTS),
              w_sh_gate[0].astype(BF16), w_sh_up[0].astype(BF16), w_sh_down[0].astype(BF16))
    r_all = b + bd
    r_pad = -(-r_all // 8) * 8
    c_all = jnp.concatenate([c_prompt, c_sample, jnp.zeros((r_pad - r_all, d), F32)], axis=0)
    mod = _adaln(c_all, w_ada[0], b_ada[0])
    mod_p = [mod[:b, i * d:(i + 1) * d].reshape(b, 1, d) for i in range(6)]
    reps = tm_s // ts
    mod_s = [jnp.repeat(mod[b:b + bd, i * d:(i + 1) * d], ts, axis=0).reshape(n_s // tm_s, tm_s, d) for i in range(6)]
    tpg_p = t // tm_p

    (q_p, kb_p, vb_p, kf_p, vf_p, lf_p, hq_p, hf_p, hi_p, hog_p, sga_p, sgb_p) = _inproj(
        x_prompt.reshape(n_p, d), mod_p[0], mod_p[1], tpg_p, w_all, bf_pad, hg_lb_logits, tm_p)
    (q_s, kb_s, vb_s, kf_s, vf_s, lf_s, hq_s, hf_s, hi_s, hog_s, sga_s, sgb_s) = _inproj(
        x_sample.reshape(n_s, d), mod_s[0], mod_s[1], 1, w_all, bf_pad, hg_lb_logits, tm_s)

    lf_t = lf_p.reshape(b, t, FOX_HEADS).transpose(0, 2, 1)
    ck = _cum_seq(lf_t.reshape(b * FOX_HEADS * (t // PAGE), PAGE), t // PAGE).reshape(b, FOX_HEADS // 2, 2, t)
    oa_p = _fox_prompt(q_p.reshape(b, t, FOX_WIDTH), kb_p.reshape(b, t, FOX_WIDTH),
                       vb_p.reshape(b, t, FOX_WIDTH), ck, tq)

    rows_c = n_phys * FOX_HEADS
    tb = 4096 if rows_c % 4096 == 0 else rows_c
    cum_pages = _cum_rows(cache_logf[0].transpose(0, 2, 1).reshape(rows_c, PAGE), tb).reshape(n_phys, FOX_HEADS, PAGE)
    lf_new = jnp.pad(lf_s.reshape(bd, ts, FOX_HEADS).transpose(0, 2, 1), ((0, 0), (0, 0), (0, PAGE - ts)))
    cum_new = _cum_rows(lf_new.reshape(bd * FOX_HEADS, PAGE), bd * FOX_HEADS).reshape(bd, FOX_HEADS, PAGE)
    pad8 = lambda a: jnp.pad(a.reshape(bd, ts, FOX_WIDTH), ((0, 0), (0, 8 - ts), (0, 0)))
    pages = 16 if n_pages % 16 == 0 else n_pages
    k_t = cache_k[0].transpose(0, 2, 3, 1).reshape(n_phys, FOX_WIDTH, PAGE)
    v_t = cache_v[0].transpose(0, 2, 3, 1).reshape(n_phys, FOX_WIDTH, PAGE)
    oa_s = _fox_sample(page_table, q_s.reshape(bd, ts, FOX_WIDTH), pad8(kf_s), pad8(vf_s), cum_new,
                       k_t, v_t, cum_pages, pages)

    tc = min(512, t)
    nb_p = 2 if b % 2 == 0 else 1
    nb_s = 2 if bd % 2 == 0 else 1
    ohg_p, st_p = _hgrn2(hq_p.reshape(b, t, HG_WIDTH), hf_p.reshape(b, t, HG_WIDTH), hi_p.reshape(b, t, HG_WIDTH),
                         jnp.zeros((b, HG_HEADS, HG_DIM, HG_DIM), F32), tc, nb_p)
    padc = lambda a, v: jnp.pad(a.reshape(bd, ts, HG_WIDTH), ((0, 0), (0, HG_CHUNK - ts), (0, 0)), constant_values=v)
    ohg_s, st_s = _hgrn2(padc(hq_s, 0.0), padc(hf_s, 1.0), padc(hi_s, 0.0), state_hg[0], HG_CHUNK, nb_s)
    ohg_s = ohg_s[:, :ts].reshape(n_s, HG_WIDTH)

    x1_p, h2p_p, idx_p, wt_p, shd_p = _post(
        x_prompt.reshape(n_p, d), oa_p.reshape(n_p, FOX_WIDTH), ohg_p.reshape(n_p, HG_WIDTH), hog_p, sga_p, sgb_p,
        mod_p[2], mod_p[3], mod_p[4], tpg_p, post_w, tm_p)
    x1_s, h2p_s, idx_s, wt_s, shd_s = _post(
        x_sample.reshape(n_s, d), oa_s.reshape(n_s, FOX_WIDTH), ohg_s, hog_s, sga_s, sgb_s,
        mod_s[2], mod_s[3], mod_s[4], 1, post_w, tm_s)

    n_chunks = next(c for c in MOE_CHUNKS if n % (2 * c) == 0)
    idx = jnp.concatenate([idx_p[:, :TOP_K], idx_s[:, :TOP_K]], axis=0)
    wts = jnp.concatenate([wt_p[:, :TOP_K], wt_s[:, :TOP_K]], axis=0)
    h2p = jnp.concatenate([h2p_p, h2p_s], axis=0).reshape(n * 4, LANES)
    blk_off, n_valid, g4_rows, s8_rows, row_w = _dispatch(idx, wts, n, n_chunks)
    routed = _moe(blk_off, n_valid, g4_rows, s8_rows, row_w, h2p, w_exp_gate[0], w_exp_up[0], w_exp_down[0],
                  n_chunks, n // n_chunks).reshape(n, d)

    lg, lb2 = ln2_g[0].reshape(1, d), ln2_b[0].reshape(1, d)
    y_p = _final(x1_p, routed, 0, shd_p, mod_p[5], tpg_p, lg, lb2, tm_p)
    y_s = _final(x1_s, routed, n_p // tm_s, shd_s, mod_s[5], 1, lg, lb2, tm_s)

    ldt, sdt = cache_logf.dtype, state_hg.dtype
    hd = (FOX_HEADS, FOX_HEAD_DIM)
    return (y_p.reshape(b, t, d), y_s.reshape(bd, ts, d),
            kf_p.reshape((1, b, t) + hd), vf_p.reshape((1, b, t) + hd),
            lf_p.reshape(1, b, t, FOX_HEADS).astype(ldt), st_p[None].astype(sdt),
            kf_s.reshape((1, bd, ts) + hd), vf_s.reshape((1, bd, ts) + hd),
            lf_s.reshape(1, bd, ts, FOX_HEADS).astype(ldt), st_s[None].astype(sdt))
```

```python
import functools

import jax
import jax.numpy as jnp
import numpy as np
from jax import lax
from jax.experimental import pallas as pl
from jax.experimental.pallas import tpu as pltpu

F32 = jnp.float32
BF16 = jnp.bfloat16
I32 = jnp.int32

D_MODEL = 1024
FOX_HEADS = 8
FOX_HEAD_DIM = 64
FOX_WIDTH = FOX_HEADS * FOX_HEAD_DIM
HG_HEADS = 4
HG_DIM = 128
HG_WIDTH = HG_HEADS * HG_DIM
HG_CHUNK = 32
N_EXPERTS = 256
TOP_K = 8
EXPERT_DIM = 256
ROUTED_SCALE = 2.5
DEPTH = 1
DEEPNORM_ALPHA = (2.0 * DEPTH) ** 0.25
LN_EPS = 1e-5
RMS_EPS = 1e-6
NEG_INF = -1e30
PAGE = 128
LANES = 128
MOE_ROWS = 64
MOE_CHUNKS = (3, 2, 4, 1)
SCATTER_BATCH = 4
WEIGHT_SLOTS = 3
RING_SLOTS = 4
PAGE_BUFFERS = 3
TOK_FILLER = (1 << 15) - 1
VMEM_LIMIT = 56 * 1024 * 1024

OFF_Q, OFF_K, OFF_V, OFF_F = 0, 512, 1024, 1536
OFF_HQ, OFF_HF, OFF_HI, OFF_HOG = 1664, 2176, 2688, 3200
OFF_GA, OFF_GB, W_ALL_COLS = 3712, 4736, 5760


def _cparams(sem):
    return pltpu.CompilerParams(dimension_semantics=sem, vmem_limit_bytes=VMEM_LIMIT)


def _ln(x):
    mu = jnp.mean(x, axis=-1, keepdims=True)
    xc = x - mu
    var = jnp.mean(xc * xc, axis=-1, keepdims=True)
    return xc * lax.rsqrt(var + LN_EPS)


def _silu(x):
    return x * jax.nn.sigmoid(x)


def _split3(x):
    hi = x.astype(BF16)
    r1 = x - hi.astype(F32)
    mid = r1.astype(BF16)
    lo = (r1 - mid.astype(F32)).astype(BF16)
    return hi, mid, lo


def _dot3(x, m):
    hi, mid, lo = _split3(x)
    d = lambda a: jnp.dot(a, m, preferred_element_type=F32)
    return d(hi) + d(mid) + d(lo)


def _dot_nt(a, b):
    return lax.dot_general(a, b, (((1,), (1,)), ((), ())), preferred_element_type=F32)


def _dot_tn(a, b):
    return lax.dot_general(a, b, (((0,), (0,)), ((), ())), preferred_element_type=F32)


def _adaln_kernel(c_ref, w_ref, b_ref, o_ref):
    s = _silu(c_ref[...]).astype(BF16)
    o_ref[...] = jnp.dot(s, w_ref[...].astype(BF16), preferred_element_type=F32) + b_ref[...]


def _adaln(c, w_ada, b_ada):
    r = c.shape[0]
    tn = 1024
    return pl.pallas_call(
        _adaln_kernel,
        grid=(6 * D_MODEL // tn,),
        in_specs=[pl.BlockSpec((r, D_MODEL), lambda j: (0, 0)),
                  pl.BlockSpec((D_MODEL, tn), lambda j: (0, j)),
                  pl.BlockSpec((1, tn), lambda j: (0, j))],
        out_specs=pl.BlockSpec((r, tn), lambda j: (0, j)),
        out_shape=jax.ShapeDtypeStruct((r, 6 * D_MODEL), F32),
        compiler_params=_cparams(("parallel",)),
        name="adaln",
    )(c, w_ada, b_ada.reshape(1, -1))


def _inproj_kernel(x_ref, sh_ref, sc_ref, w_ref, bf_ref, lbl_ref,
                   q_ref, kb_ref, vb_ref, kf_ref, vf_ref, lf_ref,
                   hq_ref, hf_ref, hi_ref, hog_ref, sga_ref, sgb_ref, *, kv_transposed):
    h = _ln(x_ref[...]) * (1.0 + sc_ref[0]) + sh_ref[0]
    hb = h.astype(BF16)

    def proj(a, b):
        return jnp.dot(hb, w_ref[:, a:b], preferred_element_type=F32)

    q_ref[...] = (proj(OFF_Q, OFF_K) * (FOX_HEAD_DIM ** -0.5)).astype(BF16)
    k = proj(OFF_K, OFF_V)
    kb_ref[...] = k.astype(BF16)
    v = proj(OFF_V, OFF_F)
    vb_ref[...] = v.astype(BF16)
    if kv_transposed:
        kf_ref[0] = jnp.transpose(k)
        vf_ref[0] = jnp.transpose(v)
    else:
        kf_ref[...] = k
        vf_ref[...] = v
    logf = jax.nn.log_sigmoid(proj(OFF_F, OFF_HQ) + bf_ref[...])
    lf_ref[...] = logf[:, :FOX_HEADS]

    hq_ref[...] = _silu(proj(OFF_HQ, OFF_HF))
    l0 = lbl_ref[0:1, :]
    l1 = lbl_ref[1:2, :]
    mx = jnp.maximum(l0, l1)
    e0 = jnp.exp(l0 - mx)
    lb = e0 / (e0 + jnp.exp(l1 - mx))
    hf_ref[...] = lb + (1.0 - lb) * jax.nn.sigmoid(proj(OFF_HF, OFF_HI))
    hi_ref[...] = proj(OFF_HI, OFF_HOG)
    hog_ref[...] = _silu(proj(OFF_HOG, OFF_GA))
    sga_ref[...] = jax.nn.sigmoid(proj(OFF_GA, OFF_GB))
    sgb_ref[...] = jax.nn.sigmoid(proj(OFF_GB, W_ALL_COLS))


def _inproj(x, sh, sc, tiles_per_group, w_all, bf_pad, lbl, tm, seq_tiles=None):
    n = x.shape[0]
    r = sh.shape[1]
    row = lambda w: pl.BlockSpec((tm, w), lambda i: (i, 0))
    mod = pl.BlockSpec((1, r, D_MODEL), lambda i: (i // tiles_per_group, 0, 0))
    const = lambda s: pl.BlockSpec(s, lambda i: (0, 0))
    sds = lambda w, dt: jax.ShapeDtypeStruct((n, w), dt)
    if seq_tiles is None:
        kv_spec, kv_sds = row(FOX_WIDTH), sds(FOX_WIDTH, F32)
    else:
        kv_spec = pl.BlockSpec((1, FOX_WIDTH, tm), lambda i: (i // seq_tiles, 0, i % seq_tiles))
        kv_sds = jax.ShapeDtypeStruct((n // (tm * seq_tiles), FOX_WIDTH, tm * seq_tiles), F32)
    return pl.pallas_call(
        functools.partial(_inproj_kernel, kv_transposed=seq_tiles is not None),
        grid=(n // tm,),
        in_specs=[row(D_MODEL), mod, mod,
                  pl.BlockSpec((D_MODEL, W_ALL_COLS), lambda i: (0, 0), pipeline_mode=pl.Buffered(1)),
                  const((1, LANES)), const((2, HG_WIDTH))],
        out_specs=[row(FOX_WIDTH)] * 3 + [kv_spec] * 2 + [row(FOX_HEADS)] + [row(HG_WIDTH)] * 4 + [row(D_MODEL)] * 2,
        out_shape=[sds(FOX_WIDTH, BF16)] * 3 + [kv_sds] * 2 + [sds(FOX_HEADS, F32)]
        + [sds(HG_WIDTH, F32)] * 4 + [sds(D_MODEL, F32)] * 2,
        compiler_params=_cparams(("parallel",)),
        name="inproj",
    )(x, sh, sc, w_all, bf_pad, lbl)


def _upper_ones():
    r = np.arange(PAGE)
    return jnp.asarray(r[:, None] <= r[None, :], BF16)


def _cum_rows_kernel(x_ref, u_ref, o_ref):
    o_ref[...] = _dot3(x_ref[...], u_ref[...])


def _cum_rows(x, tb):
    n = x.shape[0]
    return pl.pallas_call(
        _cum_rows_kernel,
        grid=(n // tb,),
        in_specs=[pl.BlockSpec((tb, PAGE), lambda i: (i, 0)), pl.BlockSpec((PAGE, PAGE), lambda i: (0, 0))],
        out_specs=pl.BlockSpec((tb, PAGE), lambda i: (i, 0)),
        out_shape=jax.ShapeDtypeStruct((n, PAGE), F32),
        compiler_params=_cparams(("parallel",)),
        name="cum_rows",
    )(x, _upper_ones())


def _cum_seq_kernel(x_ref, u_ref, g_ref, o_ref):
    x = x_ref[...]
    within = _dot3(x, u_ref[...])
    tot = _dot3(x, jnp.ones((PAGE, PAGE), BF16))
    hi, mid, lo = _split3(tot)
    g = g_ref[...]
    d = lambda a: jnp.dot(g, a, preferred_element_type=F32)
    o_ref[...] = within + d(hi) + d(mid) + d(lo)


def _cum_seq(x, rows_per_seq):
    nr = x.shape[0]
    r = np.arange(nr)
    g = (r[:, None] // rows_per_seq == r[None, :] // rows_per_seq) & (r[None, :] < r[:, None])
    return pl.pallas_call(
        _cum_seq_kernel,
        out_shape=jax.ShapeDtypeStruct(x.shape, F32),
        compiler_params=pltpu.CompilerParams(vmem_limit_bytes=VMEM_LIMIT),
        name="cum_seq",
    )(x, _upper_ones(), jnp.asarray(g, BF16))


def _fox_prompt_kernel(q_ref, k_ref, v_ref, ck_ref, o_ref, m_sc, l_sc, acc_sc, cq_sc, s_sc, p_sc, *, tq, strip):
    i = pl.program_id(2)
    q = q_ref[0]
    lo_lanes = lax.broadcasted_iota(I32, (tq, LANES), 1) < FOX_HEAD_DIM
    zero = jnp.zeros_like(q)
    qh = (jnp.where(lo_lanes, q, zero), jnp.where(lo_lanes, zero, q))
    q_off = pl.multiple_of(i * tq, tq)
    cq_rows = ck_ref[0, 0, :, pl.ds(q_off, tq)]
    for h in range(2):
        cq_sc[h] = jnp.transpose(jnp.broadcast_to(cq_rows[h:h + 1, :], (LANES, tq)))
    reps = tq // LANES
    row = lax.broadcasted_iota(I32, (strip, tq), 0)
    col = lax.broadcasted_iota(I32, (strip, tq), 1)

    m_sc[...] = jnp.full(m_sc.shape, NEG_INF, F32)
    l_sc[...] = jnp.zeros(l_sc.shape, F32)
    acc_sc[...] = jnp.zeros(acc_sc.shape, F32)

    def scores(j, slot):
        kt = k_ref[0, pl.ds(pl.multiple_of(j * tq, tq), tq), :]
        for h in range(2):
            s_sc[slot, h] = _dot_nt(qh[h], kt)

    scores(0, 0)

    def kv_step(j, slot, diag):
        if not diag:
            scores(j + 1, 1 - slot)
        k_off = pl.multiple_of(j * tq, tq)
        vt = v_ref[0, pl.ds(k_off, tq), :]
        ckt = ck_ref[0, 0, :, pl.ds(k_off, tq)]
        for h in range(2):
            ck_row = ckt[h:h + 1, :]
            for r0 in range(0, tq, strip):
                rs = slice(r0, r0 + strip)
                logits = s_sc[slot, h, rs, :] + jnp.concatenate([cq_sc[h, rs, :]] * reps, axis=1) - ck_row
                if diag:
                    logits = jnp.where(col <= row + r0, logits, NEG_INF)
                m_prev = m_sc[h, rs, :]
                m_new = jnp.maximum(m_prev, jnp.max(logits, axis=1, keepdims=True))
                p = jnp.exp(logits - jnp.concatenate([m_new] * reps, axis=1))
                alpha = jnp.exp(m_prev - m_new)
                l_sc[h, rs, :] = alpha * l_sc[h, rs, :] + jnp.sum(p, axis=1, keepdims=True)
                acc_sc[h, rs, :] = alpha * acc_sc[h, rs, :]
                m_sc[h, rs, :] = m_new
                p_sc[h, rs, :] = p.astype(BF16)
        for h in range(2):
            acc_sc[h] += jnp.dot(p_sc[h], vt, preferred_element_type=F32)

    def body(jj, c):
        kv_step(2 * jj, 0, False)
        kv_step(2 * jj + 1, 1, False)
        return c

    lax.fori_loop(0, i // 2, body, 0)

    @pl.when(i % 2 == 0)
    def _():
        kv_step(i, 0, True)

    @pl.when(i % 2 == 1)
    def _():
        kv_step(i - 1, 0, False)
        kv_step(i, 1, True)

    o = jnp.where(lo_lanes, acc_sc[0] / l_sc[0], acc_sc[1] / l_sc[1])
    o_ref[0] = o.astype(BF16)


def _fox_prompt(q, k, v, ck, tq):
    b, t, _ = q.shape
    pairs = FOX_HEADS // 2
    return pl.pallas_call(
        functools.partial(_fox_prompt_kernel, tq=tq, strip=32),
        grid=(b, pairs, t // tq),
        in_specs=[pl.BlockSpec((1, tq, LANES), lambda bi, p, i: (bi, i, p)),
                  pl.BlockSpec((1, t, LANES), lambda bi, p, i: (bi, 0, p)),
                  pl.BlockSpec((1, t, LANES), lambda bi, p, i: (bi, 0, p)),
                  pl.BlockSpec((1, 1, 2, t), lambda bi, p, i: (bi, p, 0, 0))],
        out_specs=pl.BlockSpec((1, tq, LANES), lambda bi, p, i: (bi, i, p)),
        out_shape=jax.ShapeDtypeStruct((b, t, FOX_WIDTH), BF16),
        scratch_shapes=[pltpu.VMEM((2, tq, LANES), F32)] * 4
        + [pltpu.VMEM((2, 2, tq, tq), F32), pltpu.VMEM((2, tq, tq), BF16)],
        compiler_params=_cparams(("parallel", "parallel", "arbitrary")),
        name="fox_prompt",
    )(q, k, v, ck)


def _fox_sample_kernel(pt_ref, q_ref, kn_ref, vn_ref, cn_ref, k_hbm, v_hbm, c_hbm, o_ref,
                       kbuf, vbuf, cbuf, m_sc, l_sc, acc_sc, off_sc, ksem, vsem, csem, *, pages):
    g = pl.program_id(1)
    groups = pl.num_programs(1)
    n_q = q_ref.shape[1]
    rows = n_q * FOX_HEADS

    step = pl.program_id(0) * groups + g
    n_steps = pl.num_programs(0) * groups
    ahead = PAGE_BUFFERS - 1

    def page_copies(s, slot):
        sb, sg = s // groups, s % groups
        copies = []
        for i in range(pages):
            pg = pt_ref[sb, sg * pages + i]
            copies += [pltpu.make_async_copy(k_hbm.at[pg], kbuf.at[slot, i], ksem.at[slot]),
                       pltpu.make_async_copy(v_hbm.at[pg], vbuf.at[slot, i], vsem.at[slot]),
                       pltpu.make_async_copy(c_hbm.at[pg], cbuf.at[slot, i], csem.at[slot])]
        return copies

    @pl.when(step == 0)
    def _():
        for s in range(ahead):
            for cp in page_copies(s, s):
                cp.start()

    slot = step % PAGE_BUFFERS
    for cp in page_copies(step, slot):
        cp.wait()

    @pl.when(step + ahead < n_steps)
    def _():
        for cp in page_copies(step + ahead, (step + ahead) % PAGE_BUFFERS):
            cp.start()

    @pl.when(g == 0)
    def _():
        m_sc[...] = jnp.full(m_sc.shape, NEG_INF, F32)
        l_sc[...] = jnp.zeros(l_sc.shape, F32)
        acc_sc[...] = jnp.zeros(acc_sc.shape, F32)
        off_sc[...] = jnp.zeros(off_sc.shape, F32)

    q4 = q_ref[0].astype(F32)
    head_of_lane = lax.broadcasted_iota(I32, (FOX_HEADS, FOX_WIDTH), 1) // FOX_HEAD_DIM
    bmask = head_of_lane == lax.broadcasted_iota(I32, (FOX_HEADS, FOX_WIDTH), 0)
    wq = jnp.concatenate(
        [jnp.where(bmask, jnp.broadcast_to(q4[t:t + 1, :], (FOX_HEADS, FOX_WIDTH)), 0.0) for t in range(n_q)], axis=0)

    def update(s_all, pv_fn):
        m_prev = m_sc[...]
        m_new = jnp.maximum(m_prev, jnp.max(s_all, axis=1, keepdims=True))
        p = jnp.exp(s_all - jnp.concatenate([m_new] * (s_all.shape[1] // LANES), axis=1))
        alpha = jnp.exp(m_prev - m_new)
        l_sc[...] = alpha * l_sc[...] + jnp.sum(p, axis=1, keepdims=True)
        acc_sc[...] = jnp.concatenate([alpha] * (FOX_WIDTH // LANES), axis=1) * acc_sc[...] + pv_fn(p)
        m_sc[...] = m_new

    off = off_sc[...]
    s_list = []
    for i in range(pages):
        within = cbuf[slot, i]
        s = jnp.dot(wq, kbuf[slot, i], preferred_element_type=F32)
        s_list.append(s - jnp.concatenate([off + within] * n_q, axis=0))
        off = off + jnp.broadcast_to(within[:, PAGE - 1:PAGE], (FOX_HEADS, PAGE))
    off_sc[...] = off

    def pv_pages(p):
        pv = _dot_nt(p[:, :PAGE], vbuf[slot, 0])
        for i in range(1, pages):
            pv = pv + _dot_nt(p[:, i * PAGE:(i + 1) * PAGE], vbuf[slot, i])
        return pv

    update(jnp.concatenate(s_list, axis=1), pv_pages)

    @pl.when(g == pl.num_programs(1) - 1)
    def _():
        pad = jnp.zeros((PAGE - kn_ref.shape[1], FOX_WIDTH), F32)
        kn = jnp.concatenate([kn_ref[0], pad], axis=0)
        vn = jnp.concatenate([vn_ref[0], pad], axis=0)
        s = _dot_nt(wq, kn) - jnp.concatenate([off + cn_ref[0]] * n_q, axis=0)
        t_of_row = lax.broadcasted_iota(I32, (rows, PAGE), 0) // FOX_HEADS
        key = lax.broadcasted_iota(I32, (rows, PAGE), 1)
        update(jnp.where(key <= t_of_row, s, NEG_INF),
               lambda p: jnp.dot(p, vn, preferred_element_type=F32))
        o32 = acc_sc[...] / jnp.concatenate([l_sc[...]] * (FOX_WIDTH // LANES), axis=1)
        outs = []
        for t in range(n_q):
            blk = o32[t * FOX_HEADS:(t + 1) * FOX_HEADS, :]
            outs.append(jnp.sum(jnp.where(bmask, blk, 0.0), axis=0, keepdims=True))
        o_ref[0] = jnp.concatenate(outs, axis=0).astype(BF16)


def _fox_sample(page_table, q, k_new, v_new, cum_new, cache_k, cache_v, cum_pages, pages):
    bd, n_q, _ = q.shape
    n_pages = page_table.shape[1]
    seq = lambda s: pl.BlockSpec((1,) + s, lambda b, g, pt: (b, 0, 0))
    hbm = pl.BlockSpec(memory_space=pl.ANY)
    rows = n_q * FOX_HEADS
    assert bd * (n_pages // pages) >= PAGE_BUFFERS
    page_sems = pltpu.SemaphoreType.DMA((PAGE_BUFFERS,))
    return pl.pallas_call(
        functools.partial(_fox_sample_kernel, pages=pages),
        grid_spec=pltpu.PrefetchScalarGridSpec(
            num_scalar_prefetch=1,
            grid=(bd, n_pages // pages),
            in_specs=[seq((n_q, FOX_WIDTH)), seq(k_new.shape[1:]), seq(v_new.shape[1:]), seq((FOX_HEADS, PAGE)),
                      hbm, hbm, hbm],
            out_specs=pl.BlockSpec((1, n_q, FOX_WIDTH), lambda b, g, pt: (b, 0, 0)),
            scratch_shapes=[pltpu.VMEM((PAGE_BUFFERS, pages, FOX_WIDTH, PAGE), F32),
                            pltpu.VMEM((PAGE_BUFFERS, pages, FOX_WIDTH, PAGE), F32),
                            pltpu.VMEM((PAGE_BUFFERS, pages, FOX_HEADS, PAGE), F32),
                            pltpu.VMEM((rows, LANES), F32), pltpu.VMEM((rows, LANES), F32),
                            pltpu.VMEM((rows, FOX_WIDTH), F32), pltpu.VMEM((FOX_HEADS, PAGE), F32),
                            page_sems, page_sems, page_sems]),
        out_shape=jax.ShapeDtypeStruct((bd, n_q, FOX_WIDTH), BF16),
        compiler_params=_cparams(("arbitrary", "arbitrary")),
        name="fox_sample",
    )(page_table, q, k_new, v_new, cum_new, cache_k, cache_v, cum_pages)


def _hgrn2_kernel(q_ref, f_ref, i_ref, s0_ref, o_ref, s_ref, st_sc, *, n_chunks, nb):
    c = HG_CHUNK
    tci = pl.program_id(1)
    chains = [(bi, h) for bi in range(nb) for h in range(HG_HEADS)]

    @pl.when(tci == 0)
    def _():
        for n, (bi, h) in enumerate(chains):
            st_sc[n] = jnp.transpose(s0_ref[bi, h])

    tril = lax.broadcasted_iota(I32, (c, c), 1) <= lax.broadcasted_iota(I32, (c, c), 0)
    ltri = tril.astype(BF16)

    def chunk(ci, carry):
        r0 = pl.multiple_of(ci * c, c)
        for n, (bi, h) in enumerate(chains):
            ls = slice(h * HG_DIM, (h + 1) * HG_DIM)
            f = f_ref[bi, pl.ds(r0, c), ls]
            qc = q_ref[bi, pl.ds(r0, c), ls]
            ic = i_ref[bi, pl.ds(r0, c), ls].astype(BF16)
            kc = 1.0 - f
            hi, mid, lo = _split3(jnp.log(f))
            d = lambda a: jnp.dot(ltri, a, preferred_element_type=F32)
            g = d(hi) + d(mid) + d(lo)
            g_last = g[c - 1:c, :]
            q_dec = (qc * jnp.exp(g)).astype(BF16)
            k_inv = (kc * jnp.exp(-g)).astype(BF16)
            k_end = (kc * jnp.exp(g_last - g)).astype(BF16)
            a = jnp.where(tril, _dot_nt(q_dec, k_inv), 0.0)
            st = st_sc[n]
            o = jnp.dot(a.astype(BF16), ic, preferred_element_type=F32) + _dot_nt(q_dec, st.astype(BF16))
            o_ref[bi, pl.ds(r0, c), ls] = o
            st_sc[n] = st * jnp.exp(g_last) + _dot_tn(ic, k_end)
        return carry

    lax.fori_loop(0, n_chunks, chunk, 0)

    @pl.when(tci == pl.num_programs(1) - 1)
    def _():
        for n, (bi, h) in enumerate(chains):
            s_ref[bi, h] = jnp.transpose(st_sc[n])


def _hgrn2(q, f, i, s0, tc, nb):
    b, t, _ = q.shape
    tok = pl.BlockSpec((nb, tc, HG_WIDTH), lambda g, ti: (g, ti, 0))
    st = pl.BlockSpec((nb, HG_HEADS, HG_DIM, HG_DIM), lambda g, ti: (g, 0, 0, 0))
    return pl.pallas_call(
        functools.partial(_hgrn2_kernel, n_chunks=tc // HG_CHUNK, nb=nb),
        grid=(b // nb, t // tc),
        in_specs=[tok, tok, tok, st],
        out_specs=[tok, st],
        out_shape=[jax.ShapeDtypeStruct((b, t, HG_WIDTH), F32),
                   jax.ShapeDtypeStruct((b, HG_HEADS, HG_DIM, HG_DIM), F32)],
        scratch_shapes=[pltpu.VMEM((nb * HG_HEADS, HG_DIM, HG_DIM), F32)],
        compiler_params=_cparams(("parallel", "arbitrary")),
        name="hgrn2",
    )(q, f, i, s0)


def _post_kernel(x_ref, oa_ref, ohg_ref, hog_ref, sga_ref, sgb_ref, g1_ref, sh2_ref, sc2_ref,
                 wa_ref, wb_ref, wo_ref, ng_ref, l1g_ref, l1b_ref, wr_ref, rb_ref,
                 wsg_ref, wsu_ref, wsd_ref,
                 x1_ref, h2p_ref, idx_ref, wt_ref, shd_ref):
    tm = x_ref.shape[0]
    ohg = ohg_ref[...]
    heads = []
    for h in range(HG_HEADS):
        oh = ohg[:, h * HG_DIM:(h + 1) * HG_DIM]
        heads.append(oh * lax.rsqrt(jnp.mean(oh * oh, axis=-1, keepdims=True) + RMS_EPS))
    ob = (jnp.concatenate(heads, axis=1) * ng_ref[...] * hog_ref[...]).astype(BF16)
    mixed = (sga_ref[...] * jnp.dot(oa_ref[...], wa_ref[...], preferred_element_type=F32)
             + sgb_ref[...] * jnp.dot(ob, wb_ref[...], preferred_element_type=F32))
    y = jnp.dot(mixed.astype(BF16), wo_ref[...], preferred_element_type=F32)
    x1 = _ln(DEEPNORM_ALPHA * x_ref[...] + g1_ref[0] * y) * l1g_ref[...] + l1b_ref[...]
    x1_ref[...] = x1
    h2 = _ln(x1) * (1.0 + sc2_ref[0]) + sh2_ref[0]
    h2b = h2.astype(BF16)

    half = D_MODEL // 2
    lo_bits = lax.shift_right_logical(pltpu.bitcast(h2b[:, :half].astype(F32), I32), 16)
    hi_bits = pltpu.bitcast(h2b[:, half:].astype(F32), I32) & jnp.int32(-65536)
    packed = hi_bits | lo_bits
    for j in range(half // LANES):
        h2p_ref[pl.ds(j, tm, stride=half // LANES), :] = packed[:, j * LANES:(j + 1) * LANES]

    scores = jax.nn.sigmoid(jnp.dot(h2b, wr_ref[...], preferred_element_type=F32))
    sel = scores + rb_ref[...]
    lane_e = lax.broadcasted_iota(I32, (tm, N_EXPERTS), 1).astype(F32)
    lane_o = lax.broadcasted_iota(I32, (tm, LANES), 1)
    idx_acc = jnp.zeros((tm, LANES), F32)
    w_acc = jnp.zeros((tm, LANES), F32)
    w_sum = jnp.zeros((tm, 1), F32)
    for k in range(TOP_K):
        mx = jnp.max(sel, axis=1, keepdims=True)
        ik = jnp.min(jnp.where(sel == mx, lane_e, float(N_EXPERTS)), axis=1, keepdims=True)
        hit = lane_e == ik
        wk = jnp.sum(jnp.where(hit, scores, 0.0), axis=1, keepdims=True)
        sel = jnp.where(hit, -jnp.inf, sel)
        idx_acc = jnp.where(lane_o == k, ik, idx_acc)
        w_acc = jnp.where(lane_o == k, wk, w_acc)
        w_sum = w_sum + wk
    idx_ref[...] = idx_acc.astype(I32)
    wt_ref[...] = ROUTED_SCALE * w_acc / w_sum

    sg = jnp.dot(h2b, wsg_ref[...], preferred_element_type=F32)
    su = jnp.dot(h2b, wsu_ref[...], preferred_element_type=F32)
    shd_ref[...] = jnp.dot((_silu(sg) * su).astype(BF16), wsd_ref[...], preferred_element_type=F32)


def _post(x, oa, ohg, hog, sga, sgb, g1, sh2, sc2, tiles_per_group, wts, tm):
    n = x.shape[0]
    r = g1.shape[1]
    row = lambda w: pl.BlockSpec((tm, w), lambda i: (i, 0))
    mod = pl.BlockSpec((1, r, D_MODEL), lambda i: (i // tiles_per_group, 0, 0))
    const = lambda a: pl.BlockSpec(a.shape, lambda i: (0, 0))
    sds = lambda w, dt: jax.ShapeDtypeStruct((n, w), dt)
    return pl.pallas_call(
        _post_kernel,
        grid=(n // tm,),
        in_specs=[row(D_MODEL), row(FOX_WIDTH), row(HG_WIDTH), row(HG_WIDTH), row(D_MODEL), row(D_MODEL),
                  mod, mod, mod] + [const(a) for a in wts],
        out_specs=[row(D_MODEL), pl.BlockSpec((4 * tm, LANES), lambda i: (i, 0)), row(LANES), row(LANES),
                   row(D_MODEL)],
        out_shape=[sds(D_MODEL, F32), jax.ShapeDtypeStruct((4 * n, LANES), I32), sds(LANES, I32),
                   sds(LANES, F32), sds(D_MODEL, F32)],
        compiler_params=_cparams(("parallel",)),
        name="post",
    )(x, oa, ohg, hog, sga, sgb, g1, sh2, sc2, *wts)


def _ring_copy(src_hbm, dst, sems, blk, slot):
    return pltpu.make_async_copy(src_hbm.at[blk], dst.at[slot], sems.at[slot])


def _moe_kernel(off_ref, nv_ref, g4_hbm, s8_hbm, rw_hbm, hp_hbm, hs_hbm, wg_hbm, wu_hbm, wd_hbm, out_ref,
                hv_sc, acc_sc, xt_sc, y_sc, wg_sc, wu_sc, wd_sc, g4_sm, s8_sm, rw_sm,
                sem, gsem, ssem, rsem, wsem, *, ct, n_chunks):
    n_valid = nv_ref[0]
    streams = [(g4_hbm, g4_sm, gsem), (s8_hbm, s8_sm, ssem), (rw_hbm, rw_sm, rsem)]

    def weight_copies(e, slot):
        return [pltpu.make_async_copy(wg_hbm.at[e], wg_sc.at[slot], wsem.at[0, slot]),
                pltpu.make_async_copy(wu_hbm.at[e], wu_sc.at[slot], wsem.at[1, slot]),
                pltpu.make_async_copy(wd_hbm.at[e], wd_sc.at[slot], wsem.at[2, slot])]

    for j in range(RING_SLOTS):
        for src, dst, sems in streams:
            _ring_copy(src, dst, sems, j, j).start()
    for g0 in range(WEIGHT_SLOTS - 1):
        for cp in weight_copies(g0, g0):
            cp.start()

    def arm(j0, n_blk, wslot):
        m_rows = n_blk * MOE_ROWS
        slots = [(j0 + k) % RING_SLOTS for k in range(n_blk)]
        for k in range(n_blk):
            for src, dst, sems in streams:
                _ring_copy(src, dst, sems, j0 + k, slots[k]).wait()
        for r in range(m_rows):
            t4 = pl.multiple_of(g4_sm[slots[r // MOE_ROWS], 0, r % MOE_ROWS], 4)
            xt_sc[4 * r:4 * r + 4, :] = hv_sc[pl.ds(t4, 4), :]
        lo, hi = [], []
        for j in range(4):
            w = xt_sc[pl.ds(j, m_rows, stride=4), :]
            lo.append(pltpu.bitcast(w << 16, F32))
            hi.append(pltpu.bitcast(w & jnp.int32(-65536), F32))
        x = jnp.concatenate(lo + hi, axis=1)
        gate = jnp.dot(x, wg_sc[wslot], preferred_element_type=F32)
        up = jnp.dot(x, wu_sc[wslot], preferred_element_type=F32)
        y = jnp.dot(_silu(gate) * up, wd_sc[wslot], preferred_element_type=F32)
        tiles = D_MODEL // LANES
        for j in range(tiles):
            y_sc[pl.ds(j, m_rows, stride=tiles), :] = y[:, j * LANES:(j + 1) * LANES]
        for r0 in range(0, m_rows, SCATTER_BATCH):
            new = []
            for r in range(r0, r0 + SCATTER_BATCH):
                k, m = r // MOE_ROWS, r % MOE_ROWS
                r8 = pl.multiple_of(s8_sm[slots[k], 0, m], 8)
                new.append((r8, acc_sc[pl.ds(r8, 8), :] + rw_sm[slots[k], 0, m] * y_sc[8 * r:8 * r + 8, :]))
            for r8, val in new:
                acc_sc[pl.ds(r8, 8), :] = val
        for k in range(n_blk):

            @pl.when(j0 + k + RING_SLOTS < n_valid)
            def _():
                for src, dst, sems in streams:
                    _ring_copy(src, dst, sems, j0 + k + RING_SLOTS, slots[k]).start()

    last = n_chunks - 1
    tail_p = hp_hbm.shape[0] - last * ct * 4

    def chunk(c, carry):
        @pl.when(c < last)
        def _():
            cp = pltpu.make_async_copy(hp_hbm.at[pl.ds(pl.multiple_of(c * (ct * 4), 8), ct * 4)], hv_sc, sem.at[0])
            cp.start()
            cp.wait()

        @pl.when(c == last)
        def _():
            cps = [pltpu.make_async_copy(hs_hbm, hv_sc.at[pl.ds(tail_p, hs_hbm.shape[0])], sem.at[1])]
            if tail_p:
                cps.append(pltpu.make_async_copy(hp_hbm.at[pl.ds(last * ct * 4, tail_p)],
                                                 hv_sc.at[pl.ds(0, tail_p)], sem.at[0]))
            for cp in cps:
                cp.start()
            for cp in cps:
                cp.wait()

        acc_sc[...] = jnp.zeros(acc_sc.shape, F32)

        def expert(e, carry2):
            g = c * N_EXPERTS + e
            wslot = g % WEIGHT_SLOTS
            for wcp in weight_copies(e, wslot):
                wcp.wait()
            nxt = g + WEIGHT_SLOTS - 1

            @pl.when(nxt < n_chunks * N_EXPERTS)
            def _():
                for wcp in weight_copies(nxt % N_EXPERTS, nxt % WEIGHT_SLOTS):
                    wcp.start()

            b0 = off_ref[g]
            n_blk = off_ref[g + 1] - b0

            def pair(i, carry3):
                arm(b0 + 2 * i, 2, wslot)
                return carry3

            lax.fori_loop(0, n_blk // 2, pair, 0)

            @pl.when(n_blk % 2 == 1)
            def _():
                arm(b0 + n_blk - 1, 1, wslot)

            return carry2

        lax.fori_loop(0, N_EXPERTS, expert, 0)
        out = pltpu.make_async_copy(acc_sc.at[pl.ds(0, ct * 8)], out_ref.at[c], sem.at[0])
        out.start()
        out.wait()
        return carry

    lax.fori_loop(0, n_chunks, chunk, 0)


def _moe(blk_off, n_valid, g4_rows, s8_rows, row_w, h2p_p, h2p_s, w_gate, w_up, w_down, n_chunks, ct):
    assert n_chunks * N_EXPERTS >= WEIGHT_SLOTS
    tail_p = h2p_p.shape[0] - (n_chunks - 1) * ct * 4
    assert tail_p >= 0 and tail_p % 8 == 0 and tail_p + h2p_s.shape[0] == ct * 4
    hbm = pl.BlockSpec(memory_space=pl.ANY)
    ring_sems = pltpu.SemaphoreType.DMA((RING_SLOTS,))
    ring_i32 = pltpu.SMEM((RING_SLOTS, 1, MOE_ROWS), I32)
    m_max = 2 * MOE_ROWS
    return pl.pallas_call(
        functools.partial(_moe_kernel, ct=ct, n_chunks=n_chunks),
        grid_spec=pltpu.PrefetchScalarGridSpec(
            num_scalar_prefetch=2,
            grid=(1,),
            in_specs=[hbm] * 8,
            out_specs=pl.BlockSpec(memory_space=pl.ANY),
            scratch_shapes=[pltpu.VMEM((ct * 4, LANES), I32), pltpu.VMEM(((ct + 1) * 8, LANES), F32),
                            pltpu.VMEM((4 * m_max, LANES), I32), pltpu.VMEM((8 * m_max, LANES), F32),
                            pltpu.VMEM((WEIGHT_SLOTS, D_MODEL, EXPERT_DIM), F32),
                            pltpu.VMEM((WEIGHT_SLOTS, D_MODEL, EXPERT_DIM), F32),
                            pltpu.VMEM((WEIGHT_SLOTS, EXPERT_DIM, D_MODEL), F32),
                            ring_i32, ring_i32, pltpu.SMEM((RING_SLOTS, 1, MOE_ROWS), F32),
                            pltpu.SemaphoreType.DMA((2,)), ring_sems, ring_sems, ring_sems,
                            pltpu.SemaphoreType.DMA((3, WEIGHT_SLOTS))]),
        out_shape=jax.ShapeDtypeStruct((n_chunks, ct * 8, LANES), F32),
        compiler_params=_cparams(("arbitrary",)),
        name="moe",
    )(blk_off, n_valid, g4_rows, s8_rows, row_w, h2p_p, h2p_s, w_gate, w_up, w_down)


def _dispatch(idx, wts, n, n_chunks):
    ct = n // n_chunks
    n_pairs = n * TOP_K
    n_groups = n_chunks * N_EXPERTS
    fill = MOE_ROWS - 1
    n_rows = -(-(n_pairs + n_groups * fill) // MOE_ROWS) * MOE_ROWS
    tok = jnp.arange(n_pairs, dtype=I32) // TOP_K
    grp = (tok // ct) * N_EXPERTS + idx.reshape(n_pairs)
    gid = jnp.arange(n_groups, dtype=I32)
    counts = jnp.sum((grp[:, None] == gid[None, :]).astype(I32), axis=0)
    need = (-counts) % MOE_ROWS
    fill_key = jnp.where(jnp.arange(fill, dtype=I32)[None, :] < need[:, None], gid[:, None], n_groups)
    n_tail = n_rows - n_pairs - n_groups * fill
    keys = jnp.concatenate([grp, fill_key.reshape(-1), jnp.full((n_tail,), n_groups, I32)])
    n_fill = n_rows - n_pairs
    assert ct < TOK_FILLER and n_pairs // MOE_ROWS >= RING_SLOTS
    toks = jnp.concatenate([tok % ct, jnp.full((n_fill,), ct, I32)])
    ws = jnp.concatenate([wts.reshape(n_pairs), jnp.zeros((n_fill,), F32)])
    s_word, s_w = lax.sort((keys * (TOK_FILLER + 1) + toks, ws), num_keys=1)
    s_tok = s_word % (TOK_FILLER + 1)
    blk_cnt = (counts + need) // MOE_ROWS
    blk_off = jnp.concatenate([jnp.zeros((1,), I32), jnp.cumsum(blk_cnt).astype(I32)])
    shape3 = (n_rows // MOE_ROWS, 1, MOE_ROWS)
    gather4 = jnp.where(s_tok == ct, 0, s_tok * 4)
    scatter8 = s_tok * 8
    return blk_off, blk_off[-1:], gather4.reshape(shape3), scatter8.reshape(shape3), s_w.reshape(shape3)


def _final_kernel(x1_ref, r_ref, s_ref, g2_ref, lg_ref, lb_ref, o_ref):
    tm = x1_ref.shape[0]
    tiles = D_MODEL // LANES
    routed = jnp.concatenate([r_ref[pl.ds(j, tm, stride=tiles), :] for j in range(tiles)], axis=1)
    y = routed + s_ref[...]
    o_ref[...] = _ln(DEEPNORM_ALPHA * x1_ref[...] + g2_ref[0] * y) * lg_ref[...] + lb_ref[...]


def _final(x1, routed, row_off_tiles, shared, g2, tiles_per_group, ln_g, ln_b, tm):
    n = x1.shape[0]
    r = g2.shape[1]
    row = pl.BlockSpec((tm, D_MODEL), lambda i: (i, 0))
    return pl.pallas_call(
        _final_kernel,
        grid=(n // tm,),
        in_specs=[row, pl.BlockSpec((tm * (D_MODEL // LANES), LANES), lambda i: (i + row_off_tiles, 0)), row,
                  pl.BlockSpec((1, r, D_MODEL), lambda i: (i // tiles_per_group, 0, 0)),
                  pl.BlockSpec((1, D_MODEL), lambda i: (0, 0)), pl.BlockSpec((1, D_MODEL), lambda i: (0, 0))],
        out_specs=row,
        out_shape=jax.ShapeDtypeStruct((n, D_MODEL), F32),
        compiler_params=_cparams(("parallel",)),
        name="final",
    )(x1, routed, shared, g2, ln_g, ln_b)


def kernel(x_prompt, x_sample, c_prompt, c_sample, cache_k, cache_v, cache_logf, state_hg, page_table, w_ada, b_ada, w_in, b_fox_f, hg_lb_logits, hg_norm_g, w_branch_a, w_branch_b, w_out, ln1_g, ln1_b, w_router, router_bias, w_exp_gate, w_exp_up, w_exp_down, w_sh_gate, w_sh_up, w_sh_down, ln2_g, ln2_b):
    assert w_ada.shape[0] == DEPTH and hg_lb_logits.shape[0] == DEPTH + 1
    b, t, d = x_prompt.shape
    bd, ts, _ = x_sample.shape
    n_p, n_s = b * t, bd * ts
    n = n_p + n_s
    n_phys = cache_k.shape[1]
    n_pages = page_table.shape[1]
    tm_p = 256
    tm_s = min(256, n_s)
    tq = min(256, t)
    assert n_p % tm_p == 0 and n_s % tm_s == 0 and t % tq == 0 and t % PAGE == 0
    assert HG_CHUNK % ts == 0

    pad_f = jnp.zeros((d, LANES - FOX_HEADS), F32)
    w_all = jnp.concatenate([w_in[0][:, :OFF_F + FOX_HEADS], pad_f, w_in[0][:, OFF_F + FOX_HEADS:]], axis=1).astype(BF16)
    bf_pad = jnp.concatenate([b_fox_f[0], jnp.zeros((LANES - FOX_HEADS,), F32)]).reshape(1, LANES)
    post_w = (w_branch_a[0].astype(BF16), w_branch_b[0].astype(BF16), w_out[0].astype(BF16),
              jnp.tile(hg_norm_g[0], HG_HEADS).reshape(1, HG_WIDTH), ln1_g[0].reshape(1, d), ln1_b[0].reshape(1, d),
              w_router[0].astype(BF16), router_bias[0].reshape(1, N_EXPERTS),
              w_sh_gate[0].astype(BF16), w_sh_up[0].astype(BF16), w_sh_down[0].astype(BF16))
    r_all = b + bd
    r_pad = -(-r_all // 8) * 8
    c_all = jnp.concatenate([c_prompt, c_sample, jnp.zeros((r_pad - r_all, d), F32)], axis=0)
    mod = _adaln(c_all, w_ada[0], b_ada[0])
    mod_p = [mod[:b, i * d:(i + 1) * d].reshape(b, 1, d) for i in range(6)]
    reps = tm_s // ts
    mod_s = [jnp.repeat(mod[b:b + bd, i * d:(i + 1) * d], ts, axis=0).reshape(n_s // tm_s, tm_s, d) for i in range(6)]
    tpg_p = t // tm_p

    (q_p, kb_p, vb_p, kf_p, vf_p, lf_p, hq_p, hf_p, hi_p, hog_p, sga_p, sgb_p) = _inproj(
        x_prompt.reshape(n_p, d), mod_p[0], mod_p[1], tpg_p, w_all, bf_pad, hg_lb_logits, tm_p, seq_tiles=tpg_p)
    (q_s, kb_s, vb_s, kf_s, vf_s, lf_s, hq_s, hf_s, hi_s, hog_s, sga_s, sgb_s) = _inproj(
        x_sample.reshape(n_s, d), mod_s[0], mod_s[1], 1, w_all, bf_pad, hg_lb_logits, tm_s)

    lf_t = lf_p.reshape(b, t, FOX_HEADS).transpose(0, 2, 1)
    ck = _cum_seq(lf_t.reshape(b * FOX_HEADS * (t // PAGE), PAGE), t // PAGE).reshape(b, FOX_HEADS // 2, 2, t)
    oa_p = _fox_prompt(q_p.reshape(b, t, FOX_WIDTH), kb_p.reshape(b, t, FOX_WIDTH),
                       vb_p.reshape(b, t, FOX_WIDTH), ck, tq)

    rows_c = n_phys * FOX_HEADS
    tb = 4096 if rows_c % 4096 == 0 else rows_c
    cum_pages = _cum_rows(cache_logf[0].transpose(0, 2, 1).reshape(rows_c, PAGE), tb).reshape(n_phys, FOX_HEADS, PAGE)
    lf_new = jnp.pad(lf_s.reshape(bd, ts, FOX_HEADS).transpose(0, 2, 1), ((0, 0), (0, 0), (0, PAGE - ts)))
    cum_new = _cum_rows(lf_new.reshape(bd * FOX_HEADS, PAGE), bd * FOX_HEADS).reshape(bd, FOX_HEADS, PAGE)
    pad8 = lambda a: jnp.pad(a.reshape(bd, ts, FOX_WIDTH), ((0, 0), (0, 8 - ts), (0, 0)))
    pages = 16 if n_pages % 16 == 0 else n_pages
    k_t = cache_k[0].transpose(0, 2, 3, 1).reshape(n_phys, FOX_WIDTH, PAGE)
    v_t = cache_v[0].transpose(0, 2, 3, 1).reshape(n_phys, FOX_WIDTH, PAGE)
    oa_s = _fox_sample(page_table, q_s.reshape(bd, ts, FOX_WIDTH), pad8(kf_s), pad8(vf_s), cum_new,
                       k_t, v_t, cum_pages, pages)

    tc = min(512, t)
    nb_p = 2 if b % 2 == 0 else 1
    nb_s = 2 if bd % 2 == 0 else 1
    ohg_p, st_p = _hgrn2(hq_p.reshape(b, t, HG_WIDTH), hf_p.reshape(b, t, HG_WIDTH), hi_p.reshape(b, t, HG_WIDTH),
                         jnp.zeros((b, HG_HEADS, HG_DIM, HG_DIM), F32), tc, nb_p)
    padc = lambda a, v: jnp.pad(a.reshape(bd, ts, HG_WIDTH), ((0, 0), (0, HG_CHUNK - ts), (0, 0)), constant_values=v)
    ohg_s, st_s = _hgrn2(padc(hq_s, 0.0), padc(hf_s, 1.0), padc(hi_s, 0.0), state_hg[0], HG_CHUNK, nb_s)
    ohg_s = ohg_s[:, :ts].reshape(n_s, HG_WIDTH)

    x1_p, h2p_p, idx_p, wt_p, shd_p = _post(
        x_prompt.reshape(n_p, d), oa_p.reshape(n_p, FOX_WIDTH), ohg_p.reshape(n_p, HG_WIDTH), hog_p, sga_p, sgb_p,
        mod_p[2], mod_p[3], mod_p[4], tpg_p, post_w, tm_p)
    x1_s, h2p_s, idx_s, wt_s, shd_s = _post(
        x_sample.reshape(n_s, d), oa_s.reshape(n_s, FOX_WIDTH), ohg_s, hog_s, sga_s, sgb_s,
        mod_s[2], mod_s[3], mod_s[4], 1, post_w, tm_s)

    n_chunks = next(c for c in MOE_CHUNKS if n % (2 * c) == 0)
    idx = jnp.concatenate([idx_p[:, :TOP_K], idx_s[:, :TOP_K]], axis=0)
    wts = jnp.concatenate([wt_p[:, :TOP_K], wt_s[:, :TOP_K]], axis=0)
    blk_off, n_valid, g4_rows, s8_rows, row_w = _dispatch(idx, wts, n, n_chunks)
    routed = _moe(blk_off, n_valid, g4_rows, s8_rows, row_w, h2p_p, h2p_s, w_exp_gate[0], w_exp_up[0],
                  w_exp_down[0], n_chunks, n // n_chunks).reshape(n * (d // LANES), LANES)

    lg, lb2 = ln2_g[0].reshape(1, d), ln2_b[0].reshape(1, d)
    y_p = _final(x1_p, routed, 0, shd_p, mod_p[5], tpg_p, lg, lb2, tm_p)
    y_s = _final(x1_s, routed, n_p // tm_s, shd_s, mod_s[5], 1, lg, lb2, tm_s)

    ldt, sdt = cache_logf.dtype, state_hg.dtype
    hd = (FOX_HEADS, FOX_HEAD_DIM)
    heads_last = lambda a: a.reshape((b,) + hd + (t,)).transpose(0, 3, 1, 2)[None]
    return (y_p.reshape(b, t, d), y_s.reshape(bd, ts, d),
            heads_last(kf_p), heads_last(vf_p),
            lf_p.reshape(1, b, t, FOX_HEADS).astype(ldt), st_p[None].astype(sdt),
            kf_s.reshape((1, bd, ts) + hd), vf_s.reshape((1, bd, ts) + hd),
            lf_s.reshape(1, bd, ts, FOX_HEADS).astype(ldt), st_s[None].astype(sdt))
```

```python
import functools

import jax
import jax.numpy as jnp
import numpy as np
from jax import lax
from jax.experimental import pallas as pl
from jax.experimental.pallas import tpu as pltpu

F32 = jnp.float32
BF16 = jnp.bfloat16
I32 = jnp.int32

D_MODEL = 1024
FOX_HEADS = 8
FOX_HEAD_DIM = 64
FOX_WIDTH = FOX_HEADS * FOX_HEAD_DIM
HG_HEADS = 4
HG_DIM = 128
HG_WIDTH = HG_HEADS * HG_DIM
HG_CHUNK = 32
N_EXPERTS = 256
TOP_K = 8
EXPERT_DIM = 256
ROUTED_SCALE = 2.5
DEPTH = 1
DEEPNORM_ALPHA = (2.0 * DEPTH) ** 0.25
LN_EPS = 1e-5
RMS_EPS = 1e-6
NEG_INF = -1e30
PAGE = 128
LANES = 128
MOE_ROWS = 64
MOE_CHUNKS = (3, 2, 4, 1)
SCATTER_BATCH = 4
WEIGHT_SLOTS = 3
RING_SLOTS = 4
PAGE_BUFFERS = 3
TOK_FILLER = (1 << 15) - 1
VMEM_LIMIT = 56 * 1024 * 1024

OFF_Q, OFF_K, OFF_V, OFF_F = 0, 512, 1024, 1536
OFF_HQ, OFF_HF, OFF_HI, OFF_HOG = 1664, 2176, 2688, 3200
OFF_GA, OFF_GB, W_ALL_COLS = 3712, 4736, 5760


def _cparams(sem):
    return pltpu.CompilerParams(dimension_semantics=sem, vmem_limit_bytes=VMEM_LIMIT)


def _ln(x):
    mu = jnp.mean(x, axis=-1, keepdims=True)
    xc = x - mu
    var = jnp.mean(xc * xc, axis=-1, keepdims=True)
    return xc * lax.rsqrt(var + LN_EPS)


def _silu(x):
    return x * jax.nn.sigmoid(x)


def _split3(x):
    hi = x.astype(BF16)
    r1 = x - hi.astype(F32)
    mid = r1.astype(BF16)
    lo = (r1 - mid.astype(F32)).astype(BF16)
    return hi, mid, lo


def _dot3(x, m):
    hi, mid, lo = _split3(x)
    d = lambda a: jnp.dot(a, m, preferred_element_type=F32)
    return d(hi) + d(mid) + d(lo)


def _dot_nt(a, b):
    return lax.dot_general(a, b, (((1,), (1,)), ((), ())), preferred_element_type=F32)


def _dot_tn(a, b):
    return lax.dot_general(a, b, (((0,), (0,)), ((), ())), preferred_element_type=F32)


def _adaln_kernel(c_ref, w_ref, b_ref, o_ref):
    s = _silu(c_ref[...]).astype(BF16)
    o_ref[...] = jnp.dot(s, w_ref[...].astype(BF16), preferred_element_type=F32) + b_ref[...]


def _adaln(c, w_ada, b_ada):
    r = c.shape[0]
    tn = 1024
    return pl.pallas_call(
        _adaln_kernel,
        grid=(6 * D_MODEL // tn,),
        in_specs=[pl.BlockSpec((r, D_MODEL), lambda j: (0, 0)),
                  pl.BlockSpec((D_MODEL, tn), lambda j: (0, j)),
                  pl.BlockSpec((1, tn), lambda j: (0, j))],
        out_specs=pl.BlockSpec((r, tn), lambda j: (0, j)),
        out_shape=jax.ShapeDtypeStruct((r, 6 * D_MODEL), F32),
        compiler_params=_cparams(("parallel",)),
        name="adaln",
    )(c, w_ada, b_ada.reshape(1, -1))


def _inproj_kernel(x_ref, sh_ref, sc_ref, w_ref, bf_ref, lbl_ref,
                   q_ref, kb_ref, vb_ref, kf_ref, vf_ref, lf_ref,
                   hq_ref, hf_ref, hi_ref, hog_ref, sga_ref, sgb_ref, *, kv_transposed):
    h = _ln(x_ref[...]) * (1.0 + sc_ref[0]) + sh_ref[0]
    hb = h.astype(BF16)

    def proj(a, b):
        return jnp.dot(hb, w_ref[:, a:b], preferred_element_type=F32)

    q_ref[...] = (proj(OFF_Q, OFF_K) * (FOX_HEAD_DIM ** -0.5)).astype(BF16)
    k = proj(OFF_K, OFF_V)
    kb_ref[...] = k.astype(BF16)
    v = proj(OFF_V, OFF_F)
    vb_ref[...] = v.astype(BF16)
    if kv_transposed:
        kf_ref[0] = jnp.transpose(k)
        vf_ref[0] = jnp.transpose(v)
    else:
        kf_ref[...] = k
        vf_ref[...] = v
    logf = jax.nn.log_sigmoid(proj(OFF_F, OFF_HQ) + bf_ref[...])
    lf_ref[...] = logf[:, :FOX_HEADS]

    hq_ref[...] = _silu(proj(OFF_HQ, OFF_HF))
    l0 = lbl_ref[0:1, :]
    l1 = lbl_ref[1:2, :]
    mx = jnp.maximum(l0, l1)
    e0 = jnp.exp(l0 - mx)
    lb = e0 / (e0 + jnp.exp(l1 - mx))
    hf_ref[...] = lb + (1.0 - lb) * jax.nn.sigmoid(proj(OFF_HF, OFF_HI))
    hi_ref[...] = proj(OFF_HI, OFF_HOG)
    hog_ref[...] = _silu(proj(OFF_HOG, OFF_GA))
    sga_ref[...] = jax.nn.sigmoid(proj(OFF_GA, OFF_GB))
    sgb_ref[...] = jax.nn.sigmoid(proj(OFF_GB, W_ALL_COLS))


def _inproj(x, sh, sc, tiles_per_group, w_all, bf_pad, lbl, tm, seq_tiles=None):
    n = x.shape[0]
    r = sh.shape[1]
    row = lambda w: pl.BlockSpec((tm, w), lambda i: (i, 0))
    mod = pl.BlockSpec((1, r, D_MODEL), lambda i: (i // tiles_per_group, 0, 0))
    const = lambda s: pl.BlockSpec(s, lambda i: (0, 0))
    sds = lambda w, dt: jax.ShapeDtypeStruct((n, w), dt)
    if seq_tiles is None:
        kv_spec, kv_sds = row(FOX_WIDTH), sds(FOX_WIDTH, F32)
    else:
        kv_spec = pl.BlockSpec((1, FOX_WIDTH, tm), lambda i: (i // seq_tiles, 0, i % seq_tiles))
        kv_sds = jax.ShapeDtypeStruct((n // (tm * seq_tiles), FOX_WIDTH, tm * seq_tiles), F32)
    return pl.pallas_call(
        functools.partial(_inproj_kernel, kv_transposed=seq_tiles is not None),
        grid=(n // tm,),
        in_specs=[row(D_MODEL), mod, mod,
                  pl.BlockSpec((D_MODEL, W_ALL_COLS), lambda i: (0, 0), pipeline_mode=pl.Buffered(1)),
                  const((1, LANES)), const((2, HG_WIDTH))],
        out_specs=[row(FOX_WIDTH)] * 3 + [kv_spec] * 2 + [row(FOX_HEADS)] + [row(HG_WIDTH)] * 4 + [row(D_MODEL)] * 2,
        out_shape=[sds(FOX_WIDTH, BF16)] * 3 + [kv_sds] * 2 + [sds(FOX_HEADS, F32)]
        + [sds(HG_WIDTH, F32)] * 4 + [sds(D_MODEL, F32)] * 2,
        compiler_params=_cparams(("parallel",)),
        name="inproj",
    )(x, sh, sc, w_all, bf_pad, lbl)


def _upper_ones():
    r = np.arange(PAGE)
    return jnp.asarray(r[:, None] <= r[None, :], BF16)


def _cum_rows_kernel(x_ref, u_ref, o_ref):
    o_ref[...] = _dot3(x_ref[...], u_ref[...])


def _cum_rows(x, tb):
    n = x.shape[0]
    return pl.pallas_call(
        _cum_rows_kernel,
        grid=(n // tb,),
        in_specs=[pl.BlockSpec((tb, PAGE), lambda i: (i, 0)), pl.BlockSpec((PAGE, PAGE), lambda i: (0, 0))],
        out_specs=pl.BlockSpec((tb, PAGE), lambda i: (i, 0)),
        out_shape=jax.ShapeDtypeStruct((n, PAGE), F32),
        compiler_params=_cparams(("parallel",)),
        name="cum_rows",
    )(x, _upper_ones())


def _cum_seq_kernel(x_ref, u_ref, g_ref, o_ref):
    x = x_ref[...]
    within = _dot3(x, u_ref[...])
    tot = _dot3(x, jnp.ones((PAGE, PAGE), BF16))
    hi, mid, lo = _split3(tot)
    g = g_ref[...]
    d = lambda a: jnp.dot(g, a, preferred_element_type=F32)
    o_ref[...] = within + d(hi) + d(mid) + d(lo)


def _cum_seq(x, rows_per_seq):
    nr = x.shape[0]
    r = np.arange(nr)
    g = (r[:, None] // rows_per_seq == r[None, :] // rows_per_seq) & (r[None, :] < r[:, None])
    return pl.pallas_call(
        _cum_seq_kernel,
        out_shape=jax.ShapeDtypeStruct(x.shape, F32),
        compiler_params=pltpu.CompilerParams(vmem_limit_bytes=VMEM_LIMIT),
        name="cum_seq",
    )(x, _upper_ones(), jnp.asarray(g, BF16))


def _fox_prompt_kernel(q_ref, k_ref, v_ref, ck_ref, o_ref, m_sc, l_sc, acc_sc, cq_sc, s_sc, p_sc, *, tq, strip):
    i = pl.program_id(2)
    q = q_ref[0]
    lo_lanes = lax.broadcasted_iota(I32, (tq, LANES), 1) < FOX_HEAD_DIM
    zero = jnp.zeros_like(q)
    qh = (jnp.where(lo_lanes, q, zero), jnp.where(lo_lanes, zero, q))
    q_off = pl.multiple_of(i * tq, tq)
    cq_rows = ck_ref[0, 0, :, pl.ds(q_off, tq)]
    for h in range(2):
        cq_sc[h] = jnp.transpose(jnp.broadcast_to(cq_rows[h:h + 1, :], (LANES, tq)))
    reps = tq // LANES
    row = lax.broadcasted_iota(I32, (strip, tq), 0)
    col = lax.broadcasted_iota(I32, (strip, tq), 1)

    m_sc[...] = jnp.full(m_sc.shape, NEG_INF, F32)
    l_sc[...] = jnp.zeros(l_sc.shape, F32)
    acc_sc[...] = jnp.zeros(acc_sc.shape, F32)

    def scores(j, slot):
        kt = k_ref[0, pl.ds(pl.multiple_of(j * tq, tq), tq), :]
        for h in range(2):
            s_sc[slot, h] = _dot_nt(qh[h], kt)

    scores(0, 0)

    def kv_step(j, slot, diag):
        if not diag:
            scores(j + 1, 1 - slot)
        k_off = pl.multiple_of(j * tq, tq)
        vt = v_ref[0, pl.ds(k_off, tq), :]
        ckt = ck_ref[0, 0, :, pl.ds(k_off, tq)]
        for h in range(2):
            ck_row = ckt[h:h + 1, :]
            for r0 in range(0, tq, strip):
                rs = slice(r0, r0 + strip)
                logits = s_sc[slot, h, rs, :] + jnp.concatenate([cq_sc[h, rs, :]] * reps, axis=1) - ck_row
                if diag:
                    logits = jnp.where(col <= row + r0, logits, NEG_INF)
                m_prev = m_sc[h, rs, :]
                m_new = jnp.maximum(m_prev, jnp.max(logits, axis=1, keepdims=True))
                p = jnp.exp(logits - jnp.concatenate([m_new] * reps, axis=1))
                alpha = jnp.exp(m_prev - m_new)
                l_sc[h, rs, :] = alpha * l_sc[h, rs, :] + jnp.sum(p, axis=1, keepdims=True)
                acc_sc[h, rs, :] = alpha * acc_sc[h, rs, :]
                m_sc[h, rs, :] = m_new
                p_sc[h, rs, :] = p.astype(BF16)
        for h in range(2):
            acc_sc[h] += jnp.dot(p_sc[h], vt, preferred_element_type=F32)

    def body(jj, c):
        kv_step(2 * jj, 0, False)
        kv_step(2 * jj + 1, 1, False)
        return c

    lax.fori_loop(0, i // 2, body, 0)

    @pl.when(i % 2 == 0)
    def _():
        kv_step(i, 0, True)

    @pl.when(i % 2 == 1)
    def _():
        kv_step(i - 1, 0, False)
        kv_step(i, 1, True)

    o = jnp.where(lo_lanes, acc_sc[0] / l_sc[0], acc_sc[1] / l_sc[1])
    o_ref[0] = o.astype(BF16)


def _fox_prompt(q, k, v, ck, tq):
    b, t, _ = q.shape
    pairs = FOX_HEADS // 2
    return pl.pallas_call(
        functools.partial(_fox_prompt_kernel, tq=tq, strip=32),
        grid=(b, pairs, t // tq),
        in_specs=[pl.BlockSpec((1, tq, LANES), lambda bi, p, i: (bi, i, p)),
                  pl.BlockSpec((1, t, LANES), lambda bi, p, i: (bi, 0, p)),
                  pl.BlockSpec((1, t, LANES), lambda bi, p, i: (bi, 0, p)),
                  pl.BlockSpec((1, 1, 2, t), lambda bi, p, i: (bi, p, 0, 0))],
        out_specs=pl.BlockSpec((1, tq, LANES), lambda bi, p, i: (bi, i, p)),
        out_shape=jax.ShapeDtypeStruct((b, t, FOX_WIDTH), BF16),
        scratch_shapes=[pltpu.VMEM((2, tq, LANES), F32)] * 4
        + [pltpu.VMEM((2, 2, tq, tq), F32), pltpu.VMEM((2, tq, tq), BF16)],
        compiler_params=_cparams(("parallel", "parallel", "arbitrary")),
        name="fox_prompt",
    )(q, k, v, ck)


def _fox_sample_kernel(pt_ref, q_ref, kn_ref, vn_ref, cn_ref, k_hbm, v_hbm, c_hbm, o_ref,
                       kbuf, vbuf, cbuf, m_sc, l_sc, acc_sc, off_sc, ksem, vsem, csem, *, pages):
    g = pl.program_id(1)
    groups = pl.num_programs(1)
    n_q = q_ref.shape[1]
    rows = n_q * FOX_HEADS

    step = pl.program_id(0) * groups + g
    n_steps = pl.num_programs(0) * groups
    ahead = PAGE_BUFFERS - 1

    def page_copies(s, slot):
        sb, sg = s // groups, s % groups
        copies = []
        for i in range(pages):
            pg = pt_ref[sb, sg * pages + i]
            copies += [pltpu.make_async_copy(k_hbm.at[pg], kbuf.at[slot, i], ksem.at[slot]),
                       pltpu.make_async_copy(v_hbm.at[pg], vbuf.at[slot, i], vsem.at[slot]),
                       pltpu.make_async_copy(c_hbm.at[pg], cbuf.at[slot, i], csem.at[slot])]
        return copies

    @pl.when(step == 0)
    def _():
        for s in range(ahead):
            for cp in page_copies(s, s):
                cp.start()

    slot = step % PAGE_BUFFERS
    for cp in page_copies(step, slot):
        cp.wait()

    @pl.when(step + ahead < n_steps)
    def _():
        for cp in page_copies(step + ahead, (step + ahead) % PAGE_BUFFERS):
            cp.start()

    @pl.when(g == 0)
    def _():
        m_sc[...] = jnp.full(m_sc.shape, NEG_INF, F32)
        l_sc[...] = jnp.zeros(l_sc.shape, F32)
        acc_sc[...] = jnp.zeros(acc_sc.shape, F32)
        off_sc[...] = jnp.zeros(off_sc.shape, F32)

    q4 = q_ref[0].astype(F32)
    head_of_lane = lax.broadcasted_iota(I32, (FOX_HEADS, FOX_WIDTH), 1) // FOX_HEAD_DIM
    bmask = head_of_lane == lax.broadcasted_iota(I32, (FOX_HEADS, FOX_WIDTH), 0)
    wq = jnp.concatenate(
        [jnp.where(bmask, jnp.broadcast_to(q4[t:t + 1, :], (FOX_HEADS, FOX_WIDTH)), 0.0) for t in range(n_q)], axis=0)

    def update(s_all, pv_fn):
        m_prev = m_sc[...]
        m_new = jnp.maximum(m_prev, jnp.max(s_all, axis=1, keepdims=True))
        p = jnp.exp(s_all - jnp.concatenate([m_new] * (s_all.shape[1] // LANES), axis=1))
        alpha = jnp.exp(m_prev - m_new)
        l_sc[...] = alpha * l_sc[...] + jnp.sum(p, axis=1, keepdims=True)
        acc_sc[...] = jnp.concatenate([alpha] * (FOX_WIDTH // LANES), axis=1) * acc_sc[...] + pv_fn(p)
        m_sc[...] = m_new

    off = off_sc[...]
    s_list = []
    for i in range(pages):
        within = cbuf[slot, i]
        s = jnp.dot(wq, kbuf[slot, i], preferred_element_type=F32)
        s_list.append(s - jnp.concatenate([off + within] * n_q, axis=0))
        off = off + jnp.broadcast_to(within[:, PAGE - 1:PAGE], (FOX_HEADS, PAGE))
    off_sc[...] = off

    def pv_pages(p):
        pv = _dot_nt(p[:, :PAGE], vbuf[slot, 0])
        for i in range(1, pages):
            pv = pv + _dot_nt(p[:, i * PAGE:(i + 1) * PAGE], vbuf[slot, i])
        return pv

    update(jnp.concatenate(s_list, axis=1), pv_pages)

    @pl.when(g == pl.num_programs(1) - 1)
    def _():
        pad = jnp.zeros((PAGE - kn_ref.shape[1], FOX_WIDTH), F32)
        kn = jnp.concatenate([kn_ref[0], pad], axis=0)
        vn = jnp.concatenate([vn_ref[0], pad], axis=0)
        s = _dot_nt(wq, kn) - jnp.concatenate([off + cn_ref[0]] * n_q, axis=0)
        t_of_row = lax.broadcasted_iota(I32, (rows, PAGE), 0) // FOX_HEADS
        key = lax.broadcasted_iota(I32, (rows, PAGE), 1)
        update(jnp.where(key <= t_of_row, s, NEG_INF),
               lambda p: jnp.dot(p, vn, preferred_element_type=F32))
        o32 = acc_sc[...] / jnp.concatenate([l_sc[...]] * (FOX_WIDTH // LANES), axis=1)
        outs = []
        for t in range(n_q):
            blk = o32[t * FOX_HEADS:(t + 1) * FOX_HEADS, :]
            outs.append(jnp.sum(jnp.where(bmask, blk, 0.0), axis=0, keepdims=True))
        o_ref[0] = jnp.concatenate(outs, axis=0).astype(BF16)


def _fox_sample(page_table, q, k_new, v_new, cum_new, cache_k, cache_v, cum_pages, pages):
    bd, n_q, _ = q.shape
    n_pages = page_table.shape[1]
    seq = lambda s: pl.BlockSpec((1,) + s, lambda b, g, pt: (b, 0, 0))
    hbm = pl.BlockSpec(memory_space=pl.ANY)
    rows = n_q * FOX_HEADS
    assert bd * (n_pages // pages) >= PAGE_BUFFERS
    page_sems = pltpu.SemaphoreType.DMA((PAGE_BUFFERS,))
    return pl.pallas_call(
        functools.partial(_fox_sample_kernel, pages=pages),
        grid_spec=pltpu.PrefetchScalarGridSpec(
            num_scalar_prefetch=1,
            grid=(bd, n_pages // pages),
            in_specs=[seq((n_q, FOX_WIDTH)), seq(k_new.shape[1:]), seq(v_new.shape[1:]), seq((FOX_HEADS, PAGE)),
                      hbm, hbm, hbm],
            out_specs=pl.BlockSpec((1, n_q, FOX_WIDTH), lambda b, g, pt: (b, 0, 0)),
            scratch_shapes=[pltpu.VMEM((PAGE_BUFFERS, pages, FOX_WIDTH, PAGE), F32),
                            pltpu.VMEM((PAGE_BUFFERS, pages, FOX_WIDTH, PAGE), F32),
                            pltpu.VMEM((PAGE_BUFFERS, pages, FOX_HEADS, PAGE), F32),
                            pltpu.VMEM((rows, LANES), F32), pltpu.VMEM((rows, LANES), F32),
                            pltpu.VMEM((rows, FOX_WIDTH), F32), pltpu.VMEM((FOX_HEADS, PAGE), F32),
                            page_sems, page_sems, page_sems]),
        out_shape=jax.ShapeDtypeStruct((bd, n_q, FOX_WIDTH), BF16),
        compiler_params=_cparams(("arbitrary", "arbitrary")),
        name="fox_sample",
    )(page_table, q, k_new, v_new, cum_new, cache_k, cache_v, cum_pages)


def _hgrn2_kernel(q_ref, f_ref, i_ref, s0_ref, lbd_ref, o_ref, s_ref, st_sc, qd_sc, ke_sc, dec_sc, oi_sc, *, nb):
    c = HG_CHUNK
    tc = q_ref.shape[1]
    n_ch = tc // c
    tci = pl.program_id(1)
    chains = [(bi, h) for bi in range(nb) for h in range(HG_HEADS)]

    @pl.when(tci == 0)
    def _():
        for n, (bi, h) in enumerate(chains):
            st_sc[n] = jnp.transpose(s0_ref[bi, h])

    lbd = lbd_ref[...]
    row = lax.broadcasted_iota(I32, (tc, tc), 0)
    col = lax.broadcasted_iota(I32, (tc, tc), 1)
    causal = (row // c == col // c) & (col <= row)
    for bi in range(nb):
        f = f_ref[bi]
        kc = 1.0 - f
        hi, mid, lo = _split3(jnp.log(f))
        d = lambda a: jnp.dot(lbd, a, preferred_element_type=F32)
        g = d(hi) + d(mid) + d(lo)
        g_chunks = g.reshape(n_ch, c, HG_WIDTH)
        g_last = jnp.broadcast_to(g_chunks[:, c - 1:c, :], (n_ch, c, HG_WIDTH)).reshape(tc, HG_WIDTH)
        q_dec = (q_ref[bi] * jnp.exp(g)).astype(BF16)
        k_inv = (kc * jnp.exp(-g)).astype(BF16)
        qd_sc[bi] = q_dec
        ke_sc[bi] = (kc * jnp.exp(g_last - g)).astype(BF16)
        dec_sc[bi] = jnp.exp(g_last)
        ib = i_ref[bi].astype(BF16)
        for h in range(HG_HEADS):
            ls = slice(h * HG_DIM, (h + 1) * HG_DIM)
            a = jnp.where(causal, _dot_nt(q_dec[:, ls], k_inv[:, ls]), 0.0)
            oi_sc[bi, :, ls] = jnp.dot(a.astype(BF16), ib[:, ls], preferred_element_type=F32)

    for ci in range(n_ch):
        rs = slice(ci * c, (ci + 1) * c)
        for n, (bi, h) in enumerate(chains):
            ls = slice(h * HG_DIM, (h + 1) * HG_DIM)
            st = st_sc[n]
            o_ref[bi, rs, ls] = oi_sc[bi, rs, ls] + _dot_nt(qd_sc[bi, rs, ls], st.astype(BF16))
            update = _dot_tn(i_ref[bi, rs, ls].astype(BF16), ke_sc[bi, rs, ls])
            st_sc[n] = st * dec_sc[bi, ci * c:ci * c + 1, ls] + update

    @pl.when(tci == pl.num_programs(1) - 1)
    def _():
        for n, (bi, h) in enumerate(chains):
            s_ref[bi, h] = jnp.transpose(st_sc[n])


def _hgrn2(q, f, i, s0, tc, nb):
    b, t, _ = q.shape
    tok = pl.BlockSpec((nb, tc, HG_WIDTH), lambda g, ti: (g, ti, 0))
    st = pl.BlockSpec((nb, HG_HEADS, HG_DIM, HG_DIM), lambda g, ti: (g, 0, 0, 0))
    r = np.arange(tc)
    lbd = jnp.asarray((r[:, None] // HG_CHUNK == r[None, :] // HG_CHUNK) & (r[None, :] <= r[:, None]), BF16)
    tile = lambda dt: pltpu.VMEM((nb, tc, HG_WIDTH), dt)
    return pl.pallas_call(
        functools.partial(_hgrn2_kernel, nb=nb),
        grid=(b // nb, t // tc),
        in_specs=[tok, tok, tok, st, pl.BlockSpec((tc, tc), lambda g, ti: (0, 0))],
        out_specs=[tok, st],
        out_shape=[jax.ShapeDtypeStruct((b, t, HG_WIDTH), F32),
                   jax.ShapeDtypeStruct((b, HG_HEADS, HG_DIM, HG_DIM), F32)],
        scratch_shapes=[pltpu.VMEM((nb * HG_HEADS, HG_DIM, HG_DIM), F32), tile(BF16), tile(BF16), tile(F32), tile(F32)],
        compiler_params=_cparams(("parallel", "arbitrary")),
        name="hgrn2",
    )(q, f, i, s0, lbd)


def _post_kernel(x_ref, oa_ref, ohg_ref, hog_ref, sga_ref, sgb_ref, g1_ref, sh2_ref, sc2_ref,
                 wa_ref, wb_ref, wo_ref, ng_ref, l1g_ref, l1b_ref, wr_ref, rb_ref,
                 wsg_ref, wsu_ref, wsd_ref,
                 x1_ref, h2p_ref, idx_ref, wt_ref, shd_ref):
    tm = x_ref.shape[0]
    ohg = ohg_ref[...]
    heads = []
    for h in range(HG_HEADS):
        oh = ohg[:, h * HG_DIM:(h + 1) * HG_DIM]
        heads.append(oh * lax.rsqrt(jnp.mean(oh * oh, axis=-1, keepdims=True) + RMS_EPS))
    ob = (jnp.concatenate(heads, axis=1) * ng_ref[...] * hog_ref[...]).astype(BF16)
    mixed = (sga_ref[...] * jnp.dot(oa_ref[...], wa_ref[...], preferred_element_type=F32)
             + sgb_ref[...] * jnp.dot(ob, wb_ref[...], preferred_element_type=F32))
    y = jnp.dot(mixed.astype(BF16), wo_ref[...], preferred_element_type=F32)
    x1 = _ln(DEEPNORM_ALPHA * x_ref[...] + g1_ref[0] * y) * l1g_ref[...] + l1b_ref[...]
    x1_ref[...] = x1
    h2 = _ln(x1) * (1.0 + sc2_ref[0]) + sh2_ref[0]
    h2b = h2.astype(BF16)

    half = D_MODEL // 2
    lo_bits = lax.shift_right_logical(pltpu.bitcast(h2b[:, :half].astype(F32), I32), 16)
    hi_bits = pltpu.bitcast(h2b[:, half:].astype(F32), I32) & jnp.int32(-65536)
    packed = hi_bits | lo_bits
    for j in range(half // LANES):
        h2p_ref[pl.ds(j, tm, stride=half // LANES), :] = packed[:, j * LANES:(j + 1) * LANES]

    scores = jax.nn.sigmoid(jnp.dot(h2b, wr_ref[...], preferred_element_type=F32))
    sel = scores + rb_ref[...]
    lane_e = lax.broadcasted_iota(I32, (tm, N_EXPERTS), 1).astype(F32)
    lane_o = lax.broadcasted_iota(I32, (tm, LANES), 1)
    idx_acc = jnp.zeros((tm, LANES), F32)
    w_acc = jnp.zeros((tm, LANES), F32)
    w_sum = jnp.zeros((tm, 1), F32)
    for k in range(TOP_K):
        mx = jnp.max(sel, axis=1, keepdims=True)
        ik = jnp.min(jnp.where(sel == mx, lane_e, float(N_EXPERTS)), axis=1, keepdims=True)
        hit = lane_e == ik
        wk = jnp.sum(jnp.where(hit, scores, 0.0), axis=1, keepdims=True)
        sel = jnp.where(hit, -jnp.inf, sel)
        idx_acc = jnp.where(lane_o == k, ik, idx_acc)
        w_acc = jnp.where(lane_o == k, wk, w_acc)
        w_sum = w_sum + wk
    idx_ref[...] = idx_acc.astype(I32)
    wt_ref[...] = ROUTED_SCALE * w_acc / w_sum

    sg = jnp.dot(h2b, wsg_ref[...], preferred_element_type=F32)
    su = jnp.dot(h2b, wsu_ref[...], preferred_element_type=F32)
    shd_ref[...] = jnp.dot((_silu(sg) * su).astype(BF16), wsd_ref[...], preferred_element_type=F32)


def _post(x, oa, ohg, hog, sga, sgb, g1, sh2, sc2, tiles_per_group, wts, tm):
    n = x.shape[0]
    r = g1.shape[1]
    row = lambda w: pl.BlockSpec((tm, w), lambda i: (i, 0))
    mod = pl.BlockSpec((1, r, D_MODEL), lambda i: (i // tiles_per_group, 0, 0))
    const = lambda a: pl.BlockSpec(a.shape, lambda i: (0, 0))
    sds = lambda w, dt: jax.ShapeDtypeStruct((n, w), dt)
    return pl.pallas_call(
        _post_kernel,
        grid=(n // tm,),
        in_specs=[row(D_MODEL), row(FOX_WIDTH), row(HG_WIDTH), row(HG_WIDTH), row(D_MODEL), row(D_MODEL),
                  mod, mod, mod] + [const(a) for a in wts],
        out_specs=[row(D_MODEL), pl.BlockSpec((4 * tm, LANES), lambda i: (i, 0)), row(LANES), row(LANES),
                   row(D_MODEL)],
        out_shape=[sds(D_MODEL, F32), jax.ShapeDtypeStruct((4 * n, LANES), I32), sds(LANES, I32),
                   sds(LANES, F32), sds(D_MODEL, F32)],
        compiler_params=_cparams(("parallel",)),
        name="post",
    )(x, oa, ohg, hog, sga, sgb, g1, sh2, sc2, *wts)


def _ring_copy(src_hbm, dst, sems, blk, slot):
    return pltpu.make_async_copy(src_hbm.at[blk], dst.at[slot], sems.at[slot])


def _moe_kernel(off_ref, nv_ref, g4_hbm, s8_hbm, rw_hbm, hp_hbm, hs_hbm, wg_hbm, wu_hbm, wd_hbm, out_ref,
                hv_sc, acc_sc, xt_sc, y_sc, wg_sc, wu_sc, wd_sc, g4_sm, s8_sm, rw_sm,
                sem, gsem, ssem, rsem, wsem, *, ct, n_chunks):
    n_valid = nv_ref[0]
    streams = [(g4_hbm, g4_sm, gsem), (s8_hbm, s8_sm, ssem), (rw_hbm, rw_sm, rsem)]

    def weight_copies(e, slot):
        return [pltpu.make_async_copy(wg_hbm.at[e], wg_sc.at[slot], wsem.at[0, slot]),
                pltpu.make_async_copy(wu_hbm.at[e], wu_sc.at[slot], wsem.at[1, slot]),
                pltpu.make_async_copy(wd_hbm.at[e], wd_sc.at[slot], wsem.at[2, slot])]

    for j in range(RING_SLOTS):
        for src, dst, sems in streams:
            _ring_copy(src, dst, sems, j, j).start()
    for g0 in range(WEIGHT_SLOTS - 1):
        for cp in weight_copies(g0, g0):
            cp.start()

    def arm(j0, n_blk, wslot):
        m_rows = n_blk * MOE_ROWS
        slots = [(j0 + k) % RING_SLOTS for k in range(n_blk)]
        for k in range(n_blk):
            for src, dst, sems in streams:
                _ring_copy(src, dst, sems, j0 + k, slots[k]).wait()
        for r in range(m_rows):
            t4 = pl.multiple_of(g4_sm[slots[r // MOE_ROWS], 0, r % MOE_ROWS], 4)
            xt_sc[4 * r:4 * r + 4, :] = hv_sc[pl.ds(t4, 4), :]
        lo, hi = [], []
        for j in range(4):
            w = xt_sc[pl.ds(j, m_rows, stride=4), :]
            lo.append(pltpu.bitcast(w << 16, F32))
            hi.append(pltpu.bitcast(w & jnp.int32(-65536), F32))
        x = jnp.concatenate(lo + hi, axis=1)
        gate = jnp.dot(x, wg_sc[wslot], preferred_element_type=F32)
        up = jnp.dot(x, wu_sc[wslot], preferred_element_type=F32)
        y = jnp.dot(_silu(gate) * up, wd_sc[wslot], preferred_element_type=F32)
        tiles = D_MODEL // LANES
        for j in range(tiles):
            y_sc[pl.ds(j, m_rows, stride=tiles), :] = y[:, j * LANES:(j + 1) * LANES]
        for r0 in range(0, m_rows, SCATTER_BATCH):
            new = []
            for r in range(r0, r0 + SCATTER_BATCH):
                k, m = r // MOE_ROWS, r % MOE_ROWS
                r8 = pl.multiple_of(s8_sm[slots[k], 0, m], 8)
                new.append((r8, acc_sc[pl.ds(r8, 8), :] + rw_sm[slots[k], 0, m] * y_sc[8 * r:8 * r + 8, :]))
            for r8, val in new:
                acc_sc[pl.ds(r8, 8), :] = val
        for k in range(n_blk):

            @pl.when(j0 + k + RING_SLOTS < n_valid)
            def _():
                for src, dst, sems in streams:
                    _ring_copy(src, dst, sems, j0 + k + RING_SLOTS, slots[k]).start()

    last = n_chunks - 1
    tail_p = hp_hbm.shape[0] - last * ct * 4

    def chunk(c, carry):
        @pl.when(c < last)
        def _():
            cp = pltpu.make_async_copy(hp_hbm.at[pl.ds(pl.multiple_of(c * (ct * 4), 8), ct * 4)], hv_sc, sem.at[0])
            cp.start()
            cp.wait()

        @pl.when(c == last)
        def _():
            cps = [pltpu.make_async_copy(hs_hbm, hv_sc.at[pl.ds(tail_p, hs_hbm.shape[0])], sem.at[1])]
            if tail_p:
                cps.append(pltpu.make_async_copy(hp_hbm.at[pl.ds(last * ct * 4, tail_p)],
                                                 hv_sc.at[pl.ds(0, tail_p)], sem.at[0]))
            for cp in cps:
                cp.start()
            for cp in cps:
                cp.wait()

        acc_sc[...] = jnp.zeros(acc_sc.shape, F32)

        def expert(e, carry2):
            g = c * N_EXPERTS + e
            wslot = g % WEIGHT_SLOTS
            for wcp in weight_copies(e, wslot):
                wcp.wait()
            nxt = g + WEIGHT_SLOTS - 1

            @pl.when(nxt < n_chunks * N_EXPERTS)
            def _():
                for wcp in weight_copies(nxt % N_EXPERTS, nxt % WEIGHT_SLOTS):
                    wcp.start()

            b0 = off_ref[g]
            n_blk = off_ref[g + 1] - b0

            def pair(i, carry3):
                arm(b0 + 2 * i, 2, wslot)
                return carry3

            lax.fori_loop(0, n_blk // 2, pair, 0)

            @pl.when(n_blk % 2 == 1)
            def _():
                arm(b0 + n_blk - 1, 1, wslot)

            return carry2

        lax.fori_loop(0, N_EXPERTS, expert, 0)
        out = pltpu.make_async_copy(acc_sc.at[pl.ds(0, ct * 8)], out_ref.at[c], sem.at[0])
        out.start()
        out.wait()
        return carry

    lax.fori_loop(0, n_chunks, chunk, 0)


def _moe(blk_off, n_valid, g4_rows, s8_rows, row_w, h2p_p, h2p_s, w_gate, w_up, w_down, n_chunks, ct):
    assert n_chunks * N_EXPERTS >= WEIGHT_SLOTS
    tail_p = h2p_p.shape[0] - (n_chunks - 1) * ct * 4
    assert tail_p >= 0 and tail_p % 8 == 0 and tail_p + h2p_s.shape[0] == ct * 4
    hbm = pl.BlockSpec(memory_space=pl.ANY)
    ring_sems = pltpu.SemaphoreType.DMA((RING_SLOTS,))
    ring_i32 = pltpu.SMEM((RING_SLOTS, 1, MOE_ROWS), I32)
    m_max = 2 * MOE_ROWS
    return pl.pallas_call(
        functools.partial(_moe_kernel, ct=ct, n_chunks=n_chunks),
        grid_spec=pltpu.PrefetchScalarGridSpec(
            num_scalar_prefetch=2,
            grid=(1,),
            in_specs=[hbm] * 8,
            out_specs=pl.BlockSpec(memory_space=pl.ANY),
            scratch_shapes=[pltpu.VMEM((ct * 4, LANES), I32), pltpu.VMEM(((ct + 1) * 8, LANES), F32),
                            pltpu.VMEM((4 * m_max, LANES), I32), pltpu.VMEM((8 * m_max, LANES), F32),
                            pltpu.VMEM((WEIGHT_SLOTS, D_MODEL, EXPERT_DIM), F32),
                            pltpu.VMEM((WEIGHT_SLOTS, D_MODEL, EXPERT_DIM), F32),
                            pltpu.VMEM((WEIGHT_SLOTS, EXPERT_DIM, D_MODEL), F32),
                            ring_i32, ring_i32, pltpu.SMEM((RING_SLOTS, 1, MOE_ROWS), F32),
                            pltpu.SemaphoreType.DMA((2,)), ring_sems, ring_sems, ring_sems,
                            pltpu.SemaphoreType.DMA((3, WEIGHT_SLOTS))]),
        out_shape=jax.ShapeDtypeStruct((n_chunks, ct * 8, LANES), F32),
        compiler_params=_cparams(("arbitrary",)),
        name="moe",
    )(blk_off, n_valid, g4_rows, s8_rows, row_w, h2p_p, h2p_s, w_gate, w_up, w_down)


def _dispatch(idx, wts, n, n_chunks):
    ct = n // n_chunks
    n_pairs = n * TOP_K
    n_groups = n_chunks * N_EXPERTS
    fill = MOE_ROWS - 1
    n_rows = -(-(n_pairs + n_groups * fill) // MOE_ROWS) * MOE_ROWS
    tok = jnp.arange(n_pairs, dtype=I32) // TOP_K
    grp = (tok // ct) * N_EXPERTS + idx.reshape(n_pairs)
    gid = jnp.arange(n_groups, dtype=I32)
    counts = jnp.sum((grp[:, None] == gid[None, :]).astype(I32), axis=0)
    need = (-counts) % MOE_ROWS
    fill_key = jnp.where(jnp.arange(fill, dtype=I32)[None, :] < need[:, None], gid[:, None], n_groups)
    n_tail = n_rows - n_pairs - n_groups * fill
    keys = jnp.concatenate([grp, fill_key.reshape(-1), jnp.full((n_tail,), n_groups, I32)])
    n_fill = n_rows - n_pairs
    assert ct < TOK_FILLER and n_pairs // MOE_ROWS >= RING_SLOTS
    toks = jnp.concatenate([tok % ct, jnp.full((n_fill,), ct, I32)])
    ws = jnp.concatenate([wts.reshape(n_pairs), jnp.zeros((n_fill,), F32)])
    s_word, s_w = lax.sort((keys * (TOK_FILLER + 1) + toks, ws), num_keys=1)
    s_tok = s_word % (TOK_FILLER + 1)
    blk_cnt = (counts + need) // MOE_ROWS
    blk_off = jnp.concatenate([jnp.zeros((1,), I32), jnp.cumsum(blk_cnt).astype(I32)])
    shape3 = (n_rows // MOE_ROWS, 1, MOE_ROWS)
    gather4 = jnp.where(s_tok == ct, 0, s_tok * 4)
    scatter8 = s_tok * 8
    return blk_off, blk_off[-1:], gather4.reshape(shape3), scatter8.reshape(shape3), s_w.reshape(shape3)


def _final_kernel(x1_ref, r_ref, s_ref, g2_ref, lg_ref, lb_ref, o_ref):
    tm = x1_ref.shape[0]
    tiles = D_MODEL // LANES
    routed = jnp.concatenate([r_ref[pl.ds(j, tm, stride=tiles), :] for j in range(tiles)], axis=1)
    y = routed + s_ref[...]
    o_ref[...] = _ln(DEEPNORM_ALPHA * x1_ref[...] + g2_ref[0] * y) * lg_ref[...] + lb_ref[...]


def _final(x1, routed, row_off_tiles, shared, g2, tiles_per_group, ln_g, ln_b, tm):
    n = x1.shape[0]
    r = g2.shape[1]
    row = pl.BlockSpec((tm, D_MODEL), lambda i: (i, 0))
    return pl.pallas_call(
        _final_kernel,
        grid=(n // tm,),
        in_specs=[row, pl.BlockSpec((tm * (D_MODEL // LANES), LANES), lambda i: (i + row_off_tiles, 0)), row,
                  pl.BlockSpec((1, r, D_MODEL), lambda i: (i // tiles_per_group, 0, 0)),
                  pl.BlockSpec((1, D_MODEL), lambda i: (0, 0)), pl.BlockSpec((1, D_MODEL), lambda i: (0, 0))],
        out_specs=row,
        out_shape=jax.ShapeDtypeStruct((n, D_MODEL), F32),
        compiler_params=_cparams(("parallel",)),
        name="final",
    )(x1, routed, shared, g2, ln_g, ln_b)


def kernel(x_prompt, x_sample, c_prompt, c_sample, cache_k, cache_v, cache_logf, state_hg, page_table, w_ada, b_ada, w_in, b_fox_f, hg_lb_logits, hg_norm_g, w_branch_a, w_branch_b, w_out, ln1_g, ln1_b, w_router, router_bias, w_exp_gate, w_exp_up, w_exp_down, w_sh_gate, w_sh_up, w_sh_down, ln2_g, ln2_b):
    assert w_ada.shape[0] == DEPTH and hg_lb_logits.shape[0] == DEPTH + 1
    b, t, d = x_prompt.shape
    bd, ts, _ = x_sample.shape
    n_p, n_s = b * t, bd * ts
    n = n_p + n_s
    n_phys = cache_k.shape[1]
    n_pages = page_table.shape[1]
    tm_p = 256
    tm_s = min(256, n_s)
    tq = min(256, t)
    assert n_p % tm_p == 0 and n_s % tm_s == 0 and t % tq == 0 and t % PAGE == 0
    assert HG_CHUNK % ts == 0

    pad_f = jnp.zeros((d, LANES - FOX_HEADS), F32)
    w_all = jnp.concatenate([w_in[0][:, :OFF_F + FOX_HEADS], pad_f, w_in[0][:, OFF_F + FOX_HEADS:]], axis=1).astype(BF16)
    bf_pad = jnp.concatenate([b_fox_f[0], jnp.zeros((LANES - FOX_HEADS,), F32)]).reshape(1, LANES)
    post_w = (w_branch_a[0].astype(BF16), w_branch_b[0].astype(BF16), w_out[0].astype(BF16),
              jnp.tile(hg_norm_g[0], HG_HEADS).reshape(1, HG_WIDTH), ln1_g[0].reshape(1, d), ln1_b[0].reshape(1, d),
              w_router[0].astype(BF16), router_bias[0].reshape(1, N_EXPERTS),
              w_sh_gate[0].astype(BF16), w_sh_up[0].astype(BF16), w_sh_down[0].astype(BF16))
    r_all = b + bd
    r_pad = -(-r_all // 8) * 8
    c_all = jnp.concatenate([c_prompt, c_sample, jnp.zeros((r_pad - r_all, d), F32)], axis=0)
    mod = _adaln(c_all, w_ada[0], b_ada[0])
    mod_p = [mod[:b, i * d:(i + 1) * d].reshape(b, 1, d) for i in range(6)]
    reps = tm_s // ts
    mod_s = [jnp.repeat(mod[b:b + bd, i * d:(i + 1) * d], ts, axis=0).reshape(n_s // tm_s, tm_s, d) for i in range(6)]
    tpg_p = t // tm_p

    (q_p, kb_p, vb_p, kf_p, vf_p, lf_p, hq_p, hf_p, hi_p, hog_p, sga_p, sgb_p) = _inproj(
        x_prompt.reshape(n_p, d), mod_p[0], mod_p[1], tpg_p, w_all, bf_pad, hg_lb_logits, tm_p, seq_tiles=tpg_p)
    (q_s, kb_s, vb_s, kf_s, vf_s, lf_s, hq_s, hf_s, hi_s, hog_s, sga_s, sgb_s) = _inproj(
        x_sample.reshape(n_s, d), mod_s[0], mod_s[1], 1, w_all, bf_pad, hg_lb_logits, tm_s)

    lf_t = lf_p.reshape(b, t, FOX_HEADS).transpose(0, 2, 1)
    ck = _cum_seq(lf_t.reshape(b * FOX_HEADS * (t // PAGE), PAGE), t // PAGE).reshape(b, FOX_HEADS // 2, 2, t)
    oa_p = _fox_prompt(q_p.reshape(b, t, FOX_WIDTH), kb_p.reshape(b, t, FOX_WIDTH),
                       vb_p.reshape(b, t, FOX_WIDTH), ck, tq)

    rows_c = n_phys * FOX_HEADS
    tb = 4096 if rows_c % 4096 == 0 else rows_c
    cum_pages = _cum_rows(cache_logf[0].transpose(0, 2, 1).reshape(rows_c, PAGE), tb).reshape(n_phys, FOX_HEADS, PAGE)
    lf_new = jnp.pad(lf_s.reshape(bd, ts, FOX_HEADS).transpose(0, 2, 1), ((0, 0), (0, 0), (0, PAGE - ts)))
    cum_new = _cum_rows(lf_new.reshape(bd * FOX_HEADS, PAGE), bd * FOX_HEADS).reshape(bd, FOX_HEADS, PAGE)
    pad8 = lambda a: jnp.pad(a.reshape(bd, ts, FOX_WIDTH), ((0, 0), (0, 8 - ts), (0, 0)))
    pages = 16 if n_pages % 16 == 0 else n_pages
    k_t = cache_k[0].transpose(0, 2, 3, 1).reshape(n_phys, FOX_WIDTH, PAGE)
    v_t = cache_v[0].transpose(0, 2, 3, 1).reshape(n_phys, FOX_WIDTH, PAGE)
    oa_s = _fox_sample(page_table, q_s.reshape(bd, ts, FOX_WIDTH), pad8(kf_s), pad8(vf_s), cum_new,
                       k_t, v_t, cum_pages, pages)

    tc = min(512, t)
    nb_p = 2 if b % 2 == 0 else 1
    nb_s = 2 if bd % 2 == 0 else 1
    ohg_p, st_p = _hgrn2(hq_p.reshape(b, t, HG_WIDTH), hf_p.reshape(b, t, HG_WIDTH), hi_p.reshape(b, t, HG_WIDTH),
                         jnp.zeros((b, HG_HEADS, HG_DIM, HG_DIM), F32), tc, nb_p)
    padc = lambda a, v: jnp.pad(a.reshape(bd, ts, HG_WIDTH), ((0, 0), (0, HG_CHUNK - ts), (0, 0)), constant_values=v)
    ohg_s, st_s = _hgrn2(padc(hq_s, 0.0), padc(hf_s, 1.0), padc(hi_s, 0.0), state_hg[0], HG_CHUNK, nb_s)
    ohg_s = ohg_s[:, :ts].reshape(n_s, HG_WIDTH)

    x1_p, h2p_p, idx_p, wt_p, shd_p = _post(
        x_prompt.reshape(n_p, d), oa_p.reshape(n_p, FOX_WIDTH), ohg_p.reshape(n_p, HG_WIDTH), hog_p, sga_p, sgb_p,
        mod_p[2], mod_p[3], mod_p[4], tpg_p, post_w, tm_p)
    x1_s, h2p_s, idx_s, wt_s, shd_s = _post(
        x_sample.reshape(n_s, d), oa_s.reshape(n_s, FOX_WIDTH), ohg_s, hog_s, sga_s, sgb_s,
        mod_s[2], mod_s[3], mod_s[4], 1, post_w, tm_s)

    n_chunks = next(c for c in MOE_CHUNKS if n % (2 * c) == 0)
    idx = jnp.concatenate([idx_p[:, :TOP_K], idx_s[:, :TOP_K]], axis=0)
    wts = jnp.concatenate([wt_p[:, :TOP_K], wt_s[:, :TOP_K]], axis=0)
    blk_off, n_valid, g4_rows, s8_rows, row_w = _dispatch(idx, wts, n, n_chunks)
    routed = _moe(blk_off, n_valid, g4_rows, s8_rows, row_w, h2p_p, h2p_s, w_exp_gate[0], w_exp_up[0],
                  w_exp_down[0], n_chunks, n // n_chunks).reshape(n * (d // LANES), LANES)

    lg, lb2 = ln2_g[0].reshape(1, d), ln2_b[0].reshape(1, d)
    y_p = _final(x1_p, routed, 0, shd_p, mod_p[5], tpg_p, lg, lb2, tm_p)
    y_s = _final(x1_s, routed, n_p // tm_s, shd_s, mod_s[5], 1, lg, lb2, tm_s)

    ldt, sdt = cache_logf.dtype, state_hg.dtype
    hd = (FOX_HEADS, FOX_HEAD_DIM)
    heads_last = lambda a: a.reshape((b,) + hd + (t,)).transpose(0, 3, 1, 2)[None]
    return (y_p.reshape(b, t, d), y_s.reshape(bd, ts, d),
            heads_last(kf_p), heads_last(vf_p),
            lf_p.reshape(1, b, t, FOX_HEADS).astype(ldt), st_p[None].astype(sdt),
            kf_s.reshape((1, bd, ts) + hd), vf_s.reshape((1, bd, ts) + hd),
            lf_s.reshape(1, bd, ts, FOX_HEADS).astype(ldt), st_s[None].astype(sdt))
```

```python
import functools

import jax
import jax.numpy as jnp
import numpy as np
from jax import lax
from jax.experimental import pallas as pl
from jax.experimental.pallas import tpu as pltpu

F32 = jnp.float32
BF16 = jnp.bfloat16
I32 = jnp.int32

D_MODEL = 1024
FOX_HEADS = 8
FOX_HEAD_DIM = 64
FOX_WIDTH = FOX_HEADS * FOX_HEAD_DIM
HG_HEADS = 4
HG_DIM = 128
HG_WIDTH = HG_HEADS * HG_DIM
HG_CHUNK = 32
N_EXPERTS = 256
TOP_K = 8
EXPERT_DIM = 256
ROUTED_SCALE = 2.5
DEPTH = 1
DEEPNORM_ALPHA = (2.0 * DEPTH) ** 0.25
LN_EPS = 1e-5
RMS_EPS = 1e-6
NEG_INF = -1e30
PAGE = 128
LANES = 128
MOE_ROWS = 64
MOE_CHUNKS = (3, 2, 4, 1)
SCATTER_BATCH = 4
WEIGHT_SLOTS = 3
RING_SLOTS = 4
PAGE_BUFFERS = 3
TOK_FILLER = (1 << 15) - 1
VMEM_LIMIT = 56 * 1024 * 1024

OFF_Q, OFF_K, OFF_V, OFF_F = 0, 512, 1024, 1536
OFF_HQ, OFF_HF, OFF_HI, OFF_HOG = 1664, 2176, 2688, 3200
OFF_GA, OFF_GB, W_ALL_COLS = 3712, 4736, 5760


def _cparams(sem):
    return pltpu.CompilerParams(dimension_semantics=sem, vmem_limit_bytes=VMEM_LIMIT)


def _ln(x):
    mu = jnp.mean(x, axis=-1, keepdims=True)
    xc = x - mu
    var = jnp.mean(xc * xc, axis=-1, keepdims=True)
    return xc * lax.rsqrt(var + LN_EPS)


def _silu(x):
    return x * jax.nn.sigmoid(x)


def _split3(x):
    hi = x.astype(BF16)
    r1 = x - hi.astype(F32)
    mid = r1.astype(BF16)
    lo = (r1 - mid.astype(F32)).astype(BF16)
    return hi, mid, lo


def _dot3(x, m):
    hi, mid, lo = _split3(x)
    d = lambda a: jnp.dot(a, m, preferred_element_type=F32)
    return d(hi) + d(mid) + d(lo)


def _dot_nt(a, b):
    return lax.dot_general(a, b, (((1,), (1,)), ((), ())), preferred_element_type=F32)


def _dot_tn(a, b):
    return lax.dot_general(a, b, (((0,), (0,)), ((), ())), preferred_element_type=F32)


def _adaln_kernel(c_ref, w_ref, b_ref, o_ref):
    s = _silu(c_ref[...]).astype(BF16)
    o_ref[...] = jnp.dot(s, w_ref[...].astype(BF16), preferred_element_type=F32) + b_ref[...]


def _adaln(c, w_ada, b_ada):
    r = c.shape[0]
    tn = 1024
    return pl.pallas_call(
        _adaln_kernel,
        grid=(6 * D_MODEL // tn,),
        in_specs=[pl.BlockSpec((r, D_MODEL), lambda j: (0, 0)),
                  pl.BlockSpec((D_MODEL, tn), lambda j: (0, j)),
                  pl.BlockSpec((1, tn), lambda j: (0, j))],
        out_specs=pl.BlockSpec((r, tn), lambda j: (0, j)),
        out_shape=jax.ShapeDtypeStruct((r, 6 * D_MODEL), F32),
        compiler_params=_cparams(("parallel",)),
        name="adaln",
    )(c, w_ada, b_ada.reshape(1, -1))


def _inproj_kernel(x_ref, sh_ref, sc_ref, w_ref, bf_ref, lbl_ref,
                   q_ref, kb_ref, vb_ref, kf_ref, vf_ref, lf_ref,
                   hq_ref, hf_ref, hi_ref, hog_ref, sga_ref, sgb_ref, *, kv_transposed):
    h = _ln(x_ref[...]) * (1.0 + sc_ref[0]) + sh_ref[0]
    hb = h.astype(BF16)

    def proj(a, b):
        return jnp.dot(hb, w_ref[:, a:b], preferred_element_type=F32)

    q_ref[...] = (proj(OFF_Q, OFF_K) * (FOX_HEAD_DIM ** -0.5)).astype(BF16)
    k = proj(OFF_K, OFF_V)
    kb_ref[...] = k.astype(BF16)
    v = proj(OFF_V, OFF_F)
    vb_ref[...] = v.astype(BF16)
    if kv_transposed:
        kf_ref[0] = jnp.transpose(k)
        vf_ref[0] = jnp.transpose(v)
    else:
        kf_ref[...] = k
        vf_ref[...] = v
    logf = jax.nn.log_sigmoid(proj(OFF_F, OFF_HQ) + bf_ref[...])
    lf_ref[...] = logf[:, :FOX_HEADS]

    hq_ref[...] = _silu(proj(OFF_HQ, OFF_HF))
    l0 = lbl_ref[0:1, :]
    l1 = lbl_ref[1:2, :]
    mx = jnp.maximum(l0, l1)
    e0 = jnp.exp(l0 - mx)
    lb = e0 / (e0 + jnp.exp(l1 - mx))
    hf_ref[...] = lb + (1.0 - lb) * jax.nn.sigmoid(proj(OFF_HF, OFF_HI))
    hi_ref[...] = proj(OFF_HI, OFF_HOG)
    hog_ref[...] = _silu(proj(OFF_HOG, OFF_GA))
    sga_ref[...] = jax.nn.sigmoid(proj(OFF_GA, OFF_GB))
    sgb_ref[...] = jax.nn.sigmoid(proj(OFF_GB, W_ALL_COLS))


def _inproj(x, sh, sc, tiles_per_group, w_all, bf_pad, lbl, tm, seq_tiles=None):
    n = x.shape[0]
    r = sh.shape[1]
    row = lambda w: pl.BlockSpec((tm, w), lambda i: (i, 0))
    mod = pl.BlockSpec((1, r, D_MODEL), lambda i: (i // tiles_per_group, 0, 0))
    const = lambda s: pl.BlockSpec(s, lambda i: (0, 0))
    sds = lambda w, dt: jax.ShapeDtypeStruct((n, w), dt)
    if seq_tiles is None:
        kv_spec, kv_sds = row(FOX_WIDTH), sds(FOX_WIDTH, F32)
    else:
        kv_spec = pl.BlockSpec((1, FOX_WIDTH, tm), lambda i: (i // seq_tiles, 0, i % seq_tiles))
        kv_sds = jax.ShapeDtypeStruct((n // (tm * seq_tiles), FOX_WIDTH, tm * seq_tiles), F32)
    return pl.pallas_call(
        functools.partial(_inproj_kernel, kv_transposed=seq_tiles is not None),
        grid=(n // tm,),
        in_specs=[row(D_MODEL), mod, mod,
                  pl.BlockSpec((D_MODEL, W_ALL_COLS), lambda i: (0, 0), pipeline_mode=pl.Buffered(1)),
                  const((1, LANES)), const((2, HG_WIDTH))],
        out_specs=[row(FOX_WIDTH)] * 3 + [kv_spec] * 2 + [row(FOX_HEADS)] + [row(HG_WIDTH)] * 4 + [row(D_MODEL)] * 2,
        out_shape=[sds(FOX_WIDTH, BF16)] * 3 + [kv_sds] * 2 + [sds(FOX_HEADS, F32)]
        + [sds(HG_WIDTH, F32)] * 4 + [sds(D_MODEL, F32)] * 2,
        compiler_params=_cparams(("parallel",)),
        name="inproj",
    )(x, sh, sc, w_all, bf_pad, lbl)


def _upper_ones():
    r = np.arange(PAGE)
    return jnp.asarray(r[:, None] <= r[None, :], BF16)


def _cum_rows_kernel(x_ref, u_ref, o_ref):
    o_ref[...] = _dot3(x_ref[...], u_ref[...])


def _cum_rows(x, tb):
    n = x.shape[0]
    return pl.pallas_call(
        _cum_rows_kernel,
        grid=(n // tb,),
        in_specs=[pl.BlockSpec((tb, PAGE), lambda i: (i, 0)), pl.BlockSpec((PAGE, PAGE), lambda i: (0, 0))],
        out_specs=pl.BlockSpec((tb, PAGE), lambda i: (i, 0)),
        out_shape=jax.ShapeDtypeStruct((n, PAGE), F32),
        compiler_params=_cparams(("parallel",)),
        name="cum_rows",
    )(x, _upper_ones())


def _cum_seq_kernel(x_ref, u_ref, g_ref, o_ref):
    x = x_ref[...]
    within = _dot3(x, u_ref[...])
    tot = _dot3(x, jnp.ones((PAGE, PAGE), BF16))
    hi, mid, lo = _split3(tot)
    g = g_ref[...]
    d = lambda a: jnp.dot(g, a, preferred_element_type=F32)
    o_ref[...] = within + d(hi) + d(mid) + d(lo)


def _cum_seq(x, rows_per_seq):
    nr = x.shape[0]
    r = np.arange(nr)
    g = (r[:, None] // rows_per_seq == r[None, :] // rows_per_seq) & (r[None, :] < r[:, None])
    return pl.pallas_call(
        _cum_seq_kernel,
        out_shape=jax.ShapeDtypeStruct(x.shape, F32),
        compiler_params=pltpu.CompilerParams(vmem_limit_bytes=VMEM_LIMIT),
        name="cum_seq",
    )(x, _upper_ones(), jnp.asarray(g, BF16))


def _fox_prompt_kernel(q_ref, k_ref, v_ref, ck_ref, o_ref, m_sc, l_sc, acc_sc, cq_sc, s_sc, s2_sc, p_sc, *,
                       tq, strip):
    i = pl.program_id(2)
    q = q_ref[0]
    lo_lanes = lax.broadcasted_iota(I32, (tq, LANES), 1) < FOX_HEAD_DIM
    zero = jnp.zeros_like(q)
    qh = (jnp.where(lo_lanes, q, zero), jnp.where(lo_lanes, zero, q))
    q_off = pl.multiple_of(i * tq, tq)
    cq_rows = ck_ref[0, 0, :, pl.ds(q_off, tq)]
    for h in range(2):
        cq_sc[h] = jnp.transpose(jnp.broadcast_to(cq_rows[h:h + 1, :], (LANES, tq)))
    reps = tq // LANES
    row = lax.broadcasted_iota(I32, (strip, tq), 0)
    col = lax.broadcasted_iota(I32, (strip, tq), 1)

    m_sc[...] = jnp.full(m_sc.shape, NEG_INF, F32)
    l_sc[...] = jnp.zeros(l_sc.shape, F32)
    acc_sc[...] = jnp.zeros(acc_sc.shape, F32)

    last_tile = pl.num_programs(2) - 1

    def scores(j0, buf):
        for t in range(2):
            jt = jnp.minimum(j0 + t, last_tile)
            kt = k_ref[0, pl.ds(pl.multiple_of(jt * tq, tq), tq), :]
            for h in range(2):
                buf[t, h] = _dot_nt(qh[h], kt)

    scores(0, s_sc)

    def kv_step(j0, diags, cur, nxt):
        nt = len(diags)
        if nxt is not None:
            scores(j0 + nt, nxt)
        k_off = pl.multiple_of(j0 * tq, tq)
        vt = v_ref[0, pl.ds(k_off, nt * tq), :]
        ckt = ck_ref[0, 0, :, pl.ds(k_off, nt * tq)]
        for h in range(2):
            for r0 in range(0, tq, strip):
                rs = slice(r0, r0 + strip)
                cq_rep = jnp.concatenate([cq_sc[h, rs, :]] * reps, axis=1)
                parts = []
                for t in range(nt):
                    part = cur[t, h, rs, :] + cq_rep - ckt[h:h + 1, t * tq:(t + 1) * tq]
                    if diags[t]:
                        part = jnp.where(col <= row + r0, part, NEG_INF)
                    parts.append(part)
                logits = jnp.concatenate(parts, axis=1)
                m_prev = m_sc[h, rs, :]
                m_new = jnp.maximum(m_prev, jnp.max(logits, axis=1, keepdims=True))
                p = jnp.exp(logits - jnp.concatenate([m_new] * (nt * reps), axis=1))
                alpha = jnp.exp(m_prev - m_new)
                l_sc[h, rs, :] = alpha * l_sc[h, rs, :] + jnp.sum(p, axis=1, keepdims=True)
                acc_sc[h, rs, :] = alpha * acc_sc[h, rs, :]
                m_sc[h, rs, :] = m_new
                p_sc[h, rs, :nt * tq] = p.astype(BF16)
        for h in range(2):
            acc_sc[h] += jnp.dot(p_sc[h, :, :nt * tq], vt, preferred_element_type=F32)

    full, causal_pair, causal_one = (False, False), (False, True), (True,)

    def body(jj, c):
        kv_step(4 * jj, full, s_sc, s2_sc)
        kv_step(4 * jj + 2, full, s2_sc, s_sc)
        return c

    lax.fori_loop(0, i // 4, body, 0)
    base = (i // 4) * 4

    @pl.when(i % 4 == 0)
    def _():
        kv_step(base, causal_one, s_sc, None)

    @pl.when(i % 4 == 1)
    def _():
        kv_step(base, causal_pair, s_sc, None)

    @pl.when(i % 4 == 2)
    def _():
        kv_step(base, full, s_sc, s2_sc)
        kv_step(base + 2, causal_one, s2_sc, None)

    @pl.when(i % 4 == 3)
    def _():
        kv_step(base, full, s_sc, s2_sc)
        kv_step(base + 2, causal_pair, s2_sc, None)

    o = jnp.where(lo_lanes, acc_sc[0] / l_sc[0], acc_sc[1] / l_sc[1])
    o_ref[0] = o.astype(BF16)


def _fox_prompt(q, k, v, ck, tq):
    b, t, _ = q.shape
    pairs = FOX_HEADS // 2
    return pl.pallas_call(
        functools.partial(_fox_prompt_kernel, tq=tq, strip=32),
        grid=(b, pairs, t // tq),
        in_specs=[pl.BlockSpec((1, tq, LANES), lambda bi, p, i: (bi, i, p)),
                  pl.BlockSpec((1, t, LANES), lambda bi, p, i: (bi, 0, p)),
                  pl.BlockSpec((1, t, LANES), lambda bi, p, i: (bi, 0, p)),
                  pl.BlockSpec((1, 1, 2, t), lambda bi, p, i: (bi, p, 0, 0))],
        out_specs=pl.BlockSpec((1, tq, LANES), lambda bi, p, i: (bi, i, p)),
        out_shape=jax.ShapeDtypeStruct((b, t, FOX_WIDTH), BF16),
        scratch_shapes=[pltpu.VMEM((2, tq, LANES), F32)] * 4
        + [pltpu.VMEM((2, 2, tq, tq), F32)] * 2 + [pltpu.VMEM((2, tq, 2 * tq), BF16)],
        compiler_params=_cparams(("parallel", "parallel", "arbitrary")),
        name="fox_prompt",
    )(q, k, v, ck)


def _fox_sample_kernel(pt_ref, q_ref, kn_ref, vn_ref, cn_ref, k_hbm, v_hbm, c_hbm, o_ref,
                       kbuf, vbuf, cbuf, m_sc, l_sc, acc_sc, off_sc, ksem, vsem, csem, *, pages):
    g = pl.program_id(1)
    groups = pl.num_programs(1)
    n_q = q_ref.shape[1]
    rows = n_q * FOX_HEADS

    step = pl.program_id(0) * groups + g
    n_steps = pl.num_programs(0) * groups
    ahead = PAGE_BUFFERS - 1

    def page_copies(s, slot):
        sb, sg = s // groups, s % groups
        copies = []
        for i in range(pages):
            pg = pt_ref[sb, sg * pages + i]
            copies += [pltpu.make_async_copy(k_hbm.at[pg], kbuf.at[slot, i], ksem.at[slot]),
                       pltpu.make_async_copy(v_hbm.at[pg], vbuf.at[slot, i], vsem.at[slot]),
                       pltpu.make_async_copy(c_hbm.at[pg], cbuf.at[slot, i], csem.at[slot])]
        return copies

    @pl.when(step == 0)
    def _():
        for s in range(ahead):
            for cp in page_copies(s, s):
                cp.start()

    slot = step % PAGE_BUFFERS
    for cp in page_copies(step, slot):
        cp.wait()

    @pl.when(step + ahead < n_steps)
    def _():
        for cp in page_copies(step + ahead, (step + ahead) % PAGE_BUFFERS):
            cp.start()

    @pl.when(g == 0)
    def _():
        m_sc[...] = jnp.full(m_sc.shape, NEG_INF, F32)
        l_sc[...] = jnp.zeros(l_sc.shape, F32)
        acc_sc[...] = jnp.zeros(acc_sc.shape, F32)
        off_sc[...] = jnp.zeros(off_sc.shape, F32)

    q4 = q_ref[0].astype(F32)
    head_of_lane = lax.broadcasted_iota(I32, (FOX_HEADS, FOX_WIDTH), 1) // FOX_HEAD_DIM
    bmask = head_of_lane == lax.broadcasted_iota(I32, (FOX_HEADS, FOX_WIDTH), 0)
    wq = jnp.concatenate(
        [jnp.where(bmask, jnp.broadcast_to(q4[t:t + 1, :], (FOX_HEADS, FOX_WIDTH)), 0.0) for t in range(n_q)], axis=0)

    def update(s_all, pv_fn):
        m_prev = m_sc[...]
        m_new = jnp.maximum(m_prev, jnp.max(s_all, axis=1, keepdims=True))
        p = jnp.exp(s_all - jnp.concatenate([m_new] * (s_all.shape[1] // LANES), axis=1))
        alpha = jnp.exp(m_prev - m_new)
        l_sc[...] = alpha * l_sc[...] + jnp.sum(p, axis=1, keepdims=True)
        acc_sc[...] = jnp.concatenate([alpha] * (FOX_WIDTH // LANES), axis=1) * acc_sc[...] + pv_fn(p)
        m_sc[...] = m_new

    off = off_sc[...]
    s_list = []
    for i in range(pages):
        within = cbuf[slot, i]
        s = jnp.dot(wq, kbuf[slot, i], preferred_element_type=F32)
        s_list.append(s - jnp.concatenate([off + within] * n_q, axis=0))
        off = off + jnp.broadcast_to(within[:, PAGE - 1:PAGE], (FOX_HEADS, PAGE))
    off_sc[...] = off

    def pv_pages(p):
        pv = _dot_nt(p[:, :PAGE], vbuf[slot, 0])
        for i in range(1, pages):
            pv = pv + _dot_nt(p[:, i * PAGE:(i + 1) * PAGE], vbuf[slot, i])
        return pv

    update(jnp.concatenate(s_list, axis=1), pv_pages)

    @pl.when(g == pl.num_programs(1) - 1)
    def _():
        pad = jnp.zeros((PAGE - kn_ref.shape[1], FOX_WIDTH), F32)
        kn = jnp.concatenate([kn_ref[0], pad], axis=0)
        vn = jnp.concatenate([vn_ref[0], pad], axis=0)
        s = _dot_nt(wq, kn) - jnp.concatenate([off + cn_ref[0]] * n_q, axis=0)
        t_of_row = lax.broadcasted_iota(I32, (rows, PAGE), 0) // FOX_HEADS
        key = lax.broadcasted_iota(I32, (rows, PAGE), 1)
        update(jnp.where(key <= t_of_row, s, NEG_INF),
               lambda p: jnp.dot(p, vn, preferred_element_type=F32))
        o32 = acc_sc[...] / jnp.concatenate([l_sc[...]] * (FOX_WIDTH // LANES), axis=1)
        outs = []
        for t in range(n_q):
            blk = o32[t * FOX_HEADS:(t + 1) * FOX_HEADS, :]
            outs.append(jnp.sum(jnp.where(bmask, blk, 0.0), axis=0, keepdims=True))
        o_ref[0] = jnp.concatenate(outs, axis=0).astype(BF16)


def _fox_sample(page_table, q, k_new, v_new, cum_new, cache_k, cache_v, cum_pages, pages):
    bd, n_q, _ = q.shape
    n_pages = page_table.shape[1]
    seq = lambda s: pl.BlockSpec((1,) + s, lambda b, g, pt: (b, 0, 0))
    hbm = pl.BlockSpec(memory_space=pl.ANY)
    rows = n_q * FOX_HEADS
    assert bd * (n_pages // pages) >= PAGE_BUFFERS
    page_sems = pltpu.SemaphoreType.DMA((PAGE_BUFFERS,))
    return pl.pallas_call(
        functools.partial(_fox_sample_kernel, pages=pages),
        grid_spec=pltpu.PrefetchScalarGridSpec(
            num_scalar_prefetch=1,
            grid=(bd, n_pages // pages),
            in_specs=[seq((n_q, FOX_WIDTH)), seq(k_new.shape[1:]), seq(v_new.shape[1:]), seq((FOX_HEADS, PAGE)),
                      hbm, hbm, hbm],
            out_specs=pl.BlockSpec((1, n_q, FOX_WIDTH), lambda b, g, pt: (b, 0, 0)),
            scratch_shapes=[pltpu.VMEM((PAGE_BUFFERS, pages, FOX_WIDTH, PAGE), F32),
                            pltpu.VMEM((PAGE_BUFFERS, pages, FOX_WIDTH, PAGE), F32),
                            pltpu.VMEM((PAGE_BUFFERS, pages, FOX_HEADS, PAGE), F32),
                            pltpu.VMEM((rows, LANES), F32), pltpu.VMEM((rows, LANES), F32),
                            pltpu.VMEM((rows, FOX_WIDTH), F32), pltpu.VMEM((FOX_HEADS, PAGE), F32),
                            page_sems, page_sems, page_sems]),
        out_shape=jax.ShapeDtypeStruct((bd, n_q, FOX_WIDTH), BF16),
        compiler_params=_cparams(("arbitrary", "arbitrary")),
        name="fox_sample",
    )(page_table, q, k_new, v_new, cum_new, cache_k, cache_v, cum_pages)


def _hgrn2_kernel(q_ref, f_ref, i_ref, s0_ref, lbd_ref, o_ref, s_ref, st_sc, qd_sc, ke_sc, dec_sc, oi_sc, *, nb):
    c = HG_CHUNK
    tc = q_ref.shape[1]
    n_ch = tc // c
    tci = pl.program_id(1)
    chains = [(bi, h) for bi in range(nb) for h in range(HG_HEADS)]

    @pl.when(tci == 0)
    def _():
        for n, (bi, h) in enumerate(chains):
            st_sc[n] = jnp.transpose(s0_ref[bi, h])

    lbd = lbd_ref[...]
    row = lax.broadcasted_iota(I32, (tc, tc), 0)
    col = lax.broadcasted_iota(I32, (tc, tc), 1)
    causal = (row // c == col // c) & (col <= row)
    for bi in range(nb):
        f = f_ref[bi]
        kc = 1.0 - f
        hi, mid, lo = _split3(jnp.log(f))
        d = lambda a: jnp.dot(lbd, a, preferred_element_type=F32)
        g = d(hi) + d(mid) + d(lo)
        g_chunks = g.reshape(n_ch, c, HG_WIDTH)
        g_last = jnp.broadcast_to(g_chunks[:, c - 1:c, :], (n_ch, c, HG_WIDTH)).reshape(tc, HG_WIDTH)
        q_dec = (q_ref[bi] * jnp.exp(g)).astype(BF16)
        k_inv = (kc * jnp.exp(-g)).astype(BF16)
        qd_sc[bi] = q_dec
        ke_sc[bi] = (kc * jnp.exp(g_last - g)).astype(BF16)
        dec_sc[bi] = jnp.exp(g_last)
        ib = i_ref[bi].astype(BF16)
        for h in range(HG_HEADS):
            ls = slice(h * HG_DIM, (h + 1) * HG_DIM)
            a = jnp.where(causal, _dot_nt(q_dec[:, ls], k_inv[:, ls]), 0.0)
            oi_sc[bi, :, ls] = jnp.dot(a.astype(BF16), ib[:, ls], preferred_element_type=F32)

    for ci in range(n_ch):
        rs = slice(ci * c, (ci + 1) * c)
        for n, (bi, h) in enumerate(chains):
            ls = slice(h * HG_DIM, (h + 1) * HG_DIM)
            st = st_sc[n]
            o_ref[bi, rs, ls] = oi_sc[bi, rs, ls] + _dot_nt(qd_sc[bi, rs, ls], st.astype(BF16))
            update = _dot_tn(i_ref[bi, rs, ls].astype(BF16), ke_sc[bi, rs, ls])
            st_sc[n] = st * dec_sc[bi, ci * c:ci * c + 1, ls] + update

    @pl.when(tci == pl.num_programs(1) - 1)
    def _():
        for n, (bi, h) in enumerate(chains):
            s_ref[bi, h] = jnp.transpose(st_sc[n])


def _hgrn2(q, f, i, s0, tc, nb):
    b, t, _ = q.shape
    tok = pl.BlockSpec((nb, tc, HG_WIDTH), lambda g, ti: (g, ti, 0))
    st = pl.BlockSpec((nb, HG_HEADS, HG_DIM, HG_DIM), lambda g, ti: (g, 0, 0, 0))
    r = np.arange(tc)
    lbd = jnp.asarray((r[:, None] // HG_CHUNK == r[None, :] // HG_CHUNK) & (r[None, :] <= r[:, None]), BF16)
    tile = lambda dt: pltpu.VMEM((nb, tc, HG_WIDTH), dt)
    return pl.pallas_call(
        functools.partial(_hgrn2_kernel, nb=nb),
        grid=(b // nb, t // tc),
        in_specs=[tok, tok, tok, st, pl.BlockSpec((tc, tc), lambda g, ti: (0, 0))],
        out_specs=[tok, st],
        out_shape=[jax.ShapeDtypeStruct((b, t, HG_WIDTH), F32),
                   jax.ShapeDtypeStruct((b, HG_HEADS, HG_DIM, HG_DIM), F32)],
        scratch_shapes=[pltpu.VMEM((nb * HG_HEADS, HG_DIM, HG_DIM), F32), tile(BF16), tile(BF16), tile(F32), tile(F32)],
        compiler_params=_cparams(("parallel", "arbitrary")),
        name="hgrn2",
    )(q, f, i, s0, lbd)


def _post_kernel(x_ref, oa_ref, ohg_ref, hog_ref, sga_ref, sgb_ref, g1_ref, sh2_ref, sc2_ref,
                 wa_ref, wb_ref, wo_ref, ng_ref, l1g_ref, l1b_ref, wr_ref, rb_ref,
                 wsg_ref, wsu_ref, wsd_ref,
                 x1_ref, h2p_ref, idx_ref, wt_ref, shd_ref):
    tm = x_ref.shape[0]
    ohg = ohg_ref[...]
    heads = []
    for h in range(HG_HEADS):
        oh = ohg[:, h * HG_DIM:(h + 1) * HG_DIM]
        heads.append(oh * lax.rsqrt(jnp.mean(oh * oh, axis=-1, keepdims=True) + RMS_EPS))
    ob = (jnp.concatenate(heads, axis=1) * ng_ref[...] * hog_ref[...]).astype(BF16)
    mixed = (sga_ref[...] * jnp.dot(oa_ref[...], wa_ref[...], preferred_element_type=F32)
             + sgb_ref[...] * jnp.dot(ob, wb_ref[...], preferred_element_type=F32))
    y = jnp.dot(mixed.astype(BF16), wo_ref[...], preferred_element_type=F32)
    x1 = _ln(DEEPNORM_ALPHA * x_ref[...] + g1_ref[0] * y) * l1g_ref[...] + l1b_ref[...]
    x1_ref[...] = x1
    h2 = _ln(x1) * (1.0 + sc2_ref[0]) + sh2_ref[0]
    h2b = h2.astype(BF16)

    half = D_MODEL // 2
    lo_bits = lax.shift_right_logical(pltpu.bitcast(h2b[:, :half].astype(F32), I32), 16)
    hi_bits = pltpu.bitcast(h2b[:, half:].astype(F32), I32) & jnp.int32(-65536)
    packed = hi_bits | lo_bits
    for j in range(half // LANES):
        h2p_ref[pl.ds(j, tm, stride=half // LANES), :] = packed[:, j * LANES:(j + 1) * LANES]

    scores = jax.nn.sigmoid(jnp.dot(h2b, wr_ref[...], preferred_element_type=F32))
    sel = scores + rb_ref[...]
    lane_e = lax.broadcasted_iota(I32, (tm, N_EXPERTS), 1).astype(F32)
    lane_o = lax.broadcasted_iota(I32, (tm, LANES), 1)
    idx_acc = jnp.zeros((tm, LANES), F32)
    w_acc = jnp.zeros((tm, LANES), F32)
    w_sum = jnp.zeros((tm, 1), F32)
    for k in range(TOP_K):
        mx = jnp.max(sel, axis=1, keepdims=True)
        ik = jnp.min(jnp.where(sel == mx, lane_e, float(N_EXPERTS)), axis=1, keepdims=True)
        hit = lane_e == ik
        wk = jnp.sum(jnp.where(hit, scores, 0.0), axis=1, keepdims=True)
        sel = jnp.where(hit, -jnp.inf, sel)
        idx_acc = jnp.where(lane_o == k, ik, idx_acc)
        w_acc = jnp.where(lane_o == k, wk, w_acc)
        w_sum = w_sum + wk
    idx_ref[...] = idx_acc.astype(I32)
    wt_ref[...] = ROUTED_SCALE * w_acc / w_sum

    sg = jnp.dot(h2b, wsg_ref[...], preferred_element_type=F32)
    su = jnp.dot(h2b, wsu_ref[...], preferred_element_type=F32)
    shd_ref[...] = jnp.dot((_silu(sg) * su).astype(BF16), wsd_ref[...], preferred_element_type=F32)


def _post(x, oa, ohg, hog, sga, sgb, g1, sh2, sc2, tiles_per_group, wts, tm):
    n = x.shape[0]
    r = g1.shape[1]
    row = lambda w: pl.BlockSpec((tm, w), lambda i: (i, 0))
    mod = pl.BlockSpec((1, r, D_MODEL), lambda i: (i // tiles_per_group, 0, 0))
    const = lambda a: pl.BlockSpec(a.shape, lambda i: (0, 0))
    sds = lambda w, dt: jax.ShapeDtypeStruct((n, w), dt)
    return pl.pallas_call(
        _post_kernel,
        grid=(n // tm,),
        in_specs=[row(D_MODEL), row(FOX_WIDTH), row(HG_WIDTH), row(HG_WIDTH), row(D_MODEL), row(D_MODEL),
                  mod, mod, mod] + [const(a) for a in wts],
        out_specs=[row(D_MODEL), pl.BlockSpec((4 * tm, LANES), lambda i: (i, 0)), row(LANES), row(LANES),
                   row(D_MODEL)],
        out_shape=[sds(D_MODEL, F32), jax.ShapeDtypeStruct((4 * n, LANES), I32), sds(LANES, I32),
                   sds(LANES, F32), sds(D_MODEL, F32)],
        compiler_params=_cparams(("parallel",)),
        name="post",
    )(x, oa, ohg, hog, sga, sgb, g1, sh2, sc2, *wts)


def _ring_copy(src_hbm, dst, sems, blk, slot):
    return pltpu.make_async_copy(src_hbm.at[blk], dst.at[slot], sems.at[slot])


def _moe_kernel(off_ref, nv_ref, g4_hbm, s8_hbm, rw_hbm, hp_hbm, hs_hbm, wg_hbm, wu_hbm, wd_hbm, out_ref,
                hv_sc, acc_sc, xt_sc, y_sc, wg_sc, wu_sc, wd_sc, g4_sm, s8_sm, rw_sm,
                sem, gsem, ssem, rsem, wsem, *, ct, n_chunks):
    n_valid = nv_ref[0]
    streams = [(g4_hbm, g4_sm, gsem), (s8_hbm, s8_sm, ssem), (rw_hbm, rw_sm, rsem)]

    def weight_copies(e, slot):
        return [pltpu.make_async_copy(wg_hbm.at[e], wg_sc.at[slot], wsem.at[0, slot]),
                pltpu.make_async_copy(wu_hbm.at[e], wu_sc.at[slot], wsem.at[1, slot]),
                pltpu.make_async_copy(wd_hbm.at[e], wd_sc.at[slot], wsem.at[2, slot])]

    for j in range(RING_SLOTS):
        for src, dst, sems in streams:
            _ring_copy(src, dst, sems, j, j).start()
    for g0 in range(WEIGHT_SLOTS - 1):
        for cp in weight_copies(g0, g0):
            cp.start()

    def arm(j0, n_blk, wslot):
        m_rows = n_blk * MOE_ROWS
        slots = [(j0 + k) % RING_SLOTS for k in range(n_blk)]
        for k in range(n_blk):
            for src, dst, sems in streams:
                _ring_copy(src, dst, sems, j0 + k, slots[k]).wait()
        for r in range(m_rows):
            t4 = pl.multiple_of(g4_sm[slots[r // MOE_ROWS], 0, r % MOE_ROWS], 4)
            xt_sc[4 * r:4 * r + 4, :] = hv_sc[pl.ds(t4, 4), :]
        lo, hi = [], []
        for j in range(4):
            w = xt_sc[pl.ds(j, m_rows, stride=4), :]
            lo.append(pltpu.bitcast(w << 16, F32))
            hi.append(pltpu.bitcast(w & jnp.int32(-65536), F32))
        x = jnp.concatenate(lo + hi, axis=1)
        gate = jnp.dot(x, wg_sc[wslot], preferred_element_type=F32)
        up = jnp.dot(x, wu_sc[wslot], preferred_element_type=F32)
        y = jnp.dot(_silu(gate) * up, wd_sc[wslot], preferred_element_type=F32)
        tiles = D_MODEL // LANES
        for j in range(tiles):
            y_sc[pl.ds(j, m_rows, stride=tiles), :] = y[:, j * LANES:(j + 1) * LANES]
        for r0 in range(0, m_rows, SCATTER_BATCH):
            new = []
            for r in range(r0, r0 + SCATTER_BATCH):
                k, m = r // MOE_ROWS, r % MOE_ROWS
                r8 = pl.multiple_of(s8_sm[slots[k], 0, m], 8)
                new.append((r8, acc_sc[pl.ds(r8, 8), :] + rw_sm[slots[k], 0, m] * y_sc[8 * r:8 * r + 8, :]))
            for r8, val in new:
                acc_sc[pl.ds(r8, 8), :] = val
        for k in range(n_blk):

            @pl.when(j0 + k + RING_SLOTS < n_valid)
            def _():
                for src, dst, sems in streams:
                    _ring_copy(src, dst, sems, j0 + k + RING_SLOTS, slots[k]).start()

    last = n_chunks - 1
    tail_p = hp_hbm.shape[0] - last * ct * 4

    def chunk(c, carry):
        @pl.when(c < last)
        def _():
            cp = pltpu.make_async_copy(hp_hbm.at[pl.ds(pl.multiple_of(c * (ct * 4), 8), ct * 4)], hv_sc, sem.at[0])
            cp.start()
            cp.wait()

        @pl.when(c == last)
        def _():
            cps = [pltpu.make_async_copy(hs_hbm, hv_sc.at[pl.ds(tail_p, hs_hbm.shape[0])], sem.at[1])]
            if tail_p:
                cps.append(pltpu.make_async_copy(hp_hbm.at[pl.ds(last * ct * 4, tail_p)],
                                                 hv_sc.at[pl.ds(0, tail_p)], sem.at[0]))
            for cp in cps:
                cp.start()
            for cp in cps:
                cp.wait()

        acc_sc[...] = jnp.zeros(acc_sc.shape, F32)

        def expert(e, carry2):
            g = c * N_EXPERTS + e
            wslot = g % WEIGHT_SLOTS
            for wcp in weight_copies(e, wslot):
                wcp.wait()
            nxt = g + WEIGHT_SLOTS - 1

            @pl.when(nxt < n_chunks * N_EXPERTS)
            def _():
                for wcp in weight_copies(nxt % N_EXPERTS, nxt % WEIGHT_SLOTS):
                    wcp.start()

            b0 = off_ref[g]
            n_blk = off_ref[g + 1] - b0

            def pair(i, carry3):
                arm(b0 + 2 * i, 2, wslot)
                return carry3

            lax.fori_loop(0, n_blk // 2, pair, 0)

            @pl.when(n_blk % 2 == 1)
            def _():
                arm(b0 + n_blk - 1, 1, wslot)

            return carry2

        lax.fori_loop(0, N_EXPERTS, expert, 0)
        out = pltpu.make_async_copy(acc_sc.at[pl.ds(0, ct * 8)], out_ref.at[c], sem.at[0])
        out.start()
        out.wait()
        return carry

    lax.fori_loop(0, n_chunks, chunk, 0)


def _moe(blk_off, n_valid, g4_rows, s8_rows, row_w, h2p_p, h2p_s, w_gate, w_up, w_down, n_chunks, ct):
    assert n_chunks * N_EXPERTS >= WEIGHT_SLOTS
    tail_p = h2p_p.shape[0] - (n_chunks - 1) * ct * 4
    assert tail_p >= 0 and tail_p % 8 == 0 and tail_p + h2p_s.shape[0] == ct * 4
    hbm = pl.BlockSpec(memory_space=pl.ANY)
    ring_sems = pltpu.SemaphoreType.DMA((RING_SLOTS,))
    ring_i32 = pltpu.SMEM((RING_SLOTS, 1, MOE_ROWS), I32)
    m_max = 2 * MOE_ROWS
    return pl.pallas_call(
        functools.partial(_moe_kernel, ct=ct, n_chunks=n_chunks),
        grid_spec=pltpu.PrefetchScalarGridSpec(
            num_scalar_prefetch=2,
            grid=(1,),
            in_specs=[hbm] * 8,
            out_specs=pl.BlockSpec(memory_space=pl.ANY),
            scratch_shapes=[pltpu.VMEM((ct * 4, LANES), I32), pltpu.VMEM(((ct + 1) * 8, LANES), F32),
                            pltpu.VMEM((4 * m_max, LANES), I32), pltpu.VMEM((8 * m_max, LANES), F32),
                            pltpu.VMEM((WEIGHT_SLOTS, D_MODEL, EXPERT_DIM), F32),
                            pltpu.VMEM((WEIGHT_SLOTS, D_MODEL, EXPERT_DIM), F32),
                            pltpu.VMEM((WEIGHT_SLOTS, EXPERT_DIM, D_MODEL), F32),
                            ring_i32, ring_i32, pltpu.SMEM((RING_SLOTS, 1, MOE_ROWS), F32),
                            pltpu.SemaphoreType.DMA((2,)), ring_sems, ring_sems, ring_sems,
                            pltpu.SemaphoreType.DMA((3, WEIGHT_SLOTS))]),
        out_shape=jax.ShapeDtypeStruct((n_chunks, ct * 8, LANES), F32),
        compiler_params=_cparams(("arbitrary",)),
        name="moe",
    )(blk_off, n_valid, g4_rows, s8_rows, row_w, h2p_p, h2p_s, w_gate, w_up, w_down)


def _dispatch(idx, wts, n, n_chunks):
    ct = n // n_chunks
    n_pairs = n * TOP_K
    n_groups = n_chunks * N_EXPERTS
    fill = MOE_ROWS - 1
    n_rows = -(-(n_pairs + n_groups * fill) // MOE_ROWS) * MOE_ROWS
    tok = jnp.arange(n_pairs, dtype=I32) // TOP_K
    grp = (tok // ct) * N_EXPERTS + idx.reshape(n_pairs)
    gid = jnp.arange(n_groups, dtype=I32)
    counts = jnp.sum((grp[:, None] == gid[None, :]).astype(I32), axis=0)
    need = (-counts) % MOE_ROWS
    fill_key = jnp.where(jnp.arange(fill, dtype=I32)[None, :] < need[:, None], gid[:, None], n_groups)
    n_tail = n_rows - n_pairs - n_groups * fill
    keys = jnp.concatenate([grp, fill_key.reshape(-1), jnp.full((n_tail,), n_groups, I32)])
    n_fill = n_rows - n_pairs
    assert ct < TOK_FILLER and n_pairs // MOE_ROWS >= RING_SLOTS
    toks = jnp.concatenate([tok % ct, jnp.full((n_fill,), ct, I32)])
    ws = jnp.concatenate([wts.reshape(n_pairs), jnp.zeros((n_fill,), F32)])
    s_word, s_w = lax.sort((keys * (TOK_FILLER + 1) + toks, ws), num_keys=1)
    s_tok = s_word % (TOK_FILLER + 1)
    blk_cnt = (counts + need) // MOE_ROWS
    blk_off = jnp.concatenate([jnp.zeros((1,), I32), jnp.cumsum(blk_cnt).astype(I32)])
    shape3 = (n_rows // MOE_ROWS, 1, MOE_ROWS)
    gather4 = jnp.where(s_tok == ct, 0, s_tok * 4)
    scatter8 = s_tok * 8
    return blk_off, blk_off[-1:], gather4.reshape(shape3), scatter8.reshape(shape3), s_w.reshape(shape3)


def _final_kernel(x1_ref, r_ref, s_ref, g2_ref, lg_ref, lb_ref, o_ref):
    tm = x1_ref.shape[0]
    tiles = D_MODEL // LANES
    routed = jnp.concatenate([r_ref[pl.ds(j, tm, stride=tiles), :] for j in range(tiles)], axis=1)
    y = routed + s_ref[...]
    o_ref[...] = _ln(DEEPNORM_ALPHA * x1_ref[...] + g2_ref[0] * y) * lg_ref[...] + lb_ref[...]


def _final(x1, routed, row_off_tiles, shared, g2, tiles_per_group, ln_g, ln_b, tm):
    n = x1.shape[0]
    r = g2.shape[1]
    row = pl.BlockSpec((tm, D_MODEL), lambda i: (i, 0))
    return pl.pallas_call(
        _final_kernel,
        grid=(n // tm,),
        in_specs=[row, pl.BlockSpec((tm * (D_MODEL // LANES), LANES), lambda i: (i + row_off_tiles, 0)), row,
                  pl.BlockSpec((1, r, D_MODEL), lambda i: (i // tiles_per_group, 0, 0)),
                  pl.BlockSpec((1, D_MODEL), lambda i: (0, 0)), pl.BlockSpec((1, D_MODEL), lambda i: (0, 0))],
        out_specs=row,
        out_shape=jax.ShapeDtypeStruct((n, D_MODEL), F32),
        compiler_params=_cparams(("parallel",)),
        name="final",
    )(x1, routed, shared, g2, ln_g, ln_b)


def kernel(x_prompt, x_sample, c_prompt, c_sample, cache_k, cache_v, cache_logf, state_hg, page_table, w_ada, b_ada, w_in, b_fox_f, hg_lb_logits, hg_norm_g, w_branch_a, w_branch_b, w_out, ln1_g, ln1_b, w_router, router_bias, w_exp_gate, w_exp_up, w_exp_down, w_sh_gate, w_sh_up, w_sh_down, ln2_g, ln2_b):
    assert w_ada.shape[0] == DEPTH and hg_lb_logits.shape[0] == DEPTH + 1
    b, t, d = x_prompt.shape
    bd, ts, _ = x_sample.shape
    n_p, n_s = b * t, bd * ts
    n = n_p + n_s
    n_phys = cache_k.shape[1]
    n_pages = page_table.shape[1]
    tm_p = 256
    tm_s = min(256, n_s)
    tq = min(256, t)
    assert n_p % tm_p == 0 and n_s % tm_s == 0 and t % tq == 0 and t % PAGE == 0
    assert HG_CHUNK % ts == 0

    pad_f = jnp.zeros((d, LANES - FOX_HEADS), F32)
    w_all = jnp.concatenate([w_in[0][:, :OFF_F + FOX_HEADS], pad_f, w_in[0][:, OFF_F + FOX_HEADS:]], axis=1).astype(BF16)
    bf_pad = jnp.concatenate([b_fox_f[0], jnp.zeros((LANES - FOX_HEADS,), F32)]).reshape(1, LANES)
    post_w = (w_branch_a[0].astype(BF16), w_branch_b[0].astype(BF16), w_out[0].astype(BF16),
              jnp.tile(hg_norm_g[0], HG_HEADS).reshape(1, HG_WIDTH), ln1_g[0].reshape(1, d), ln1_b[0].reshape(1, d),
              w_router[0].astype(BF16), router_bias[0].reshape(1, N_EXPERTS),
              w_sh_gate[0].astype(BF16), w_sh_up[0].astype(BF16), w_sh_down[0].astype(BF16))
    r_all = b + bd
    r_pad = -(-r_all // 8) * 8
    c_all = jnp.concatenate([c_prompt, c_sample, jnp.zeros((r_pad - r_all, d), F32)], axis=0)
    mod = _adaln(c_all, w_ada[0], b_ada[0])
    mod_p = [mod[:b, i * d:(i + 1) * d].reshape(b, 1, d) for i in range(6)]
    reps = tm_s // ts
    mod_s = [jnp.repeat(mod[b:b + bd, i * d:(i + 1) * d], ts, axis=0).reshape(n_s // tm_s, tm_s, d) for i in range(6)]
    tpg_p = t // tm_p

    (q_p, kb_p, vb_p, kf_p, vf_p, lf_p, hq_p, hf_p, hi_p, hog_p, sga_p, sgb_p) = _inproj(
        x_prompt.reshape(n_p, d), mod_p[0], mod_p[1], tpg_p, w_all, bf_pad, hg_lb_logits, tm_p, seq_tiles=tpg_p)
    (q_s, kb_s, vb_s, kf_s, vf_s, lf_s, hq_s, hf_s, hi_s, hog_s, sga_s, sgb_s) = _inproj(
        x_sample.reshape(n_s, d), mod_s[0], mod_s[1], 1, w_all, bf_pad, hg_lb_logits, tm_s)

    lf_t = lf_p.reshape(b, t, FOX_HEADS).transpose(0, 2, 1)
    ck = _cum_seq(lf_t.reshape(b * FOX_HEADS * (t // PAGE), PAGE), t // PAGE).reshape(b, FOX_HEADS // 2, 2, t)
    oa_p = _fox_prompt(q_p.reshape(b, t, FOX_WIDTH), kb_p.reshape(b, t, FOX_WIDTH),
                       vb_p.reshape(b, t, FOX_WIDTH), ck, tq)

    rows_c = n_phys * FOX_HEADS
    tb = 4096 if rows_c % 4096 == 0 else rows_c
    cum_pages = _cum_rows(cache_logf[0].transpose(0, 2, 1).reshape(rows_c, PAGE), tb).reshape(n_phys, FOX_HEADS, PAGE)
    lf_new = jnp.pad(lf_s.reshape(bd, ts, FOX_HEADS).transpose(0, 2, 1), ((0, 0), (0, 0), (0, PAGE - ts)))
    cum_new = _cum_rows(lf_new.reshape(bd * FOX_HEADS, PAGE), bd * FOX_HEADS).reshape(bd, FOX_HEADS, PAGE)
    pad8 = lambda a: jnp.pad(a.reshape(bd, ts, FOX_WIDTH), ((0, 0), (0, 8 - ts), (0, 0)))
    pages = 16 if n_pages % 16 == 0 else n_pages
    k_t = cache_k[0].transpose(0, 2, 3, 1).reshape(n_phys, FOX_WIDTH, PAGE)
    v_t = cache_v[0].transpose(0, 2, 3, 1).reshape(n_phys, FOX_WIDTH, PAGE)
    oa_s = _fox_sample(page_table, q_s.reshape(bd, ts, FOX_WIDTH), pad8(kf_s), pad8(vf_s), cum_new,
                       k_t, v_t, cum_pages, pages)

    tc = min(512, t)
    nb_p = 2 if b % 2 == 0 else 1
    nb_s = 2 if bd % 2 == 0 else 1
    ohg_p, st_p = _hgrn2(hq_p.reshape(b, t, HG_WIDTH), hf_p.reshape(b, t, HG_WIDTH), hi_p.reshape(b, t, HG_WIDTH),
                         jnp.zeros((b, HG_HEADS, HG_DIM, HG_DIM), F32), tc, nb_p)
    padc = lambda a, v: jnp.pad(a.reshape(bd, ts, HG_WIDTH), ((0, 0), (0, HG_CHUNK - ts), (0, 0)), constant_values=v)
    ohg_s, st_s = _hgrn2(padc(hq_s, 0.0), padc(hf_s, 1.0), padc(hi_s, 0.0), state_hg[0], HG_CHUNK, nb_s)
    ohg_s = ohg_s[:, :ts].reshape(n_s, HG_WIDTH)

    x1_p, h2p_p, idx_p, wt_p, shd_p = _post(
        x_prompt.reshape(n_p, d), oa_p.reshape(n_p, FOX_WIDTH), ohg_p.reshape(n_p, HG_WIDTH), hog_p, sga_p, sgb_p,
        mod_p[2], mod_p[3], mod_p[4], tpg_p, post_w, tm_p)
    x1_s, h2p_s, idx_s, wt_s, shd_s = _post(
        x_sample.reshape(n_s, d), oa_s.reshape(n_s, FOX_WIDTH), ohg_s, hog_s, sga_s, sgb_s,
        mod_s[2], mod_s[3], mod_s[4], 1, post_w, tm_s)

    n_chunks = next(c for c in MOE_CHUNKS if n % (2 * c) == 0)
    idx = jnp.concatenate([idx_p[:, :TOP_K], idx_s[:, :TOP_K]], axis=0)
    wts = jnp.concatenate([wt_p[:, :TOP_K], wt_s[:, :TOP_K]], axis=0)
    blk_off, n_valid, g4_rows, s8_rows, row_w = _dispatch(idx, wts, n, n_chunks)
    routed = _moe(blk_off, n_valid, g4_rows, s8_rows, row_w, h2p_p, h2p_s, w_exp_gate[0], w_exp_up[0],
                  w_exp_down[0], n_chunks, n // n_chunks).reshape(n * (d // LANES), LANES)

    lg, lb2 = ln2_g[0].reshape(1, d), ln2_b[0].reshape(1, d)
    y_p = _final(x1_p, routed, 0, shd_p, mod_p[5], tpg_p, lg, lb2, tm_p)
    y_s = _final(x1_s, routed, n_p // tm_s, shd_s, mod_s[5], 1, lg, lb2, tm_s)

    ldt, sdt = cache_logf.dtype, state_hg.dtype
    hd = (FOX_HEADS, FOX_HEAD_DIM)
    heads_last = lambda a: a.reshape((b,) + hd + (t,)).transpose(0, 3, 1, 2)[None]
    return (y_p.reshape(b, t, d), y_s.reshape(bd, ts, d),
            heads_last(kf_p), heads_last(vf_p),
            lf_p.reshape(1, b, t, FOX_HEADS).astype(ldt), st_p[None].astype(sdt),
            kf_s.reshape((1, bd, ts) + hd), vf_s.reshape((1, bd, ts) + hd),
            lf_s.reshape(1, bd, ts, FOX_HEADS).astype(ldt), st_s[None].astype(sdt))
```
